```python
import jax, jax.numpy as jnp
from jax import lax
import numpy as np

D_MODEL = 1024
BATCH = 32
SEQ = 256
DEPTH = 2
DEC_BATCH = 4
DEC_SEQ = 4096
PAST_LEN = 512

GRID_W = 64
HEAD_DIM = 64
BRANCH_WIDTH = D_MODEL // 2
A_HEADS = BRANCH_WIDTH // HEAD_DIM
A_KV_HEADS = A_HEADS // 4
A_GROUP = A_HEADS // A_KV_HEADS
B_GROUPS = 4
B_GROUP_WIDTH = BRANCH_WIDTH // B_GROUPS
POOL_WINDOWS = (2, 4, 8, 16)
C_HEADS = BRANCH_WIDTH // HEAD_DIM
C_WIDTH = C_HEADS * HEAD_DIM
NA_ROWS = 8
NA_COLS = 16
D_HEADS = 8
D_NOPE = 64
D_ROPE = 32
D_V = 64
D_WIDTH = D_HEADS * D_V
D_Q_LORA = 384
D_KV_LORA = 256
Q_BLOCK = 128
ROPE_THETA = 10000.0
EPS = 1e-6
N_EVEN = (DEPTH + 1) // 2
N_ODD = DEPTH // 2
EVEN_IN_SIZES = (A_HEADS * HEAD_DIM, A_KV_HEADS * HEAD_DIM, A_KV_HEADS * HEAD_DIM, BRANCH_WIDTH, BRANCH_WIDTH, BRANCH_WIDTH)
ODD_IN_SIZES = (C_WIDTH, C_WIDTH, C_WIDTH, C_WIDTH, D_Q_LORA, D_KV_LORA, D_ROPE, D_WIDTH)
EVEN_IN = sum(EVEN_IN_SIZES)
ODD_IN = sum(ODD_IN_SIZES)

kernel_name = 'hybrid_flow_prefix_trunk_step'


def split_last(z, sizes):
    out, off = [], 0
    for n in sizes:
        out.append(z[..., off:off + n])
        off += n
    return out


def rms_norm(x, g):
    xf = x.astype(jnp.float32)
    y = xf * lax.rsqrt(jnp.mean(xf * xf, axis=-1, keepdims=True) + EPS)
    return (y * g.astype(jnp.float32)).astype(x.dtype)


def adaln(cond, w, b):
    m = (jax.nn.silu(cond) @ w + b)[:, None, :]
    return split_last(m, (D_MODEL, D_MODEL, D_MODEL))


def pre_norm(x, g, mod):
    shift, scale, _ = mod
    return rms_norm(x, g) * (1 + scale) + shift


def post_residual(x, y, g, mod):
    return x + mod[2] * rms_norm(y, g)


def _rotate(x, ang):
    n = x.shape[-1]
    cos = jnp.cos(ang)[None, :, None, :]
    sin = jnp.sin(ang)[None, :, None, :]
    x1, x2 = x[..., :n // 2], x[..., n // 2:]
    return jnp.concatenate([x1 * cos - x2 * sin, x2 * cos + x1 * sin], axis=-1)


def axial_rope(x):
    S, R = x.shape[1], x.shape[-1]
    half = R // 2
    t = jnp.arange(S)
    inv = ROPE_THETA ** (-jnp.arange(0, half, 2, dtype=jnp.float32) / half)
    xf = x.astype(jnp.float32)
    out_r = _rotate(xf[..., :half], (t // GRID_W).astype(jnp.float32)[:, None] * inv)
    out_c = _rotate(xf[..., half:], (t % GRID_W).astype(jnp.float32)[:, None] * inv)
    return jnp.concatenate([out_r, out_c], axis=-1).astype(x.dtype)


def blocked_attention(q, k, v):
    B, S, KV, G, Dk = q.shape
    nb = S // Q_BLOCK
    scale = Dk ** -0.5
    qb = q.reshape(B, nb, Q_BLOCK, KV, G, Dk).swapaxes(0, 1)

    def one(qblk):
        s = jnp.einsum('bqkgd,bnkd->bkgqn', qblk, k, preferred_element_type=jnp.float32) * scale
        p = jax.nn.softmax(s, axis=-1).astype(v.dtype)
        return jnp.einsum('bkgqn,bnkd->bqkgd', p, v)

    o = lax.map(one, qb)
    return o.swapaxes(0, 1).reshape(B, S, KV, G, v.shape[-1])


def multiscale_pool(u, b_map, b_scale):
    B, S, _ = u.shape
    ug = u.reshape(B, S, B_GROUPS, B_GROUP_WIDTH)
    csum = jnp.pad(jnp.cumsum(ug.astype(jnp.float32), axis=1), ((0, 0), (1, 0), (0, 0), (0, 0)))
    win = jnp.array(POOL_WINDOWS, dtype=jnp.int32)[None, :]
    t = jnp.arange(S)[:, None]
    lo = jnp.clip(t - win // 2, 0, S)
    hi = jnp.clip(t + win - win // 2, 0, S)
    g = jnp.arange(B_GROUPS)[None, :]
    mean = (csum[:, hi, g] - csum[:, lo, g]) / (hi - lo).astype(jnp.float32)[None, :, :, None]
    diff = (mean - ug.astype(jnp.float32)).astype(u.dtype)
    y = jnp.einsum('bsgc,gcd->bsgd', diff, b_map)
    return y.reshape(B, S, BRANCH_WIDTH) * b_scale


def neighborhood_attention(q, k, v, k_ctx, v_ctx, rpb):
    B, S, H, D = q.shape
    L = k_ctx.shape[1]
    rows = S // GRID_W
    wr = min(NA_ROWS, rows)
    nk = wr * NA_COLS
    t = jnp.arange(S)
    qr, qc = t // GRID_W, t % GRID_W
    r0 = jnp.clip(qr - wr // 2, 0, rows - wr)
    c0 = jnp.clip(qc - NA_COLS // 2, 0, GRID_W - NA_COLS)
    kr = r0[:, None, None] + jnp.arange(wr)[None, :, None]
    kc = c0[:, None, None] + jnp.arange(NA_COLS)[None, None, :]
    idx = (kr * GRID_W + kc).reshape(S, nk)
    dr = jnp.broadcast_to(kr - qr[:, None, None] + NA_ROWS - 1, (S, wr, NA_COLS)).reshape(S, nk)
    dc = jnp.broadcast_to(kc - qc[:, None, None] + NA_COLS - 1, (S, wr, NA_COLS)).reshape(S, nk)
    nb = S // Q_BLOCK
    scale = D ** -0.5
    qb = q.reshape(B, nb, Q_BLOCK, H, D).swapaxes(0, 1)

    def one(args):
        qblk, idx_b, dr_b, dc_b = args
        kn = k[:, idx_b]
        vn = v[:, idx_b]
        s_nb = jnp.einsum('bqhd,bqnhd->bhqn', qblk, kn, preferred_element_type=jnp.float32) * scale
        s_nb = s_nb + rpb[:, dr_b, dc_b].astype(jnp.float32)
        s_cx = jnp.einsum('bqhd,blhd->bhql', qblk, k_ctx, preferred_element_type=jnp.float32) * scale
        p = jax.nn.softmax(jnp.concatenate([s_cx, s_nb], axis=-1), axis=-1).astype(v.dtype)
        return (jnp.einsum('bhql,blhd->bqhd', p[..., :L], v_ctx)
                + jnp.einsum('bhqn,bqnhd->bqhd', p[..., L:], vn))

    o = lax.map(one, (qb, idx.reshape(nb, Q_BLOCK, nk), dr.reshape(nb, Q_BLOCK, nk), dc.reshape(nb, Q_BLOCK, nk)))
    return o.swapaxes(0, 1).reshape(B, S, H, D)


def even_mixer(h, ctx_k, ctx_v, w_in, q_norm, k_norm, b_map, b_scale, w_out):
    B, S, _ = h.shape
    q, k, v, ga, ub, gb = split_last(h @ w_in, EVEN_IN_SIZES)
    q = rms_norm(q.reshape(B, S, A_HEADS, HEAD_DIM), q_norm)
    k = rms_norm(k.reshape(B, S, A_KV_HEADS, HEAD_DIM), k_norm)
    v = v.reshape(B, S, A_KV_HEADS, HEAD_DIM)
    if ctx_k is None:
        k_all, v_all = k, v
    else:
        q, k = axial_rope(q), axial_rope(k)
        k_all = jnp.concatenate([ctx_k, k], axis=1)
        v_all = jnp.concatenate([ctx_v, v], axis=1)
    o_a = blocked_attention(q.reshape(B, S, A_KV_HEADS, A_GROUP, HEAD_DIM), k_all, v_all).reshape(B, S, BRANCH_WIDTH)
    o_b = multiscale_pool(ub, b_map, b_scale)
    y = jnp.concatenate([o_a * jax.nn.silu(ga), o_b * jax.nn.silu(gb)], axis=-1) @ w_out
    return y, k, v


def odd_mixer(h, ctx_ck, ctx_cv, ctx_ckv, ctx_kpe, w_in, rpb, q_norm, w_uq, kv_norm, w_ukv, w_out):
    B, S, _ = h.shape
    qc, kc, vc, gc, cq, ckv, kpe, gd = split_last(h @ w_in, ODD_IN_SIZES)
    qc = qc.reshape(B, S, C_HEADS, HEAD_DIM)
    kc = kc.reshape(B, S, C_HEADS, HEAD_DIM)
    vc = vc.reshape(B, S, C_HEADS, HEAD_DIM)
    qd = (rms_norm(cq, q_norm) @ w_uq).reshape(B, S, D_HEADS, D_NOPE + D_ROPE)
    q_nope, q_pe = qd[..., :D_NOPE], qd[..., D_NOPE:]
    ckv = rms_norm(ckv, kv_norm)
    if ctx_ck is None:
        o_c = blocked_attention(qc[:, :, :, None, :], kc, vc)[:, :, :, 0, :]
        ckv_all, kpe_all = ckv, kpe
    else:
        o_c = neighborhood_attention(qc, kc, vc, ctx_ck, ctx_cv, rpb)
        q_pe = axial_rope(q_pe)
        kpe_rot = axial_rope(kpe[:, :, None, :])[:, :, 0, :]
        ckv_all = jnp.concatenate([ctx_ckv, ckv], axis=1)
        kpe_all = jnp.concatenate([ctx_kpe, kpe_rot], axis=1)
    N = ckv_all.shape[1]
    kv = (ckv_all @ w_ukv).reshape(B, N, D_HEADS, D_NOPE + D_V)
    k_d = jnp.concatenate([kv[..., :D_NOPE], jnp.broadcast_to(kpe_all[:, :, None, :], (B, N, D_HEADS, D_ROPE))], axis=-1)
    v_d = kv[..., D_NOPE:]
    q_d = jnp.concatenate([q_nope, q_pe], axis=-1)
    o_d = blocked_attention(q_d[:, :, :, None, :], k_d, v_d)[:, :, :, 0, :].reshape(B, S, D_WIDTH)
    o_c = o_c.reshape(B, S, C_WIDTH)
    y = jnp.concatenate([o_c * jax.nn.silu(gc), o_d * jax.nn.silu(gd)], axis=-1) @ w_out
    return y, kc, vc, ckv, kpe


def setup_inputs(seed: int = 0) -> dict:
    key = jax.random.key(seed)
    ks = jax.random.split(key, 27)
    f32 = jnp.float32

    def nrm(k, shape, scale):
        return jax.random.normal(k, shape, f32) * scale

    return {
        'x_prompt': nrm(ks[0], (BATCH, SEQ, D_MODEL), 1.0),
        'x_sample': nrm(ks[1], (DEC_BATCH, DEC_SEQ, D_MODEL), 1.0),
        'cache_a_k': nrm(ks[2], (DEC_BATCH, N_EVEN, PAST_LEN, A_KV_HEADS, HEAD_DIM), 1.0),
        'cache_a_v': nrm(ks[3], (DEC_BATCH, N_EVEN, PAST_LEN, A_KV_HEADS, HEAD_DIM), 1.0),
        'cache_c_k': nrm(ks[4], (DEC_BATCH, N_ODD, PAST_LEN, C_HEADS, HEAD_DIM), 1.0),
        'cache_c_v': nrm(ks[5], (DEC_BATCH, N_ODD, PAST_LEN, C_HEADS, HEAD_DIM), 1.0),
        'cache_d_ckv': nrm(ks[6], (DEC_BATCH, N_ODD, PAST_LEN, D_KV_LORA), 1.0),
        'cache_d_kpe': nrm(ks[7], (DEC_BATCH, N_ODD, PAST_LEN, D_ROPE), 1.0),
        'c': nrm(ks[8], (DEC_BATCH, D_MODEL), 1.0),
        'c_ctx': nrm(ks[9], (D_MODEL,), 1.0),
        'w_mod': nrm(ks[10], (DEPTH, D_MODEL, 3 * D_MODEL), D_MODEL ** -0.5),
        'b_mod': nrm(ks[11], (DEPTH, 3 * D_MODEL), 0.02),
        'g_pre': 1.0 + nrm(ks[12], (DEPTH, D_MODEL), 0.05),
        'g_post': 1.0 + nrm(ks[13], (DEPTH, D_MODEL), 0.05),
        'w_in_e': nrm(ks[14], (N_EVEN, D_MODEL, EVEN_IN), D_MODEL ** -0.5),
        'a_q_norm': 1.0 + nrm(ks[15], (N_EVEN, HEAD_DIM), 0.05),
        'a_k_norm': 1.0 + nrm(ks[16], (N_EVEN, HEAD_DIM), 0.05),
        'b_map': nrm(ks[17], (N_EVEN, B_GROUPS, B_GROUP_WIDTH, B_GROUP_WIDTH), B_GROUP_WIDTH ** -0.5),
        'b_scale': 1.0 + nrm(ks[18], (N_EVEN, BRANCH_WIDTH), 0.1),
        'w_out_e': nrm(ks[19], (N_EVEN, 2 * BRANCH_WIDTH, D_MODEL), (2 * BRANCH_WIDTH) ** -0.5),
        'w_in_o': nrm(ks[20], (N_ODD, D_MODEL, ODD_IN), D_MODEL ** -0.5),
        'c_rpb': nrm(ks[21], (N_ODD, C_HEADS, 2 * NA_ROWS - 1, 2 * NA_COLS - 1), 0.2),
        'd_q_norm': 1.0 + nrm(ks[22], (N_ODD, D_Q_LORA), 0.05),
        'd_w_uq': nrm(ks[23], (N_ODD, D_Q_LORA, D_HEADS * (D_NOPE + D_ROPE)), D_Q_LORA ** -0.5),
        'd_kv_norm': 1.0 + nrm(ks[24], (N_ODD, D_KV_LORA), 0.05),
        'd_w_ukv': nrm(ks[25], (N_ODD, D_KV_LORA, D_HEADS * (D_NOPE + D_V)), D_KV_LORA ** -0.5),
        'w_out_o': nrm(ks[26], (N_ODD, C_WIDTH + D_WIDTH, D_MODEL), (C_WIDTH + D_WIDTH) ** -0.5),
    }


def reference(x_prompt, x_sample, cache_a_k, cache_a_v, cache_c_k, cache_c_v, cache_d_ckv, cache_d_kpe,
              c, c_ctx, w_mod, b_mod, g_pre, g_post, w_in_e, a_q_norm, a_k_norm, b_map, b_scale, w_out_e,
              w_in_o, c_rpb, d_q_norm, d_w_uq, d_kv_norm, d_w_ukv, w_out_o):
    xp, xs = x_prompt, x_sample
    new_a_k, new_a_v, new_c_k, new_c_v, new_d_ckv, new_d_kpe = [], [], [], [], [], []
    for layer in range(DEPTH):
        j = layer // 2
        mod_p = adaln(c_ctx[None, :], w_mod[layer], b_mod[layer])
        mod_s = adaln(c, w_mod[layer], b_mod[layer])
        hp = pre_norm(xp, g_pre[layer], mod_p)
        hs = pre_norm(xs, g_pre[layer], mod_s)
        if layer % 2 == 0:
            yp, k_a, v_a = even_mixer(hp, None, None, w_in_e[j], a_q_norm[j], a_k_norm[j],
                                      b_map[j], b_scale[j], w_out_e[j])
            ys = even_mixer(hs, cache_a_k[:, j], cache_a_v[:, j], w_in_e[j], a_q_norm[j], a_k_norm[j],
                            b_map[j], b_scale[j], w_out_e[j])[0]
            new_a_k.append(k_a)
            new_a_v.append(v_a)
        else:
            yp, k_c, v_c, ckv, kpe = odd_mixer(hp, None, None, None, None, w_in_o[j], c_rpb[j], d_q_norm[j],
                                               d_w_uq[j], d_kv_norm[j], d_w_ukv[j], w_out_o[j])
            ys = odd_mixer(hs, cache_c_k[:, j], cache_c_v[:, j], cache_d_ckv[:, j], cache_d_kpe[:, j],
                           w_in_o[j], c_rpb[j], d_q_norm[j], d_w_uq[j], d_kv_norm[j], d_w_ukv[j], w_out_o[j])[0]
            new_c_k.append(k_c)
            new_c_v.append(v_c)
            new_d_ckv.append(ckv)
            new_d_kpe.append(kpe)
        xp = post_residual(xp, yp, g_post[layer], mod_p)
        xs = post_residual(xs, ys, g_post[layer], mod_s)
    return (xp, xs, jnp.stack(new_a_k, axis=1), jnp.stack(new_a_v, axis=1), jnp.stack(new_c_k, axis=1),
            jnp.stack(new_c_v, axis=1), jnp.stack(new_d_ckv, axis=1), jnp.stack(new_d_kpe, axis=1))
```

```python
import functools

import jax
import jax.numpy as jnp
from jax import lax
from jax.experimental import pallas as pl
from jax.experimental.pallas import tpu as pltpu

F32 = jnp.float32
BF16 = jnp.bfloat16

LANES = 128
GRID_W = 64
HEAD_DIM = 64
NA_ROWS = 8
NA_COLS = 16
POOL_WINDOWS = (2, 4, 8, 16)
ROPE_THETA = 10000.0
EPS = 1e-6
NEG = -1e30
VMEM_LIMIT = 56 * 1024 * 1024


def _params(sem):
    return pltpu.CompilerParams(dimension_semantics=sem, vmem_limit_bytes=VMEM_LIMIT)


def _silu(x):
    return x / (1.0 + jnp.exp(-x))


def _mod_kernel(c_ref, w_ref, b_ref, o_ref):
    s = _silu(c_ref[...]).astype(BF16)
    o_ref[...] = jnp.dot(s, w_ref[...].astype(BF16), preferred_element_type=F32) + b_ref[...]


def _mod_call(cond, w_mod, b_mod):
    depth, d, n = w_mod.shape
    rows = cond.shape[0]
    tn = 1024
    return pl.pallas_call(
        _mod_kernel,
        grid=(depth, n // tn),
        in_specs=[
            pl.BlockSpec((rows, d), lambda l, j: (0, 0)),
            pl.BlockSpec((None, d, tn), lambda l, j: (l, 0, j)),
            pl.BlockSpec((None, 1, tn), lambda l, j: (l, 0, j)),
        ],
        out_specs=pl.BlockSpec((None, rows, tn), lambda l, j: (l, 0, j)),
        out_shape=jax.ShapeDtypeStruct((depth, rows, n), F32),
        compiler_params=_params(("arbitrary", "arbitrary")),
        name="adaln_mod",
    )(cond, w_mod, b_mod.reshape(depth, 1, n))


def _norm_mm_kernel(x_ref, g_ref, sc_ref, sh_ref, w_ref, *o_refs, segs, do_norm):
    x = x_ref[...].astype(F32)
    if do_norm:
        ms = jnp.mean(x * x, axis=-1, keepdims=True)
        x = x * lax.rsqrt(ms + EPS) * g_ref[...]
        x = x * (1.0 + sc_ref[...]) + sh_ref[...]
    z = jnp.dot(x.astype(BF16), w_ref[...], preferred_element_type=F32)
    for o_ref, (start, width, mul) in zip(o_refs, segs):
        piece = z[:, start:start + width]
        if mul != 1.0:
            piece = piece * mul
        o_ref[...] = piece.astype(o_ref.dtype)


def _norm_mm(x, g, scale, shift, w, segs, rows_per_mod, do_norm=True, tm=256, name="norm_mm"):
    t, k = x.shape
    n = w.shape[1]
    tm = min(tm, t)
    kern = functools.partial(_norm_mm_kernel, segs=tuple((s, wd, m) for s, wd, _, m in segs),
                             do_norm=do_norm)
    mod_map = lambda i: ((i * tm) // rows_per_mod, 0, 0)
    return pl.pallas_call(
        kern,
        grid=(t // tm,),
        in_specs=[
            pl.BlockSpec((tm, k), lambda i: (i, 0)),
            pl.BlockSpec((1, k), lambda i: (0, 0)),
            pl.BlockSpec((None, 1, k), mod_map),
            pl.BlockSpec((None, 1, k), mod_map),
            pl.BlockSpec((k, n), lambda i: (0, 0)),
        ],
        out_specs=[pl.BlockSpec((tm, wd), lambda i: (i, 0)) for _, wd, _, _ in segs],
        out_shape=[jax.ShapeDtypeStruct((t, wd), dt) for _, wd, dt, _ in segs],
        compiler_params=_params(("arbitrary",)),
        name=name,
    )(x, g, scale, shift, w)


def _head_prep_kernel(x_ref, g_ref, cos_ref, sa_ref, sb_ref, o_ref, *, do_norm, rope_shift, mul):
    tm, w = x_ref.shape
    lane = lax.broadcasted_iota(jnp.int32, (tm, LANES), 1)
    lo = lane < HEAD_DIM
    for c in range(w // LANES):
        x = x_ref[:, c * LANES:(c + 1) * LANES].astype(F32)
        if do_norm:
            x2 = x * x
            s_all = jnp.sum(x2, axis=-1, keepdims=True)
            s_lo = jnp.sum(jnp.where(lo, x2, 0.0), axis=-1, keepdims=True)
            ms = jnp.where(lo, s_lo, s_all - s_lo) * (1.0 / HEAD_DIM)
            x = x * lax.rsqrt(ms + EPS) * g_ref[...]
        if rope_shift:
            x = (x * cos_ref[...]
                 + pltpu.roll(x, LANES - rope_shift, 1) * sa_ref[...]
                 + pltpu.roll(x, rope_shift, 1) * sb_ref[...])
        if mul != 1.0:
            x = x * mul
        o_ref[:, c * LANES:(c + 1) * LANES] = x.astype(o_ref.dtype)


def _head_prep(x, gain128, tables, seq, out_dtype, do_norm, rope_shift, mul=1.0, tm=256):
    t, w = x.shape
    tm = min(tm, seq)
    nseq = seq // tm
    cos, sa, sb = tables
    kern = functools.partial(_head_prep_kernel, do_norm=do_norm, rope_shift=rope_shift, mul=mul)
    tab_spec = pl.BlockSpec((tm, LANES), lambda i: (i % nseq, 0))
    return pl.pallas_call(
        kern,
        grid=(t // tm,),
        in_specs=[
            pl.BlockSpec((tm, w), lambda i: (i, 0)),
            pl.BlockSpec((1, LANES), lambda i: (0, 0)),
            tab_spec, tab_spec, tab_spec,
        ],
        out_specs=pl.BlockSpec((tm, w), lambda i: (i, 0)),
        out_shape=jax.ShapeDtypeStruct((t, w), out_dtype),
        compiler_params=_params(("arbitrary",)),
        name="head_prep",
    )(x, gain128, cos, sa, sb)


def _axial_tables(seq, dims, lane0):
    half = dims // 2
    quarter = half // 2
    inv = ROPE_THETA ** (-jnp.arange(0, half, 2, dtype=F32) / half)
    t = jnp.arange(seq)
    ang_r = (t // GRID_W).astype(F32)[:, None] * inv
    ang_c = (t % GRID_W).astype(F32)[:, None] * inv
    cos = jnp.concatenate([jnp.cos(ang_r)] * 2 + [jnp.cos(ang_c)] * 2, axis=-1)
    sin = jnp.concatenate([jnp.sin(ang_r)] * 2 + [jnp.sin(ang_c)] * 2, axis=-1)
    first = (jnp.arange(dims) % half) < quarter
    sa = jnp.where(first, -sin, 0.0)
    sb = jnp.where(first, 0.0, sin)
    if dims == HEAD_DIM:
        return tuple(jnp.concatenate([a, a], axis=-1) for a in (cos, sa, sb))
    pad_l, pad_r = lane0, LANES - lane0 - dims
    cos = jnp.pad(cos, ((0, 0), (pad_l, pad_r)), constant_values=1.0)
    sa = jnp.pad(sa, ((0, 0), (pad_l, pad_r)))
    sb = jnp.pad(sb, ((0, 0), (pad_l, pad_r)))
    return cos, sa, sb


def _softmax_step(q, k, v, state, bias=None):
    m, l, acc = state
    s = lax.dot_general(q, k, (((1,), (1,)), ((), ())), preferred_element_type=F32)
    if bias is not None:
        s = s + bias
    m_new = jnp.maximum(m, jnp.max(s, axis=-1, keepdims=True))
    alpha = jnp.exp(m - m_new)
    p = jnp.exp(s - m_new)
    l = alpha * l + jnp.sum(p, axis=-1, keepdims=True)
    acc = alpha * acc + jnp.dot(p.astype(BF16), v, preferred_element_type=F32)
    return m_new, l, acc


def _init_state(rows):
    return (jnp.full((rows, 1), -jnp.inf, F32), jnp.zeros((rows, 1), F32),
            jnp.zeros((rows, LANES), F32))


def _attend(q, kv_refs, k_lanes, tk):
    state = _init_state(q.shape[0])
    for k_ref, v_ref in kv_refs:
        n = k_ref.shape[0]
        step = min(tk, n)
        for c in range(n // step):
            k = k_ref[c * step:(c + 1) * step, k_lanes].astype(BF16)
            v = v_ref[c * step:(c + 1) * step, :].astype(BF16)
            state = _softmax_step(q, k, v, state)
    _, l, acc = state
    return acc / l


def _gqa_kernel(*refs, n_src, tk):
    q_ref, o_ref = refs[0], refs[-1]
    kv_refs = [(refs[1 + 2 * i], refs[2 + 2 * i]) for i in range(n_src)]
    tq = q_ref.shape[0]
    lo = lax.broadcasted_iota(jnp.int32, (tq, LANES), 1) < HEAD_DIM
    for kvh in range(2):
        pieces = []
        for j in (2 * kvh, 2 * kvh + 1):
            blk = q_ref[:, j * LANES:(j + 1) * LANES].astype(F32)
            rolled = pltpu.roll(blk, HEAD_DIM, 1)
            if kvh == 0:
                pieces += [jnp.where(lo, blk, 0.0), jnp.where(lo, rolled, 0.0)]
            else:
                pieces += [jnp.where(lo, 0.0, rolled), jnp.where(lo, 0.0, blk)]
        q = jnp.concatenate(pieces, axis=0).astype(BF16)
        o = _attend(q, kv_refs, slice(None), tk)
        for idx, j in enumerate((2 * kvh, 2 * kvh + 1)):
            o_a = o[(2 * idx) * tq:(2 * idx + 1) * tq]
            o_b = o[(2 * idx + 1) * tq:(2 * idx + 2) * tq]
            if kvh == 0:
                blk_out = jnp.where(lo, o_a, pltpu.roll(o_b, HEAD_DIM, 1))
            else:
                blk_out = jnp.where(lo, pltpu.roll(o_a, HEAD_DIM, 1), o_b)
            o_ref[:, j * LANES:(j + 1) * LANES] = blk_out.astype(o_ref.dtype)


def _gqa_attention(q, kv, batch, tq, tk=512):
    t, w = q.shape
    s = t // batch
    tq = min(tq, s)
    nq = s // tq
    in_specs = [pl.BlockSpec((tq, w), lambda b, i: (b * nq + i, 0))]
    args = [q]
    for k, v in kv:
        n = k.shape[1]
        spec = pl.BlockSpec((None, n, LANES), lambda b, i: (b, 0, 0))
        in_specs += [spec, spec]
        args += [k, v]
    return pl.pallas_call(
        functools.partial(_gqa_kernel, n_src=len(kv), tk=tk),
        grid=(batch, nq),
        in_specs=in_specs,
        out_specs=pl.BlockSpec((tq, w), lambda b, i: (b * nq + i, 0)),
        out_shape=jax.ShapeDtypeStruct((t, w), BF16),
        compiler_params=_params(("arbitrary", "arbitrary")),
        name="gqa_attention",
    )(*args)


def _pair_kernel(*refs, n_src, tk, dk):
    q_ref, o_ref = refs[0], refs[-1]
    kv_refs = [(refs[1 + 2 * i], refs[2 + 2 * i]) for i in range(n_src)]
    tq = q_ref.shape[0]
    lo = lax.broadcasted_iota(jnp.int32, (tq, LANES), 1) < HEAD_DIM
    outs = []
    for h in range(2):
        if dk == HEAD_DIM:
            blk = q_ref[...].astype(F32)
            q = (jnp.where(lo, blk, 0.0) if h == 0 else jnp.where(lo, 0.0, blk)).astype(BF16)
            k_lanes = slice(None)
        else:
            q = q_ref[:, h * LANES:(h + 1) * LANES].astype(BF16)
            k_lanes = slice(h * LANES, (h + 1) * LANES)
        outs.append(_attend(q, kv_refs, k_lanes, tk))
    o_ref[...] = jnp.where(lo, outs[0], outs[1]).astype(o_ref.dtype)


def _pair_attention(q, kv, batch, dk, tq, tk=512):
    t, w = q.shape
    qw = LANES if dk == HEAD_DIM else 2 * LANES
    pairs = w // qw
    s = t // batch
    tq = min(tq, s)
    nq = s // tq
    in_specs = [pl.BlockSpec((tq, qw), lambda b, p, i: (b * nq + i, p))]
    args = [q]
    for k, v in kv:
        n = k.shape[1]
        in_specs += [pl.BlockSpec((None, n, qw), lambda b, p, i: (b, 0, p)),
                     pl.BlockSpec((None, n, LANES), lambda b, p, i: (b, 0, p))]
        args += [k, v]
    return pl.pallas_call(
        functools.partial(_pair_kernel, n_src=len(kv), tk=tk, dk=dk),
        grid=(batch, pairs, nq),
        in_specs=in_specs,
        out_specs=pl.BlockSpec((tq, LANES), lambda b, p, i: (b * nq + i, p)),
        out_shape=jax.ShapeDtypeStruct((t, pairs * LANES), BF16),
        compiler_params=_params(("arbitrary", "arbitrary", "arbitrary")),
        name="pair_attention",
    )(*args)


NB_QROWS = 2
NB_KROWS = 10
NB_INVALID = 2 * NA_ROWS - 1


def _nb_kernel(q_ref, kc_ref, vc_ref, k_ref, v_ref, tab_ref, o_ref):
    blk = pl.program_id(2)
    rows = k_ref.shape[0] // GRID_W
    a = blk * NB_QROWS
    base = jnp.clip(a - NA_ROWS // 2, 0, rows - NB_KROWS)
    start = pl.multiple_of(base * GRID_W, GRID_W)
    kwin = k_ref[pl.ds(start, NB_KROWS * GRID_W), :].astype(BF16)
    vwin = v_ref[pl.ds(start, NB_KROWS * GRID_W), :].astype(BF16)
    kctx = kc_ref[...].astype(BF16)
    vctx = vc_ref[...].astype(BF16)
    tq = q_ref.shape[0]
    lo = lax.broadcasted_iota(jnp.int32, (tq, LANES), 1) < HEAD_DIM
    lo_tile = lax.broadcasted_iota(jnp.int32, (GRID_W, LANES), 1) < HEAD_DIM
    qblk = q_ref[...].astype(F32)
    outs = []
    for h in range(2):
        q = (jnp.where(lo, qblk, 0.0) if h == 0 else jnp.where(lo, 0.0, qblk)).astype(BF16)
        row_tiles = []
        for i in range(NB_QROWS):
            qr = a + i
            r0 = jnp.clip(qr - NA_ROWS // 2, 0, rows - NA_ROWS)
            tiles = []
            for jp in range(NB_KROWS // 2):
                halves = []
                for j in (2 * jp, 2 * jp + 1):
                    kr = base + j
                    valid = jnp.logical_and(kr >= r0, kr < r0 + NA_ROWS)
                    idx = jnp.where(valid, kr - qr + NA_ROWS - 1, NB_INVALID)
                    halves.append(tab_ref[h, idx])
                tiles.append(jnp.where(lo_tile, halves[0], halves[1]))
            row_tiles.append(jnp.concatenate(tiles, axis=1))
        bias = jnp.concatenate(row_tiles, axis=0)
        state = _init_state(tq)
        state = _softmax_step(q, kctx, vctx, state)
        state = _softmax_step(q, kwin, vwin, state, bias=bias)
        _, l, acc = state
        outs.append(acc / l)
    o_ref[...] = jnp.where(lo, outs[0], outs[1]).astype(o_ref.dtype)


def _nb_attention(q, k, v, kctx, vctx, table, batch):
    t, w = q.shape
    s = t // batch
    pairs = w // LANES
    tq = NB_QROWS * GRID_W
    nq = s // tq
    l = kctx.shape[1]
    return pl.pallas_call(
        _nb_kernel,
        grid=(batch, pairs, nq),
        in_specs=[
            pl.BlockSpec((tq, LANES), lambda b, p, i: (b * nq + i, p)),
            pl.BlockSpec((None, l, LANES), lambda b, p, i: (b, 0, p)),
            pl.BlockSpec((None, l, LANES), lambda b, p, i: (b, 0, p)),
            pl.BlockSpec((None, s, LANES), lambda b, p, i: (b, 0, p)),
            pl.BlockSpec((None, s, LANES), lambda b, p, i: (b, 0, p)),
            pl.BlockSpec((2, NB_INVALID + 1, GRID_W, LANES), lambda b, p, i: (p, 0, 0, 0)),
        ],
        out_specs=pl.BlockSpec((tq, LANES), lambda b, p, i: (b * nq + i, p)),
        out_shape=jax.ShapeDtypeStruct((t, w), BF16),
        compiler_params=_params(("arbitrary", "arbitrary", "arbitrary")),
        name="nb_attention",
    )(q, kctx, vctx, k, v, table)


def _nb_bias_table(rpb):
    h = rpb.shape[0]
    qc = jnp.arange(GRID_W)[:, None]
    kc = jnp.arange(GRID_W)[None, :]
    c0 = jnp.clip(qc - NA_COLS // 2, 0, GRID_W - NA_COLS)
    inwin = jnp.logical_and(kc >= c0, kc < c0 + NA_COLS)
    dc = jnp.clip(kc - qc + NA_COLS - 1, 0, 2 * NA_COLS - 2)
    t = jnp.where(inwin[None, None], rpb[:, :, dc].astype(F32), NEG)
    t = jnp.concatenate([t, jnp.full((h, 1, GRID_W, GRID_W), NEG, F32)], axis=1)
    return jnp.concatenate([t, t], axis=-1)


PAD = 8


def _pool_kernel(u_ref, map_ref, sc_ref, o_ref, pad_ref):
    s = u_ref.shape[0]
    g = pl.program_id(1)
    u = u_ref[...]
    pad_ref[0:PAD, :] = jnp.zeros((PAD, LANES), F32)
    pad_ref[PAD + s:PAD + s + PAD, :] = jnp.zeros((PAD, LANES), F32)
    pad_ref[PAD:PAD + s, :] = u
    t = lax.broadcasted_iota(jnp.int32, (s, LANES), 0)
    for gi, win in enumerate(POOL_WINDOWS):
        @pl.when(g == gi)
        def _():
            acc = jnp.zeros((s, LANES), F32)
            for d in range(-(win // 2), win - win // 2):
                acc = acc + pad_ref[PAD + d:PAD + d + s, :]
            cnt = jnp.minimum(t + (win - win // 2), s) - jnp.maximum(t - win // 2, 0)
            diff = acc / cnt.astype(F32) - u
            y = jnp.dot(diff.astype(BF16), map_ref[...].astype(BF16), preferred_element_type=F32)
            o_ref[...] = (y * sc_ref[...]).astype(o_ref.dtype)


def _pool(u, b_map, b_scale, batch):
    t, w = u.shape
    s = t // batch
    groups = w // LANES
    return pl.pallas_call(
        _pool_kernel,
        grid=(batch, groups),
        in_specs=[
            pl.BlockSpec((s, LANES), lambda b, g: (b, g)),
            pl.BlockSpec((None, LANES, LANES), lambda b, g: (g, 0, 0)),
            pl.BlockSpec((1, LANES), lambda b, g: (0, g)),
        ],
        out_specs=pl.BlockSpec((s, LANES), lambda b, g: (b, g)),
        out_shape=jax.ShapeDtypeStruct((t, w), BF16),
        scratch_shapes=[pltpu.VMEM((s + 2 * PAD, LANES), F32)],
        compiler_params=_params(("arbitrary", "arbitrary")),
        name="pool",
    )(u, b_map, b_scale.reshape(1, w))


def _out_kernel(x_ref, a_ref, ga_ref, b_ref, gb_ref, w_ref, g_ref, gate_ref, o_ref):
    half = a_ref.shape[1]
    ta = (a_ref[...].astype(F32) * _silu(ga_ref[...].astype(F32))).astype(BF16)
    tb = (b_ref[...].astype(F32) * _silu(gb_ref[...].astype(F32))).astype(BF16)
    y = (jnp.dot(ta, w_ref[0:half, :], preferred_element_type=F32)
         + jnp.dot(tb, w_ref[half:2 * half, :], preferred_element_type=F32))
    ms = jnp.mean(y * y, axis=-1, keepdims=True)
    o_ref[...] = x_ref[...] + gate_ref[...] * (y * lax.rsqrt(ms + EPS) * g_ref[...])


def _out_proj(x, a, ga, b, gb, w, g_post, gate, rows_per_mod, tm=256):
    t, d = x.shape
    half = a.shape[1]
    tm = min(tm, t)
    br = pl.BlockSpec((tm, half), lambda i: (i, 0))
    return pl.pallas_call(
        _out_kernel,
        grid=(t // tm,),
        in_specs=[
            pl.BlockSpec((tm, d), lambda i: (i, 0)),
            br, br, br, br,
            pl.BlockSpec((2 * half, d), lambda i: (0, 0)),
            pl.BlockSpec((1, d), lambda i: (0, 0)),
            pl.BlockSpec((None, 1, d), lambda i: ((i * tm) // rows_per_mod, 0, 0)),
        ],
        out_specs=pl.BlockSpec((tm, d), lambda i: (i, 0)),
        out_shape=jax.ShapeDtypeStruct((t, d), F32),
        compiler_params=_params(("arbitrary",)),
        name="out_proj",
    )(x, a, ga, b, gb, w, g_post, gate)


def _kvup_kernel(c_ref, g_ref, pe_ref, wk_ref, wv_ref, cn_ref, k_ref, v_ref, *, do_norm):
    c = c_ref[...]
    if do_norm:
        ms = jnp.mean(c * c, axis=-1, keepdims=True)
        c = c * lax.rsqrt(ms + EPS) * g_ref[...]
    cn_ref[...] = c
    cb = c.astype(BF16)
    heads = k_ref.shape[1] // LANES
    k = jnp.dot(cb, wk_ref[...], preferred_element_type=F32)
    k = k + jnp.concatenate([pe_ref[...]] * heads, axis=1)
    k_ref[...] = k.astype(k_ref.dtype)
    v_ref[...] = jnp.dot(cb, wv_ref[...], preferred_element_type=F32).astype(v_ref.dtype)


def _kvup(ckv, g, pe128, wk, wv, do_norm, tm=256):
    t, r = ckv.shape
    tm = min(tm, t)
    nk, nv = wk.shape[1], wv.shape[1]
    row = lambda width: pl.BlockSpec((tm, width), lambda i: (i, 0))
    full = lambda a: pl.BlockSpec(a.shape, lambda i: (0, 0))
    return pl.pallas_call(
        functools.partial(_kvup_kernel, do_norm=do_norm),
        grid=(t // tm,),
        in_specs=[row(r), full(g), row(LANES), full(wk), full(wv)],
        out_specs=[row(r), row(nk), row(nv)],
        out_shape=[jax.ShapeDtypeStruct((t, r), F32), jax.ShapeDtypeStruct((t, nk), BF16),
                   jax.ShapeDtypeStruct((t, nv), BF16)],
        compiler_params=_params(("arbitrary",)),
        name="kv_up",
    )(ckv, g, pe128, wk, wv)


def kernel(x_prompt, x_sample, cache_a_k, cache_a_v, cache_c_k, cache_c_v, cache_d_ckv, cache_d_kpe,
           c, c_ctx, w_mod, b_mod, g_pre, g_post, w_in_e, a_q_norm, a_k_norm, b_map, b_scale, w_out_e,
           w_in_o, c_rpb, d_q_norm, d_w_uq, d_kv_norm, d_w_ukv, w_out_o):
    bp, sp, d = x_prompt.shape
    bs, ss, _ = x_sample.shape
    past = cache_a_k.shape[2]
    depth = w_mod.shape[0]
    tp, ts = bp * sp, bs * ss
    branch = d // 2
    d_heads = 8
    d_nope, d_rope, d_v = 64, 32, 64
    q_lora, kv_lora = d_q_norm.shape[1], d_kv_norm.shape[1]
    kv_w = a_k_norm.shape[1] * 2

    xp = x_prompt.reshape(tp, d)
    xs = x_sample.reshape(ts, d)

    cond = jnp.concatenate([c_ctx[None, :], c, jnp.zeros((8 - 1 - bs, d), F32)], axis=0)
    mods = _mod_call(cond, w_mod, b_mod)

    def mod_rows(layer, part, lo_row, n_rows):
        return mods[layer, lo_row:lo_row + n_rows, part * d:(part + 1) * d].reshape(n_rows, 1, d)

    rope_a = _axial_tables(ss, HEAD_DIM, 0)
    rope_d = _axial_tables(ss, d_rope, d_nope)
    no_rope_p = tuple(jnp.zeros((sp, LANES), F32) for _ in range(3))
    zeros_mod = jnp.zeros((1, 1, q_lora), F32)

    outs = {name: [] for name in ("a_k", "a_v", "c_k", "c_v", "d_ckv", "d_kpe")}
    for layer in range(depth):
        j = layer // 2
        g_in = g_pre[layer].reshape(1, d)
        g_out = g_post[layer].reshape(1, d)
        paths = (("p", xp, bp, sp, 0, 1, tp), ("s", xs, bs, ss, 1, bs, ss))
        if layer % 2 == 0:
            w_in = w_in_e[j].astype(BF16)
            w_out = w_out_e[j].astype(BF16)
            qg = jnp.concatenate([a_q_norm[j]] * 2).reshape(1, LANES)
            kg = jnp.concatenate([a_k_norm[j]] * 2).reshape(1, LANES)
            o0 = branch
            segs = [(0, branch, F32, 1.0), (o0, kv_w, F32, 1.0), (o0 + kv_w, kv_w, F32, 1.0),
                    (o0 + 2 * kv_w, branch, BF16, 1.0), (o0 + 2 * kv_w + branch, branch, F32, 1.0),
                    (o0 + 2 * kv_w + 2 * branch, branch, BF16, 1.0)]
            new_x = []
            for name, x, nb, seq, row0, nrows, rpm in paths:
                shift, scale, gate = (mod_rows(layer, part, row0, nrows) for part in range(3))
                q, k, v, ga, ub, gb = _norm_mm(x, g_in, scale, shift, w_in, segs, rpm, name="in_proj_even")
                if name == "p":
                    qn = _head_prep(q, qg, no_rope_p, seq, BF16, True, 0, mul=HEAD_DIM ** -0.5)
                    kn = _head_prep(k, kg, no_rope_p, seq, F32, True, 0)
                    outs["a_k"].append(kn)
                    outs["a_v"].append(v)
                    kv = [(kn.reshape(nb, seq, kv_w), v.reshape(nb, seq, kv_w))]
                    o_a = _gqa_attention(qn, kv, nb, tq=256)
                else:
                    qn = _head_prep(q, qg, rope_a, seq, BF16, True, 16, mul=HEAD_DIM ** -0.5)
                    kn = _head_prep(k, kg, rope_a, seq, BF16, True, 16)
                    kv = [(cache_a_k[:, j].reshape(nb, past, kv_w), cache_a_v[:, j].reshape(nb, past, kv_w)),
                          (kn.reshape(nb, seq, kv_w), v.reshape(nb, seq, kv_w))]
                    o_a = _gqa_attention(qn, kv, nb, tq=128)
                o_b = _pool(ub, b_map[j], b_scale[j], nb)
                new_x.append(_out_proj(x, o_a, ga, o_b, gb, w_out, g_out, gate, rpm))
            xp, xs = new_x
        else:
            pad = lambda a, n: jnp.pad(a, ((0, 0), (0, n)))
            wi = w_in_o[j]
            off_kpe = 4 * branch + q_lora + kv_lora
            w_in = jnp.concatenate([wi[:, :off_kpe], jnp.zeros((d, d_nope), F32),
                                    wi[:, off_kpe:off_kpe + d_rope],
                                    jnp.zeros((d, LANES - d_nope - d_rope), F32),
                                    wi[:, off_kpe + d_rope:]], axis=1).astype(BF16)
            w_out = w_out_o[j].astype(BF16)
            dqk = d_nope + d_rope
            w_uq = jnp.pad(d_w_uq[j].reshape(q_lora, d_heads, dqk),
                           ((0, 0), (0, 0), (0, LANES - dqk))).reshape(q_lora, d_heads * LANES).astype(BF16)
            ukv = d_w_ukv[j].reshape(kv_lora, d_heads, d_nope + d_v)
            w_uk = jnp.pad(ukv[:, :, :d_nope], ((0, 0), (0, 0), (0, LANES - d_nope))
                           ).reshape(kv_lora, d_heads * LANES).astype(BF16)
            w_uv = ukv[:, :, d_nope:].reshape(kv_lora, d_heads * d_v).astype(BF16)
            qn_g = d_q_norm[j].reshape(1, q_lora)
            kvn_g = d_kv_norm[j].reshape(1, kv_lora)
            table = _nb_bias_table(c_rpb[j])
            b4 = 4 * branch
            new_x = []
            for name, x, nb, seq, row0, nrows, rpm in paths:
                shift, scale, gate = (mod_rows(layer, part, row0, nrows) for part in range(3))
                kv_dt = F32 if name == "p" else BF16
                segs = [(0, branch, BF16, HEAD_DIM ** -0.5), (branch, branch, kv_dt, 1.0),
                        (2 * branch, branch, kv_dt, 1.0), (3 * branch, branch, BF16, 1.0),
                        (b4, q_lora, F32, 1.0), (b4 + q_lora, kv_lora, F32, 1.0),
                        (b4 + q_lora + kv_lora, LANES, F32, 1.0),
                        (b4 + q_lora + kv_lora + LANES, branch, BF16, 1.0)]
                qc, kc, vc, gc, cq, ckv, kpe, gd = _norm_mm(x, g_in, scale, shift, w_in, segs, rpm,
                                                            name="in_proj_odd")
                (qd,) = _norm_mm(cq, qn_g, zeros_mod, zeros_mod, w_uq,
                                 [(0, d_heads * LANES, BF16 if name == "p" else F32, dqk ** -0.5)],
                                 cq.shape[0], name="q_up")
                if name == "p":
                    outs["c_k"].append(kc)
                    outs["c_v"].append(vc)
                    ckv_n, k_d, v_d = _kvup(ckv, kvn_g, kpe, w_uk, w_uv, True)
                    outs["d_ckv"].append(ckv_n)
                    outs["d_kpe"].append(kpe[:, d_nope:d_nope + d_rope])
                    o_c = _pair_attention(qc, [(kc.reshape(nb, seq, branch), vc.reshape(nb, seq, branch))],
                                          nb, HEAD_DIM, tq=256)
                    o_d = _pair_attention(qd, [(k_d.reshape(nb, seq, -1), v_d.reshape(nb, seq, -1))],
                                          nb, 2 * HEAD_DIM, tq=256)
                else:
                    o_c = _nb_attention(qc, kc.reshape(nb, seq, branch), vc.reshape(nb, seq, branch),
                                        cache_c_k[:, j].reshape(nb, past, branch),
                                        cache_c_v[:, j].reshape(nb, past, branch), table, nb)
                    ones = jnp.ones((1, LANES), F32)
                    qd = _head_prep(qd, ones, rope_d, seq, BF16, False, 8)
                    kpe_r = _head_prep(kpe, ones, rope_d, seq, F32, False, 8)
                    _, k_new, v_new = _kvup(ckv, kvn_g, kpe_r, w_uk, w_uv, True)
                    pe_ctx = jnp.pad(cache_d_kpe[:, j].reshape(nb * past, d_rope),
                                     ((0, 0), (d_nope, LANES - d_nope - d_rope)))
                    _, k_ctx, v_ctx = _kvup(cache_d_ckv[:, j].reshape(nb * past, kv_lora), kvn_g, pe_ctx,
                                            w_uk, w_uv, False)
                    kv = [(k_ctx.reshape(nb, past, -1), v_ctx.reshape(nb, past, -1)),
                          (k_new.reshape(nb, seq, -1), v_new.reshape(nb, seq, -1))]
                    o_d = _pair_attention(qd, kv, nb, 2 * HEAD_DIM, tq=256)
                new_x.append(_out_proj(x, o_c, gc, o_d, gd, w_out, g_out, gate, rpm))
            xp, xs = new_x

    kvh = kv_w // HEAD_DIM
    ch = branch // HEAD_DIM

    def stacked(name, *tail):
        return jnp.stack([a.reshape(bp, sp, *tail) for a in outs[name]], axis=1)

    return (xp.reshape(bp, sp, d), xs.reshape(bs, ss, d),
            stacked("a_k", kvh, HEAD_DIM), stacked("a_v", kvh, HEAD_DIM),
            stacked("c_k", ch, HEAD_DIM), stacked("c_v", ch, HEAD_DIM),
            stacked("d_ckv", kv_lora), stacked("d_kpe", d_rope))
```

```python
import functools

import jax
import jax.numpy as jnp
from jax import lax
from jax.experimental import pallas as pl
from jax.experimental.pallas import tpu as pltpu

F32 = jnp.float32
BF16 = jnp.bfloat16

LANES = 128
GRID_W = 64
HEAD_DIM = 64
NA_ROWS = 8
NA_COLS = 16
POOL_WINDOWS = (2, 4, 8, 16)
ROPE_THETA = 10000.0
EPS = 1e-6
NEG = -1e30
LOG2E = 1.4426950408889634
VMEM_LIMIT = 56 * 1024 * 1024


def _params(sem):
    return pltpu.CompilerParams(dimension_semantics=sem, vmem_limit_bytes=VMEM_LIMIT)


def _silu(x):
    return x / (1.0 + jnp.exp(-x))


def _mod_kernel(c_ref, w_ref, b_ref, o_ref):
    s = _silu(c_ref[...]).astype(BF16)
    o_ref[...] = jnp.dot(s, w_ref[...].astype(BF16), preferred_element_type=F32) + b_ref[...]


def _mod_call(cond, w_mod, b_mod):
    depth, d, n = w_mod.shape
    rows = cond.shape[0]
    tn = 1024
    return pl.pallas_call(
        _mod_kernel,
        grid=(depth, n // tn),
        in_specs=[
            pl.BlockSpec((rows, d), lambda l, j: (0, 0)),
            pl.BlockSpec((None, d, tn), lambda l, j: (l, 0, j)),
            pl.BlockSpec((None, 1, tn), lambda l, j: (l, 0, j)),
        ],
        out_specs=pl.BlockSpec((None, rows, tn), lambda l, j: (l, 0, j)),
        out_shape=jax.ShapeDtypeStruct((depth, rows, n), F32),
        compiler_params=_params(("arbitrary", "arbitrary")),
        name="adaln_mod",
    )(cond, w_mod, b_mod.reshape(depth, 1, n))


def _norm_mm_kernel(x_ref, g_ref, sc_ref, sh_ref, w_ref, *o_refs, segs, do_norm):
    x = x_ref[...].astype(F32)
    if do_norm:
        ms = jnp.mean(x * x, axis=-1, keepdims=True)
        x = x * lax.rsqrt(ms + EPS) * g_ref[...]
        x = x * (1.0 + sc_ref[...]) + sh_ref[...]
    z = jnp.dot(x.astype(BF16), w_ref[...], preferred_element_type=F32)
    for o_ref, (start, width, mul) in zip(o_refs, segs):
        piece = z[:, start:start + width]
        if mul != 1.0:
            piece = piece * mul
        o_ref[...] = piece.astype(o_ref.dtype)


def _norm_mm(x, g, scale, shift, w, segs, rows_per_mod, do_norm=True, tm=256, name="norm_mm"):
    t, k = x.shape
    n = w.shape[1]
    tm = min(tm, t)
    kern = functools.partial(_norm_mm_kernel, segs=tuple((s, wd, m) for s, wd, _, m in segs),
                             do_norm=do_norm)
    mod_map = lambda i: ((i * tm) // rows_per_mod, 0, 0)
    return pl.pallas_call(
        kern,
        grid=(t // tm,),
        in_specs=[
            pl.BlockSpec((tm, k), lambda i: (i, 0)),
            pl.BlockSpec((1, k), lambda i: (0, 0)),
            pl.BlockSpec((None, 1, k), mod_map),
            pl.BlockSpec((None, 1, k), mod_map),
            pl.BlockSpec((k, n), lambda i: (0, 0)),
        ],
        out_specs=[pl.BlockSpec((tm, wd), lambda i: (i, 0)) for _, wd, _, _ in segs],
        out_shape=[jax.ShapeDtypeStruct((t, wd), dt) for _, wd, dt, _ in segs],
        compiler_params=_params(("arbitrary",)),
        name=name,
    )(x, g, scale, shift, w)


def _head_prep_kernel(x_ref, g_ref, cos_ref, sa_ref, sb_ref, o_ref, *, do_norm, rope_shift, mul):
    tm, w = x_ref.shape
    lane = lax.broadcasted_iota(jnp.int32, (tm, LANES), 1)
    lo = lane < HEAD_DIM
    for c in range(w // LANES):
        x = x_ref[:, c * LANES:(c + 1) * LANES].astype(F32)
        if do_norm:
            x2 = x * x
            s_all = jnp.sum(x2, axis=-1, keepdims=True)
            s_lo = jnp.sum(jnp.where(lo, x2, 0.0), axis=-1, keepdims=True)
            ms = jnp.where(lo, s_lo, s_all - s_lo) * (1.0 / HEAD_DIM)
            x = x * lax.rsqrt(ms + EPS) * g_ref[...]
        if rope_shift:
            x = (x * cos_ref[...]
                 + pltpu.roll(x, LANES - rope_shift, 1) * sa_ref[...]
                 + pltpu.roll(x, rope_shift, 1) * sb_ref[...])
        if mul != 1.0:
            x = x * mul
        o_ref[:, c * LANES:(c + 1) * LANES] = x.astype(o_ref.dtype)


def _head_prep(x, gain128, tables, seq, out_dtype, do_norm, rope_shift, mul=1.0, tm=256):
    t, w = x.shape
    tm = min(tm, seq)
    nseq = seq // tm
    cos, sa, sb = tables
    kern = functools.partial(_head_prep_kernel, do_norm=do_norm, rope_shift=rope_shift, mul=mul)
    tab_spec = pl.BlockSpec((tm, LANES), lambda i: (i % nseq, 0))
    return pl.pallas_call(
        kern,
        grid=(t // tm,),
        in_specs=[
            pl.BlockSpec((tm, w), lambda i: (i, 0)),
            pl.BlockSpec((1, LANES), lambda i: (0, 0)),
            tab_spec, tab_spec, tab_spec,
        ],
        out_specs=pl.BlockSpec((tm, w), lambda i: (i, 0)),
        out_shape=jax.ShapeDtypeStruct((t, w), out_dtype),
        compiler_params=_params(("arbitrary",)),
        name="head_prep",
    )(x, gain128, cos, sa, sb)


def _axial_tables(seq, dims, lane0):
    half = dims // 2
    quarter = half // 2
    inv = ROPE_THETA ** (-jnp.arange(0, half, 2, dtype=F32) / half)
    t = jnp.arange(seq)
    ang_r = (t // GRID_W).astype(F32)[:, None] * inv
    ang_c = (t % GRID_W).astype(F32)[:, None] * inv
    cos = jnp.concatenate([jnp.cos(ang_r)] * 2 + [jnp.cos(ang_c)] * 2, axis=-1)
    sin = jnp.concatenate([jnp.sin(ang_r)] * 2 + [jnp.sin(ang_c)] * 2, axis=-1)
    first = (jnp.arange(dims) % half) < quarter
    sa = jnp.where(first, -sin, 0.0)
    sb = jnp.where(first, 0.0, sin)
    if dims == HEAD_DIM:
        return tuple(jnp.concatenate([a, a], axis=-1) for a in (cos, sa, sb))
    pad_l, pad_r = lane0, LANES - lane0 - dims
    cos = jnp.pad(cos, ((0, 0), (pad_l, pad_r)), constant_values=1.0)
    sa = jnp.pad(sa, ((0, 0), (pad_l, pad_r)))
    sb = jnp.pad(sb, ((0, 0), (pad_l, pad_r)))
    return cos, sa, sb


def _softmax_step(q, k, v, state, bias=None):
    m, l, acc = state
    s = lax.dot_general(q, k, (((1,), (1,)), ((), ())), preferred_element_type=F32)
    if bias is not None:
        s = s + bias
    m_new = jnp.maximum(m, jnp.max(s, axis=-1, keepdims=True))
    alpha = jnp.exp2(m - m_new)
    p = jnp.exp2(s - m_new)
    l = alpha * l + jnp.sum(p, axis=-1, keepdims=True)
    acc = alpha * acc + jnp.dot(p.astype(BF16), v, preferred_element_type=F32)
    return m_new, l, acc


def _init_state(rows):
    return (jnp.full((rows, 1), -jnp.inf, F32), jnp.zeros((rows, 1), F32),
            jnp.zeros((rows, LANES), F32))


def _attend_t(streams, k_refs, vt_ref, s_ref, p_ref, tk):
    chunks = []
    off = 0
    for k_ref in k_refs:
        n = k_ref.shape[0]
        assert n % tk == 0
        chunks += [(k_ref, c * tk, off + c * tk) for c in range(n // tk)]
        off += n
    nc = len(chunks)
    ns = len(streams)
    mq = streams[0][0].shape[1]
    m = [jnp.full((1, mq), -jnp.inf, F32) for _ in range(ns)]
    l = [jnp.zeros((1, mq), F32) for _ in range(ns)]
    acc = [jnp.zeros((LANES, mq), F32) for _ in range(ns)]
    m_chunk = [None] * ns
    alpha = [[None] * ns for _ in range(2)]
    for t in range(nc + 2):
        if t < nc:
            k_ref, row, _ = chunks[t]
            for i, (q_t, k_lanes) in enumerate(streams):
                s = jnp.dot(k_ref[row:row + tk, k_lanes].astype(BF16), q_t, preferred_element_type=F32)
                m_chunk[i] = jnp.max(s, axis=0, keepdims=True)
                s_ref[t % 2, i] = s
        if 0 <= t - 2 < nc:
            col = chunks[t - 2][2]
            for i in range(ns):
                pv = jnp.dot(vt_ref[:, col:col + tk], p_ref[t % 2, i], preferred_element_type=F32)
                acc[i] = alpha[t % 2][i] * acc[i] + pv
        if 0 <= t - 1 < nc:
            for i in range(ns):
                m_new = jnp.maximum(m[i], m_prev_chunk[i])
                alpha[(t - 1) % 2][i] = jnp.exp2(m[i] - m_new)
                p = jnp.exp2(s_ref[(t - 1) % 2, i] - m_new)
                l[i] = alpha[(t - 1) % 2][i] * l[i] + jnp.sum(p, axis=0, keepdims=True)
                p_ref[(t - 1) % 2, i] = p.astype(BF16)
                m[i] = m_new
        m_prev_chunk = list(m_chunk)
    return [acc[i] / l[i] for i in range(ns)]


def _fill_vt(vt_ref, v_refs, chunk=512):
    off = 0
    for v_ref in v_refs:
        n = v_ref.shape[0]
        step = min(chunk, n)
        for c in range(n // step):
            blk = v_ref[c * step:(c + 1) * step, :].astype(F32)
            vt_ref[:, off + c * step:off + (c + 1) * step] = blk.T.astype(vt_ref.dtype)
        off += n


def _attn_scratch(n_total, n_streams, tk, mq):
    return [pltpu.VMEM((LANES, n_total), BF16),
            pltpu.VMEM((2, n_streams, tk, mq), F32),
            pltpu.VMEM((2, n_streams, tk, mq), BF16)]


def _gqa_kernel(*refs, n_src, tk):
    q_ref, o_ref, vt_ref, s_ref, p_ref = refs[0], refs[-4], refs[-3], refs[-2], refs[-1]
    k_refs = [refs[1 + 2 * i] for i in range(n_src)]
    v_refs = [refs[2 + 2 * i] for i in range(n_src)]
    tq = q_ref.shape[0]

    @pl.when(pl.program_id(1) == 0)
    def _():
        _fill_vt(vt_ref, v_refs)

    q_blocks = [q_ref[:, j * LANES:(j + 1) * LANES].astype(F32).T for j in range(4)]
    zeros = jnp.zeros((HEAD_DIM, tq), F32)
    streams = []
    for g in range(2):
        cols = []
        for j in (2 * g, 2 * g + 1):
            for part in (q_blocks[j][:HEAD_DIM], q_blocks[j][HEAD_DIM:]):
                cols.append(jnp.concatenate([part, zeros] if g == 0 else [zeros, part], axis=0))
        streams.append((jnp.concatenate(cols, axis=1).astype(BF16), slice(None)))
    outs = _attend_t(streams, k_refs, vt_ref, s_ref, p_ref, tk)
    for g in range(2):
        o_t = outs[g][g * HEAD_DIM:(g + 1) * HEAD_DIM]
        for idx, j in enumerate((2 * g, 2 * g + 1)):
            blk_t = jnp.concatenate([o_t[:, (2 * idx) * tq:(2 * idx + 1) * tq],
                                     o_t[:, (2 * idx + 1) * tq:(2 * idx + 2) * tq]], axis=0)
            o_ref[:, j * LANES:(j + 1) * LANES] = blk_t.T.astype(o_ref.dtype)


def _gqa_attention(q, kv, batch, tq, tk=512):
    t, w = q.shape
    s = t // batch
    tq = min(tq, s)
    nq = s // tq
    tk = min([tk] + [k.shape[1] for k, _ in kv])
    in_specs = [pl.BlockSpec((tq, w), lambda b, i: (b * nq + i, 0))]
    args = [q]
    n_total = 0
    for k, v in kv:
        n = k.shape[1]
        n_total += n
        spec = pl.BlockSpec((None, n, LANES), lambda b, i: (b, 0, 0))
        in_specs += [spec, spec]
        args += [k, v]
    return pl.pallas_call(
        functools.partial(_gqa_kernel, n_src=len(kv), tk=tk),
        grid=(batch, nq),
        in_specs=in_specs,
        out_specs=pl.BlockSpec((tq, w), lambda b, i: (b * nq + i, 0)),
        out_shape=jax.ShapeDtypeStruct((t, w), BF16),
        scratch_shapes=_attn_scratch(n_total, 2, tk, 4 * tq),
        compiler_params=_params(("arbitrary", "arbitrary")),
        name="gqa_attention",
    )(*args)


def _pair_kernel(*refs, n_src, tk, dk):
    q_ref, o_ref, vt_ref, s_ref, p_ref = refs[0], refs[-4], refs[-3], refs[-2], refs[-1]
    k_refs = [refs[1 + 2 * i] for i in range(n_src)]
    v_refs = [refs[2 + 2 * i] for i in range(n_src)]
    tq = q_ref.shape[0]

    @pl.when(pl.program_id(2) == 0)
    def _():
        _fill_vt(vt_ref, v_refs)

    top = lax.broadcasted_iota(jnp.int32, (LANES, tq), 0) < HEAD_DIM
    streams = []
    for h in range(2):
        if dk == HEAD_DIM:
            both = q_ref[...].astype(F32).T
            q_t = (jnp.where(top, both, 0.0) if h == 0 else jnp.where(top, 0.0, both)).astype(BF16)
            streams.append((q_t, slice(None)))
        else:
            q_t = q_ref[:, h * LANES:(h + 1) * LANES].astype(F32).T.astype(BF16)
            streams.append((q_t, slice(h * LANES, (h + 1) * LANES)))
    outs = _attend_t(streams, k_refs, vt_ref, s_ref, p_ref, tk)
    o_ref[...] = jnp.where(top, outs[0], outs[1]).T.astype(o_ref.dtype)


def _pair_attention(q, kv, batch, dk, tq, tk=512):
    t, w = q.shape
    qw = LANES if dk == HEAD_DIM else 2 * LANES
    pairs = w // qw
    s = t // batch
    tq = min(tq, s)
    nq = s // tq
    tk = min([tk] + [k.shape[1] for k, _ in kv])
    in_specs = [pl.BlockSpec((tq, qw), lambda b, p, i: (b * nq + i, p))]
    args = [q]
    n_total = 0
    for k, v in kv:
        n = k.shape[1]
        n_total += n
        in_specs += [pl.BlockSpec((None, n, qw), lambda b, p, i: (b, 0, p)),
                     pl.BlockSpec((None, n, LANES), lambda b, p, i: (b, 0, p))]
        args += [k, v]
    return pl.pallas_call(
        functools.partial(_pair_kernel, n_src=len(kv), tk=tk, dk=dk),
        grid=(batch, pairs, nq),
        in_specs=in_specs,
        out_specs=pl.BlockSpec((tq, LANES), lambda b, p, i: (b * nq + i, p)),
        out_shape=jax.ShapeDtypeStruct((t, pairs * LANES), BF16),
        scratch_shapes=_attn_scratch(n_total, 2, tk, tq),
        compiler_params=_params(("arbitrary", "arbitrary", "arbitrary")),
        name="pair_attention",
    )(*args)


NB_QROWS = 2
NB_KROWS = 10
NB_INVALID = 2 * NA_ROWS - 1


def _nb_kernel(q_ref, kc_ref, vc_ref, k_ref, v_ref, tab_ref, o_ref):
    blk = pl.program_id(2)
    rows = k_ref.shape[0] // GRID_W
    a = blk * NB_QROWS
    base = jnp.clip(a - NA_ROWS // 2, 0, rows - NB_KROWS)
    start = pl.multiple_of(base * GRID_W, GRID_W)
    kwin = k_ref[pl.ds(start, NB_KROWS * GRID_W), :].astype(BF16)
    vwin = v_ref[pl.ds(start, NB_KROWS * GRID_W), :].astype(BF16)
    kctx = kc_ref[...].astype(BF16)
    vctx = vc_ref[...].astype(BF16)
    tq = q_ref.shape[0]
    lo = lax.broadcasted_iota(jnp.int32, (tq, LANES), 1) < HEAD_DIM
    lo_tile = lax.broadcasted_iota(jnp.int32, (GRID_W, LANES), 1) < HEAD_DIM
    qblk = q_ref[...].astype(F32)
    outs = []
    for h in range(2):
        q = (jnp.where(lo, qblk, 0.0) if h == 0 else jnp.where(lo, 0.0, qblk)).astype(BF16)
        row_tiles = []
        for i in range(NB_QROWS):
            qr = a + i
            r0 = jnp.clip(qr - NA_ROWS // 2, 0, rows - NA_ROWS)
            tiles = []
            for jp in range(NB_KROWS // 2):
                halves = []
                for j in (2 * jp, 2 * jp + 1):
                    kr = base + j
                    valid = jnp.logical_and(kr >= r0, kr < r0 + NA_ROWS)
                    idx = jnp.where(valid, kr - qr + NA_ROWS - 1, NB_INVALID)
                    halves.append(tab_ref[h, idx])
                tiles.append(jnp.where(lo_tile, halves[0], halves[1]))
            row_tiles.append(jnp.concatenate(tiles, axis=1))
        bias = jnp.concatenate(row_tiles, axis=0)
        state = _init_state(tq)
        state = _softmax_step(q, kctx, vctx, state)
        state = _softmax_step(q, kwin, vwin, state, bias=bias)
        _, l, acc = state
        outs.append(acc / l)
    o_ref[...] = jnp.where(lo, outs[0], outs[1]).astype(o_ref.dtype)


def _nb_attention(q, k, v, kctx, vctx, table, batch):
    t, w = q.shape
    s = t // batch
    pairs = w // LANES
    tq = NB_QROWS * GRID_W
    nq = s // tq
    l = kctx.shape[1]
    return pl.pallas_call(
        _nb_kernel,
        grid=(batch, pairs, nq),
        in_specs=[
            pl.BlockSpec((tq, LANES), lambda b, p, i: (b * nq + i, p)),
            pl.BlockSpec((None, l, LANES), lambda b, p, i: (b, 0, p)),
            pl.BlockSpec((None, l, LANES), lambda b, p, i: (b, 0, p)),
            pl.BlockSpec((None, s, LANES), lambda b, p, i: (b, 0, p)),
            pl.BlockSpec((None, s, LANES), lambda b, p, i: (b, 0, p)),
            pl.BlockSpec((2, NB_INVALID + 1, GRID_W, LANES), lambda b, p, i: (p, 0, 0, 0)),
        ],
        out_specs=pl.BlockSpec((tq, LANES), lambda b, p, i: (b * nq + i, p)),
        out_shape=jax.ShapeDtypeStruct((t, w), BF16),
        compiler_params=_params(("arbitrary", "arbitrary", "arbitrary")),
        name="nb_attention",
    )(q, kctx, vctx, k, v, table)


def _nb_bias_table(rpb):
    h = rpb.shape[0]
    qc = jnp.arange(GRID_W)[:, None]
    kc = jnp.arange(GRID_W)[None, :]
    c0 = jnp.clip(qc - NA_COLS // 2, 0, GRID_W - NA_COLS)
    inwin = jnp.logical_and(kc >= c0, kc < c0 + NA_COLS)
    dc = jnp.clip(kc - qc + NA_COLS - 1, 0, 2 * NA_COLS - 2)
    t = jnp.where(inwin[None, None], rpb[:, :, dc].astype(F32) * LOG2E, NEG)
    t = jnp.concatenate([t, jnp.full((h, 1, GRID_W, GRID_W), NEG, F32)], axis=1)
    return jnp.concatenate([t, t], axis=-1)


PAD = 8


def _pool_kernel(u_ref, map_ref, sc_ref, o_ref, pad_ref):
    s = u_ref.shape[0]
    g = pl.program_id(1)
    u = u_ref[...]
    pad_ref[0:PAD, :] = jnp.zeros((PAD, LANES), F32)
    pad_ref[PAD + s:PAD + s + PAD, :] = jnp.zeros((PAD, LANES), F32)
    pad_ref[PAD:PAD + s, :] = u
    t = lax.broadcasted_iota(jnp.int32, (s, LANES), 0)
    for gi, win in enumerate(POOL_WINDOWS):
        @pl.when(g == gi)
        def _():
            acc = jnp.zeros((s, LANES), F32)
            for d in range(-(win // 2), win - win // 2):
                acc = acc + pad_ref[PAD + d:PAD + d + s, :]
            cnt = jnp.minimum(t + (win - win // 2), s) - jnp.maximum(t - win // 2, 0)
            diff = acc / cnt.astype(F32) - u
            y = jnp.dot(diff.astype(BF16), map_ref[...].astype(BF16), preferred_element_type=F32)
            o_ref[...] = (y * sc_ref[...]).astype(o_ref.dtype)


def _pool(u, b_map, b_scale, batch):
    t, w = u.shape
    s = t // batch
    groups = w // LANES
    return pl.pallas_call(
        _pool_kernel,
        grid=(batch, groups),
        in_specs=[
            pl.BlockSpec((s, LANES), lambda b, g: (b, g)),
            pl.BlockSpec((None, LANES, LANES), lambda b, g: (g, 0, 0)),
            pl.BlockSpec((1, LANES), lambda b, g: (0, g)),
        ],
        out_specs=pl.BlockSpec((s, LANES), lambda b, g: (b, g)),
        out_shape=jax.ShapeDtypeStruct((t, w), BF16),
        scratch_shapes=[pltpu.VMEM((s + 2 * PAD, LANES), F32)],
        compiler_params=_params(("arbitrary", "arbitrary")),
        name="pool",
    )(u, b_map, b_scale.reshape(1, w))


def _out_kernel(x_ref, a_ref, ga_ref, b_ref, gb_ref, w_ref, g_ref, gate_ref, o_ref):
    half = a_ref.shape[1]
    ta = (a_ref[...].astype(F32) * _silu(ga_ref[...].astype(F32))).astype(BF16)
    tb = (b_ref[...].astype(F32) * _silu(gb_ref[...].astype(F32))).astype(BF16)
    y = (jnp.dot(ta, w_ref[0:half, :], preferred_element_type=F32)
         + jnp.dot(tb, w_ref[half:2 * half, :], preferred_element_type=F32))
    ms = jnp.mean(y * y, axis=-1, keepdims=True)
    o_ref[...] = x_ref[...] + gate_ref[...] * (y * lax.rsqrt(ms + EPS) * g_ref[...])


def _out_proj(x, a, ga, b, gb, w, g_post, gate, rows_per_mod, tm=256):
    t, d = x.shape
    half = a.shape[1]
    tm = min(tm, t)
    br = pl.BlockSpec((tm, half), lambda i: (i, 0))
    return pl.pallas_call(
        _out_kernel,
        grid=(t // tm,),
        in_specs=[
            pl.BlockSpec((tm, d), lambda i: (i, 0)),
            br, br, br, br,
            pl.BlockSpec((2 * half, d), lambda i: (0, 0)),
            pl.BlockSpec((1, d), lambda i: (0, 0)),
            pl.BlockSpec((None, 1, d), lambda i: ((i * tm) // rows_per_mod, 0, 0)),
        ],
        out_specs=pl.BlockSpec((tm, d), lambda i: (i, 0)),
        out_shape=jax.ShapeDtypeStruct((t, d), F32),
        compiler_params=_params(("arbitrary",)),
        name="out_proj",
    )(x, a, ga, b, gb, w, g_post, gate)


def _kvup_kernel(c_ref, g_ref, pe_ref, wk_ref, wv_ref, cn_ref, k_ref, v_ref, *, do_norm):
    c = c_ref[...]
    if do_norm:
        ms = jnp.mean(c * c, axis=-1, keepdims=True)
        c = c * lax.rsqrt(ms + EPS) * g_ref[...]
    cn_ref[...] = c
    cb = c.astype(BF16)
    heads = k_ref.shape[1] // LANES
    k = jnp.dot(cb, wk_ref[...], preferred_element_type=F32)
    k = k + jnp.concatenate([pe_ref[...]] * heads, axis=1)
    k_ref[...] = k.astype(k_ref.dtype)
    v_ref[...] = jnp.dot(cb, wv_ref[...], preferred_element_type=F32).astype(v_ref.dtype)


def _kvup(ckv, g, pe128, wk, wv, do_norm, tm=256):
    t, r = ckv.shape
    tm = min(tm, t)
    nk, nv = wk.shape[1], wv.shape[1]
    row = lambda width: pl.BlockSpec((tm, width), lambda i: (i, 0))
    full = lambda a: pl.BlockSpec(a.shape, lambda i: (0, 0))
    return pl.pallas_call(
        functools.partial(_kvup_kernel, do_norm=do_norm),
        grid=(t // tm,),
        in_specs=[row(r), full(g), row(LANES), full(wk), full(wv)],
        out_specs=[row(r), row(nk), row(nv)],
        out_shape=[jax.ShapeDtypeStruct((t, r), F32), jax.ShapeDtypeStruct((t, nk), BF16),
                   jax.ShapeDtypeStruct((t, nv), BF16)],
        compiler_params=_params(("arbitrary",)),
        name="kv_up",
    )(ckv, g, pe128, wk, wv)


def kernel(x_prompt, x_sample, cache_a_k, cache_a_v, cache_c_k, cache_c_v, cache_d_ckv, cache_d_kpe,
           c, c_ctx, w_mod, b_mod, g_pre, g_post, w_in_e, a_q_norm, a_k_norm, b_map, b_scale, w_out_e,
           w_in_o, c_rpb, d_q_norm, d_w_uq, d_kv_norm, d_w_ukv, w_out_o):
    bp, sp, d = x_prompt.shape
    bs, ss, _ = x_sample.shape
    past = cache_a_k.shape[2]
    depth = w_mod.shape[0]
    tp, ts = bp * sp, bs * ss
    branch = d // 2
    d_heads = 8
    d_nope, d_rope, d_v = 64, 32, 64
    q_lora, kv_lora = d_q_norm.shape[1], d_kv_norm.shape[1]
    kv_w = a_k_norm.shape[1] * 2

    xp = x_prompt.reshape(tp, d)
    xs = x_sample.reshape(ts, d)

    cond = jnp.concatenate([c_ctx[None, :], c, jnp.zeros((8 - 1 - bs, d), F32)], axis=0)
    mods = _mod_call(cond, w_mod, b_mod)

    def mod_rows(layer, part, lo_row, n_rows):
        return mods[layer, lo_row:lo_row + n_rows, part * d:(part + 1) * d].reshape(n_rows, 1, d)

    rope_a = _axial_tables(ss, HEAD_DIM, 0)
    rope_d = _axial_tables(ss, d_rope, d_nope)
    no_rope_p = tuple(jnp.zeros((sp, LANES), F32) for _ in range(3))
    zeros_mod = jnp.zeros((1, 1, q_lora), F32)

    outs = {name: [] for name in ("a_k", "a_v", "c_k", "c_v", "d_ckv", "d_kpe")}
    for layer in range(depth):
        j = layer // 2
        g_in = g_pre[layer].reshape(1, d)
        g_out = g_post[layer].reshape(1, d)
        paths = (("p", xp, bp, sp, 0, 1, tp), ("s", xs, bs, ss, 1, bs, ss))
        if layer % 2 == 0:
            w_in = w_in_e[j].astype(BF16)
            w_out = w_out_e[j].astype(BF16)
            qg = jnp.concatenate([a_q_norm[j]] * 2).reshape(1, LANES)
            kg = jnp.concatenate([a_k_norm[j]] * 2).reshape(1, LANES)
            o0 = branch
            segs = [(0, branch, F32, 1.0), (o0, kv_w, F32, 1.0), (o0 + kv_w, kv_w, F32, 1.0),
                    (o0 + 2 * kv_w, branch, BF16, 1.0), (o0 + 2 * kv_w + branch, branch, F32, 1.0),
                    (o0 + 2 * kv_w + 2 * branch, branch, BF16, 1.0)]
            new_x = []
            for name, x, nb, seq, row0, nrows, rpm in paths:
                shift, scale, gate = (mod_rows(layer, part, row0, nrows) for part in range(3))
                q, k, v, ga, ub, gb = _norm_mm(x, g_in, scale, shift, w_in, segs, rpm, name="in_proj_even")
                if name == "p":
                    qn = _head_prep(q, qg, no_rope_p, seq, BF16, True, 0, mul=LOG2E * HEAD_DIM ** -0.5)
                    kn = _head_prep(k, kg, no_rope_p, seq, F32, True, 0)
                    outs["a_k"].append(kn)
                    outs["a_v"].append(v)
                    kv = [(kn.reshape(nb, seq, kv_w), v.reshape(nb, seq, kv_w))]
                    o_a = _gqa_attention(qn, kv, nb, tq=256)
                else:
                    qn = _head_prep(q, qg, rope_a, seq, BF16, True, 16, mul=LOG2E * HEAD_DIM ** -0.5)
                    kn = _head_prep(k, kg, rope_a, seq, BF16, True, 16)
                    kv = [(cache_a_k[:, j].reshape(nb, past, kv_w), cache_a_v[:, j].reshape(nb, past, kv_w)),
                          (kn.reshape(nb, seq, kv_w), v.reshape(nb, seq, kv_w))]
                    o_a = _gqa_attention(qn, kv, nb, tq=128)
                o_b = _pool(ub, b_map[j], b_scale[j], nb)
                new_x.append(_out_proj(x, o_a, ga, o_b, gb, w_out, g_out, gate, rpm))
            xp, xs = new_x
        else:
            pad = lambda a, n: jnp.pad(a, ((0, 0), (0, n)))
            wi = w_in_o[j]
            off_kpe = 4 * branch + q_lora + kv_lora
            w_in = jnp.concatenate([wi[:, :off_kpe], jnp.zeros((d, d_nope), F32),
                                    wi[:, off_kpe:off_kpe + d_rope],
                                    jnp.zeros((d, LANES - d_nope - d_rope), F32),
                                    wi[:, off_kpe + d_rope:]], axis=1).astype(BF16)
            w_out = w_out_o[j].astype(BF16)
            dqk = d_nope + d_rope
            w_uq = jnp.pad(d_w_uq[j].reshape(q_lora, d_heads, dqk),
                           ((0, 0), (0, 0), (0, LANES - dqk))).reshape(q_lora, d_heads * LANES).astype(BF16)
            ukv = d_w_ukv[j].reshape(kv_lora, d_heads, d_nope + d_v)
            w_uk = jnp.pad(ukv[:, :, :d_nope], ((0, 0), (0, 0), (0, LANES - d_nope))
                           ).reshape(kv_lora, d_heads * LANES).astype(BF16)
            w_uv = ukv[:, :, d_nope:].reshape(kv_lora, d_heads * d_v).astype(BF16)
            qn_g = d_q_norm[j].reshape(1, q_lora)
            kvn_g = d_kv_norm[j].reshape(1, kv_lora)
            table = _nb_bias_table(c_rpb[j])
            b4 = 4 * branch
            new_x = []
            for name, x, nb, seq, row0, nrows, rpm in paths:
                shift, scale, gate = (mod_rows(layer, part, row0, nrows) for part in range(3))
                kv_dt = F32 if name == "p" else BF16
                segs = [(0, branch, BF16, LOG2E * HEAD_DIM ** -0.5), (branch, branch, kv_dt, 1.0),
                        (2 * branch, branch, kv_dt, 1.0), (3 * branch, branch, BF16, 1.0),
                        (b4, q_lora, F32, 1.0), (b4 + q_lora, kv_lora, F32, 1.0),
                        (b4 + q_lora + kv_lora, LANES, F32, 1.0),
                        (b4 + q_lora + kv_lora + LANES, branch, BF16, 1.0)]
                qc, kc, vc, gc, cq, ckv, kpe, gd = _norm_mm(x, g_in, scale, shift, w_in, segs, rpm,
                                                            name="in_proj_odd")
                (qd,) = _norm_mm(cq, qn_g, zeros_mod, zeros_mod, w_uq,
                                 [(0, d_heads * LANES, BF16 if name == "p" else F32, LOG2E * dqk ** -0.5)],
                                 cq.shape[0], name="q_up")
                if name == "p":
                    outs["c_k"].append(kc)
                    outs["c_v"].append(vc)
                    ckv_n, k_d, v_d = _kvup(ckv, kvn_g, kpe, w_uk, w_uv, True)
                    outs["d_ckv"].append(ckv_n)
                    outs["d_kpe"].append(kpe[:, d_nope:d_nope + d_rope])
                    o_c = _pair_attention(qc, [(kc.reshape(nb, seq, branch), vc.reshape(nb, seq, branch))],
                                          nb, HEAD_DIM, tq=256)
                    o_d = _pair_attention(qd, [(k_d.reshape(nb, seq, -1), v_d.reshape(nb, seq, -1))],
                                          nb, 2 * HEAD_DIM, tq=256)
                else:
                    o_c = _nb_attention(qc, kc.reshape(nb, seq, branch), vc.reshape(nb, seq, branch),
                                        cache_c_k[:, j].reshape(nb, past, branch),
                                        cache_c_v[:, j].reshape(nb, past, branch), table, nb)
                    ones = jnp.ones((1, LANES), F32)
                    qd = _head_prep(qd, ones, rope_d, seq, BF16, False, 8)
                    kpe_r = _head_prep(kpe, ones, rope_d, seq, F32, False, 8)
                    _, k_new, v_new = _kvup(ckv, kvn_g, kpe_r, w_uk, w_uv, True)
                    pe_ctx = jnp.pad(cache_d_kpe[:, j].reshape(nb * past, d_rope),
                                     ((0, 0), (d_nope, LANES - d_nope - d_rope)))
                    _, k_ctx, v_ctx = _kvup(cache_d_ckv[:, j].reshape(nb * past, kv_lora), kvn_g, pe_ctx,
                                            w_uk, w_uv, False)
                    kv = [(k_ctx.reshape(nb, past, -1), v_ctx.reshape(nb, past, -1)),
                          (k_new.reshape(nb, seq, -1), v_new.reshape(nb, seq, -1))]
                    o_d = _pair_attention(qd, kv, nb, 2 * HEAD_DIM, tq=256)
                new_x.append(_out_proj(x, o_c, gc, o_d, gd, w_out, g_out, gate, rpm))
            xp, xs = new_x

    kvh = kv_w // HEAD_DIM
    ch = branch // HEAD_DIM

    def stacked(name, *tail):
        return jnp.stack([a.reshape(bp, sp, *tail) for a in outs[name]], axis=1)

    return (xp.reshape(bp, sp, d), xs.reshape(bs, ss, d),
            stacked("a_k", kvh, HEAD_DIM), stacked("a_v", kvh, HEAD_DIM),
            stacked("c_k", ch, HEAD_DIM), stacked("c_v", ch, HEAD_DIM),
            stacked("d_ckv", kv_lora), stacked("d_kpe", d_rope))
```

```python
import functools

import jax
import jax.numpy as jnp
from jax import lax
from jax.experimental import pallas as pl
from jax.experimental.pallas import tpu as pltpu

F32 = jnp.float32
BF16 = jnp.bfloat16

LANES = 128
GRID_W = 64
HEAD_DIM = 64
NA_ROWS = 8
NA_COLS = 16
POOL_WINDOWS = (2, 4, 8, 16)
ROPE_THETA = 10000.0
EPS = 1e-6
NEG = -1e30
LOG2E = 1.4426950408889634
VMEM_LIMIT = 56 * 1024 * 1024


def _params(sem):
    return pltpu.CompilerParams(dimension_semantics=sem, vmem_limit_bytes=VMEM_LIMIT)


def _silu(x):
    return x / (1.0 + jnp.exp(-x))


def _mod_kernel(c_ref, w_ref, b_ref, o_ref):
    s = _silu(c_ref[...]).astype(BF16)
    o_ref[...] = jnp.dot(s, w_ref[...].astype(BF16), preferred_element_type=F32) + b_ref[...]


def _mod_call(cond, w_mod, b_mod):
    depth, d, n = w_mod.shape
    rows = cond.shape[0]
    tn = 1024
    return pl.pallas_call(
        _mod_kernel,
        grid=(depth, n // tn),
        in_specs=[
            pl.BlockSpec((rows, d), lambda l, j: (0, 0)),
            pl.BlockSpec((None, d, tn), lambda l, j: (l, 0, j)),
            pl.BlockSpec((None, 1, tn), lambda l, j: (l, 0, j)),
        ],
        out_specs=pl.BlockSpec((None, rows, tn), lambda l, j: (l, 0, j)),
        out_shape=jax.ShapeDtypeStruct((depth, rows, n), F32),
        compiler_params=_params(("arbitrary", "arbitrary")),
        name="adaln_mod",
    )(cond, w_mod, b_mod.reshape(depth, 1, n))


def _norm_mm_kernel(x_ref, g_ref, sc_ref, sh_ref, w_ref, *o_refs, segs, do_norm):
    x = x_ref[...].astype(F32)
    if do_norm:
        ms = jnp.mean(x * x, axis=-1, keepdims=True)
        x = x * lax.rsqrt(ms + EPS) * g_ref[...]
        x = x * (1.0 + sc_ref[...]) + sh_ref[...]
    z = jnp.dot(x.astype(BF16), w_ref[...], preferred_element_type=F32)
    for o_ref, (start, width, mul) in zip(o_refs, segs):
        piece = z[:, start:start + width]
        if mul != 1.0:
            piece = piece * mul
        o_ref[...] = piece.astype(o_ref.dtype)


def _norm_mm(x, g, scale, shift, w, segs, rows_per_mod, do_norm=True, tm=512, name="norm_mm"):
    t, k = x.shape
    n = w.shape[1]
    tm = min(tm, t)
    kern = functools.partial(_norm_mm_kernel, segs=tuple((s, wd, m) for s, wd, _, m in segs),
                             do_norm=do_norm)
    mod_map = lambda i: ((i * tm) // rows_per_mod, 0, 0)
    return pl.pallas_call(
        kern,
        grid=(t // tm,),
        in_specs=[
            pl.BlockSpec((tm, k), lambda i: (i, 0)),
            pl.BlockSpec((1, k), lambda i: (0, 0)),
            pl.BlockSpec((None, 1, k), mod_map),
            pl.BlockSpec((None, 1, k), mod_map),
            pl.BlockSpec((k, n), lambda i: (0, 0)),
        ],
        out_specs=[pl.BlockSpec((tm, wd), lambda i: (i, 0)) for _, wd, _, _ in segs],
        out_shape=[jax.ShapeDtypeStruct((t, wd), dt) for _, wd, dt, _ in segs],
        compiler_params=_params(("arbitrary",)),
        name=name,
    )(x, g, scale, shift, w)


def _head_prep_kernel(x_ref, g_ref, cos_ref, sa_ref, sb_ref, o_ref, *, do_norm, rope_shift, mul):
    tm, w = x_ref.shape
    lane = lax.broadcasted_iota(jnp.int32, (tm, LANES), 1)
    lo = lane < HEAD_DIM
    for c in range(w // LANES):
        x = x_ref[:, c * LANES:(c + 1) * LANES].astype(F32)
        if do_norm:
            x2 = x * x
            s_all = jnp.sum(x2, axis=-1, keepdims=True)
            s_lo = jnp.sum(jnp.where(lo, x2, 0.0), axis=-1, keepdims=True)
            ms = jnp.where(lo, s_lo, s_all - s_lo) * (1.0 / HEAD_DIM)
            x = x * lax.rsqrt(ms + EPS) * g_ref[...]
        if rope_shift:
            x = (x * cos_ref[...]
                 + pltpu.roll(x, LANES - rope_shift, 1) * sa_ref[...]
                 + pltpu.roll(x, rope_shift, 1) * sb_ref[...])
        if mul != 1.0:
            x = x * mul
        o_ref[:, c * LANES:(c + 1) * LANES] = x.astype(o_ref.dtype)


def _head_prep(x, gain128, tables, seq, out_dtype, do_norm, rope_shift, mul=1.0, tm=256):
    t, w = x.shape
    tm = min(tm, seq)
    nseq = seq // tm
    cos, sa, sb = tables
    kern = functools.partial(_head_prep_kernel, do_norm=do_norm, rope_shift=rope_shift, mul=mul)
    tab_spec = pl.BlockSpec((tm, LANES), lambda i: (i % nseq, 0))
    return pl.pallas_call(
        kern,
        grid=(t // tm,),
        in_specs=[
            pl.BlockSpec((tm, w), lambda i: (i, 0)),
            pl.BlockSpec((1, LANES), lambda i: (0, 0)),
            tab_spec, tab_spec, tab_spec,
        ],
        out_specs=pl.BlockSpec((tm, w), lambda i: (i, 0)),
        out_shape=jax.ShapeDtypeStruct((t, w), out_dtype),
        compiler_params=_params(("arbitrary",)),
        name="head_prep",
    )(x, gain128, cos, sa, sb)


def _axial_tables(seq, dims, lane0):
    half = dims // 2
    quarter = half // 2
    inv = ROPE_THETA ** (-jnp.arange(0, half, 2, dtype=F32) / half)
    t = jnp.arange(seq)
    ang_r = (t // GRID_W).astype(F32)[:, None] * inv
    ang_c = (t % GRID_W).astype(F32)[:, None] * inv
    cos = jnp.concatenate([jnp.cos(ang_r)] * 2 + [jnp.cos(ang_c)] * 2, axis=-1)
    sin = jnp.concatenate([jnp.sin(ang_r)] * 2 + [jnp.sin(ang_c)] * 2, axis=-1)
    first = (jnp.arange(dims) % half) < quarter
    sa = jnp.where(first, -sin, 0.0)
    sb = jnp.where(first, 0.0, sin)
    if dims == HEAD_DIM:
        return tuple(jnp.concatenate([a, a], axis=-1) for a in (cos, sa, sb))
    pad_l, pad_r = lane0, LANES - lane0 - dims
    cos = jnp.pad(cos, ((0, 0), (pad_l, pad_r)), constant_values=1.0)
    sa = jnp.pad(sa, ((0, 0), (pad_l, pad_r)))
    sb = jnp.pad(sb, ((0, 0), (pad_l, pad_r)))
    return cos, sa, sb


def _attend_t(streams, k_refs, vt_ref, s_ref, p_ref, tk):
    chunks = []
    off = 0
    for k_ref in k_refs:
        n = k_ref.shape[0]
        assert n % tk == 0
        for c in range(n // tk):
            k_fn = lambda i, k_ref=k_ref, r=c * tk: k_ref[r:r + tk, streams[i][1]].astype(BF16)
            vt_fn = lambda col=off + c * tk: vt_ref[:, col:col + tk]
            chunks.append((k_fn, vt_fn, None))
        off += n
    return _pipeline([q_t for q_t, _ in streams], chunks, s_ref, p_ref)


def _pipeline(q_ts, chunks, s_ref, p_ref):
    nc = len(chunks)
    ns = len(q_ts)
    mq = q_ts[0].shape[1]
    m = [jnp.full((1, mq), -jnp.inf, F32) for _ in range(ns)]
    l = [jnp.zeros((1, mq), F32) for _ in range(ns)]
    acc = [jnp.zeros((LANES, mq), F32) for _ in range(ns)]
    m_chunk = [None] * ns
    alpha = [[None] * ns for _ in range(2)]
    for t in range(nc + 2):
        if t < nc:
            k_fn, _, bias_fn = chunks[t]
            for i, q_t in enumerate(q_ts):
                s = jnp.dot(k_fn(i), q_t, preferred_element_type=F32)
                if bias_fn is not None:
                    s = s + bias_fn(i)
                m_chunk[i] = jnp.max(s, axis=0, keepdims=True)
                s_ref[t % 2, i] = s
        if 0 <= t - 2 < nc:
            vt = chunks[t - 2][1]()
            for i in range(ns):
                pv = jnp.dot(vt, p_ref[t % 2, i], preferred_element_type=F32)
                acc[i] = alpha[t % 2][i] * acc[i] + pv
        if 0 <= t - 1 < nc:
            for i in range(ns):
                m_new = jnp.maximum(m[i], m_prev_chunk[i])
                alpha[(t - 1) % 2][i] = jnp.exp2(m[i] - m_new)
                p = jnp.exp2(s_ref[(t - 1) % 2, i] - m_new)
                l[i] = alpha[(t - 1) % 2][i] * l[i] + jnp.sum(p, axis=0, keepdims=True)
                p_ref[(t - 1) % 2, i] = p.astype(BF16)
                m[i] = m_new
        m_prev_chunk = list(m_chunk)
    return [acc[i] / l[i] for i in range(ns)]


def _fill_vt(vt_ref, v_refs, chunk=512):
    off = 0
    for v_ref in v_refs:
        n = v_ref.shape[0]
        step = min(chunk, n)
        for c in range(n // step):
            blk = v_ref[c * step:(c + 1) * step, :].astype(F32)
            vt_ref[:, off + c * step:off + (c + 1) * step] = blk.T.astype(vt_ref.dtype)
        off += n


def _attn_scratch(n_total, n_streams, tk, mq):
    return [pltpu.VMEM((LANES, n_total), BF16),
            pltpu.VMEM((2, n_streams, tk, mq), F32),
            pltpu.VMEM((2, n_streams, tk, mq), BF16)]


def _gqa_kernel(*refs, n_src, tk):
    q_ref, o_ref, vt_ref, s_ref, p_ref = refs[0], refs[-4], refs[-3], refs[-2], refs[-1]
    k_refs = [refs[1 + 2 * i] for i in range(n_src)]
    v_refs = [refs[2 + 2 * i] for i in range(n_src)]
    tq = q_ref.shape[0]

    @pl.when(pl.program_id(1) == 0)
    def _():
        _fill_vt(vt_ref, v_refs)

    q_blocks = [q_ref[:, j * LANES:(j + 1) * LANES].astype(F32).T for j in range(4)]
    zeros = jnp.zeros((HEAD_DIM, tq), F32)
    streams = []
    for g in range(2):
        cols = []
        for j in (2 * g, 2 * g + 1):
            for part in (q_blocks[j][:HEAD_DIM], q_blocks[j][HEAD_DIM:]):
                cols.append(jnp.concatenate([part, zeros] if g == 0 else [zeros, part], axis=0))
        streams.append((jnp.concatenate(cols, axis=1).astype(BF16), slice(None)))
    outs = _attend_t(streams, k_refs, vt_ref, s_ref, p_ref, tk)
    for g in range(2):
        o_t = outs[g][g * HEAD_DIM:(g + 1) * HEAD_DIM]
        for idx, j in enumerate((2 * g, 2 * g + 1)):
            blk_t = jnp.concatenate([o_t[:, (2 * idx) * tq:(2 * idx + 1) * tq],
                                     o_t[:, (2 * idx + 1) * tq:(2 * idx + 2) * tq]], axis=0)
            o_ref[:, j * LANES:(j + 1) * LANES] = blk_t.T.astype(o_ref.dtype)


def _gqa_attention(q, kv, batch, tq, tk=512):
    t, w = q.shape
    s = t // batch
    tq = min(tq, s)
    nq = s // tq
    tk = min([tk] + [k.shape[1] for k, _ in kv])
    in_specs = [pl.BlockSpec((tq, w), lambda b, i: (b * nq + i, 0))]
    args = [q]
    n_total = 0
    for k, v in kv:
        n = k.shape[1]
        n_total += n
        spec = pl.BlockSpec((None, n, LANES), lambda b, i: (b, 0, 0))
        in_specs += [spec, spec]
        args += [k, v]
    return pl.pallas_call(
        functools.partial(_gqa_kernel, n_src=len(kv), tk=tk),
        grid=(batch, nq),
        in_specs=in_specs,
        out_specs=pl.BlockSpec((tq, w), lambda b, i: (b * nq + i, 0)),
        out_shape=jax.ShapeDtypeStruct((t, w), BF16),
        scratch_shapes=_attn_scratch(n_total, 2, tk, 4 * tq),
        compiler_params=_params(("arbitrary", "arbitrary")),
        name="gqa_attention",
    )(*args)


def _pair_kernel(*refs, n_src, tk, dk):
    q_ref, o_ref, vt_ref, s_ref, p_ref = refs[0], refs[-4], refs[-3], refs[-2], refs[-1]
    k_refs = [refs[1 + 2 * i] for i in range(n_src)]
    v_refs = [refs[2 + 2 * i] for i in range(n_src)]
    tq = q_ref.shape[0]

    @pl.when(pl.program_id(2) == 0)
    def _():
        _fill_vt(vt_ref, v_refs)

    top = lax.broadcasted_iota(jnp.int32, (LANES, tq), 0) < HEAD_DIM
    streams = []
    for h in range(2):
        if dk == HEAD_DIM:
            both = q_ref[...].astype(F32).T
            q_t = (jnp.where(top, both, 0.0) if h == 0 else jnp.where(top, 0.0, both)).astype(BF16)
            streams.append((q_t, slice(None)))
        else:
            q_t = q_ref[:, h * LANES:(h + 1) * LANES].astype(F32).T.astype(BF16)
            streams.append((q_t, slice(h * LANES, (h + 1) * LANES)))
    outs = _attend_t(streams, k_refs, vt_ref, s_ref, p_ref, tk)
    o_ref[...] = jnp.where(top, outs[0], outs[1]).T.astype(o_ref.dtype)


def _pair_attention(q, kv, batch, dk, tq, tk=512):
    t, w = q.shape
    qw = LANES if dk == HEAD_DIM else 2 * LANES
    pairs = w // qw
    s = t // batch
    tq = min(tq, s)
    nq = s // tq
    tk = min([tk] + [k.shape[1] for k, _ in kv])
    in_specs = [pl.BlockSpec((tq, qw), lambda b, p, i: (b * nq + i, p))]
    args = [q]
    n_total = 0
    for k, v in kv:
        n = k.shape[1]
        n_total += n
        in_specs += [pl.BlockSpec((None, n, qw), lambda b, p, i: (b, 0, p)),
                     pl.BlockSpec((None, n, LANES), lambda b, p, i: (b, 0, p))]
        args += [k, v]
    return pl.pallas_call(
        functools.partial(_pair_kernel, n_src=len(kv), tk=tk, dk=dk),
        grid=(batch, pairs, nq),
        in_specs=in_specs,
        out_specs=pl.BlockSpec((tq, LANES), lambda b, p, i: (b * nq + i, p)),
        out_shape=jax.ShapeDtypeStruct((t, pairs * LANES), BF16),
        scratch_shapes=_attn_scratch(n_total, 2, tk, tq),
        compiler_params=_params(("arbitrary", "arbitrary", "arbitrary")),
        name="pair_attention",
    )(*args)


NB_QROWS = 4
NB_KROWS = 12
NB_TK = NB_QROWS * GRID_W
NB_INVALID = 2 * NA_ROWS - 1


def _nb_kernel(q_ref, kc_ref, vc_ref, k_ref, v_ref, tab_ref, o_ref, vtc_ref, vt_ref, bias_ref, s_ref, p_ref):
    i = pl.program_id(2)
    rows = k_ref.shape[0] // GRID_W
    half = NA_ROWS // 2
    a = i * NB_QROWS
    base_blk = jnp.clip(i - half // NB_QROWS, 0, (rows - NB_KROWS) // NB_QROWS)
    base = base_blk * NB_QROWS
    tq = q_ref.shape[0]

    @pl.when(i == 0)
    def _():
        for ref, src in ((vtc_ref, vc_ref), (vt_ref, v_ref)):
            for c in range(ref.shape[0]):
                ref[c] = src[c * NB_TK:(c + 1) * NB_TK, :].astype(F32).T.astype(ref.dtype)

    def unclipped(first_row):
        return jnp.logical_and(first_row - half >= 0, first_row + NB_QROWS - 1 - half <= rows - NA_ROWS)

    @pl.when(jnp.logical_not(jnp.logical_and(unclipped(a), unclipped(a - NB_QROWS))))
    def _():
        lo_tile = lax.broadcasted_iota(jnp.int32, (GRID_W, LANES), 1) < HEAD_DIM
        for h in range(2):
            for j in range(NB_KROWS):
                kr = base + j
                for ip in range(NB_QROWS // 2):
                    halves = []
                    for qi in (2 * ip, 2 * ip + 1):
                        qr = a + qi
                        r0 = jnp.clip(qr - half, 0, rows - NA_ROWS)
                        valid = jnp.logical_and(kr >= r0, kr < r0 + NA_ROWS)
                        halves.append(tab_ref[h, jnp.where(valid, kr - qr + NA_ROWS - 1, NB_INVALID)])
                    bias_ref[h, j * GRID_W:(j + 1) * GRID_W, ip * LANES:(ip + 1) * LANES] = (
                        jnp.where(lo_tile, halves[0], halves[1]))

    top = lax.broadcasted_iota(jnp.int32, (LANES, tq), 0) < HEAD_DIM
    both = q_ref[...].astype(F32).T
    q_ts = [jnp.where(top, both, 0.0).astype(BF16), jnp.where(top, 0.0, both).astype(BF16)]
    chunks = []
    for c in range(vtc_ref.shape[0]):
        chunks.append((lambda i_, c=c: kc_ref[c * NB_TK:(c + 1) * NB_TK, :].astype(BF16),
                       lambda c=c: vtc_ref[c], None))
    for c in range(NB_KROWS // NB_QROWS):
        start = pl.multiple_of((base + c * NB_QROWS) * GRID_W, NB_TK)
        chunks.append((lambda i_, start=start: k_ref[pl.ds(start, NB_TK), :].astype(BF16),
                       lambda c=c: vt_ref[base_blk + c],
                       lambda i_, c=c: bias_ref[i_, c * NB_TK:(c + 1) * NB_TK, :]))
    outs = _pipeline(q_ts, chunks, s_ref, p_ref)
    o_ref[...] = jnp.where(top, outs[0], outs[1]).T.astype(o_ref.dtype)


def _nb_attention(q, k, v, kctx, vctx, table, batch):
    t, w = q.shape
    s = t // batch
    pairs = w // LANES
    tq = NB_TK
    nq = s // tq
    l = kctx.shape[1]
    assert l % NB_TK == 0 and (s // GRID_W) % NB_QROWS == 0
    return pl.pallas_call(
        _nb_kernel,
        grid=(batch, pairs, nq),
        in_specs=[
            pl.BlockSpec((tq, LANES), lambda b, p, i: (b * nq + i, p)),
            pl.BlockSpec((None, l, LANES), lambda b, p, i: (b, 0, p)),
            pl.BlockSpec((None, l, LANES), lambda b, p, i: (b, 0, p)),
            pl.BlockSpec((None, s, LANES), lambda b, p, i: (b, 0, p)),
            pl.BlockSpec((None, s, LANES), lambda b, p, i: (b, 0, p)),
            pl.BlockSpec((2, NB_INVALID + 1, GRID_W, LANES), lambda b, p, i: (p, 0, 0, 0)),
        ],
        out_specs=pl.BlockSpec((tq, LANES), lambda b, p, i: (b * nq + i, p)),
        out_shape=jax.ShapeDtypeStruct((t, w), BF16),
        scratch_shapes=[pltpu.VMEM((l // NB_TK, LANES, NB_TK), BF16),
                        pltpu.VMEM((s // NB_TK, LANES, NB_TK), BF16),
                        pltpu.VMEM((2, NB_KROWS * GRID_W, tq), F32),
                        pltpu.VMEM((2, 2, NB_TK, tq), F32),
                        pltpu.VMEM((2, 2, NB_TK, tq), BF16)],
        compiler_params=_params(("arbitrary", "arbitrary", "arbitrary")),
        name="nb_attention",
    )(q, kctx, vctx, k, v, table)


def _nb_bias_table(rpb):
    h = rpb.shape[0]
    qc = jnp.arange(GRID_W)[None, :]
    kc = jnp.arange(GRID_W)[:, None]
    c0 = jnp.clip(qc - NA_COLS // 2, 0, GRID_W - NA_COLS)
    inwin = jnp.logical_and(kc >= c0, kc < c0 + NA_COLS)
    dc = jnp.clip(kc - qc + NA_COLS - 1, 0, 2 * NA_COLS - 2)
    t = jnp.where(inwin[None, None], rpb[:, :, dc].astype(F32) * LOG2E, NEG)
    t = jnp.concatenate([t, jnp.full((h, 1, GRID_W, GRID_W), NEG, F32)], axis=1)
    return jnp.concatenate([t, t], axis=-1)


PAD = 8


def _pool_kernel(u_ref, map_ref, sc_ref, o_ref, pad_ref):
    s = u_ref.shape[0]
    g = pl.program_id(1)
    u = u_ref[...]
    pad_ref[0:PAD, :] = jnp.zeros((PAD, LANES), F32)
    pad_ref[PAD + s:PAD + s + PAD, :] = jnp.zeros((PAD, LANES), F32)
    pad_ref[PAD:PAD + s, :] = u
    t = lax.broadcasted_iota(jnp.int32, (s, LANES), 0)
    for gi, win in enumerate(POOL_WINDOWS):
        @pl.when(g == gi)
        def _():
            acc = jnp.zeros((s, LANES), F32)
            for d in range(-(win // 2), win - win // 2):
                acc = acc + pad_ref[PAD + d:PAD + d + s, :]
            cnt = jnp.minimum(t + (win - win // 2), s) - jnp.maximum(t - win // 2, 0)
            diff = acc / cnt.astype(F32) - u
            y = jnp.dot(diff.astype(BF16), map_ref[...].astype(BF16), preferred_element_type=F32)
            o_ref[...] = (y * sc_ref[...]).astype(o_ref.dtype)


def _pool(u, b_map, b_scale, batch):
    t, w = u.shape
    s = t // batch
    groups = w // LANES
    return pl.pallas_call(
        _pool_kernel,
        grid=(batch, groups),
        in_specs=[
            pl.BlockSpec((s, LANES), lambda b, g: (b, g)),
            pl.BlockSpec((None, LANES, LANES), lambda b, g: (g, 0, 0)),
            pl.BlockSpec((1, LANES), lambda b, g: (0, g)),
        ],
        out_specs=pl.BlockSpec((s, LANES), lambda b, g: (b, g)),
        out_shape=jax.ShapeDtypeStruct((t, w), BF16),
        scratch_shapes=[pltpu.VMEM((s + 2 * PAD, LANES), F32)],
        compiler_params=_params(("arbitrary", "arbitrary")),
        name="pool",
    )(u, b_map, b_scale.reshape(1, w))


def _out_kernel(x_ref, a_ref, ga_ref, b_ref, gb_ref, w_ref, g_ref, gate_ref, o_ref):
    half = a_ref.shape[1]
    ta = (a_ref[...].astype(F32) * _silu(ga_ref[...].astype(F32))).astype(BF16)
    tb = (b_ref[...].astype(F32) * _silu(gb_ref[...].astype(F32))).astype(BF16)
    y = (jnp.dot(ta, w_ref[0:half, :], preferred_element_type=F32)
         + jnp.dot(tb, w_ref[half:2 * half, :], preferred_element_type=F32))
    ms = jnp.mean(y * y, axis=-1, keepdims=True)
    o_ref[...] = x_ref[...] + gate_ref[...] * (y * lax.rsqrt(ms + EPS) * g_ref[...])


def _out_proj(x, a, ga, b, gb, w, g_post, gate, rows_per_mod, tm=512):
    t, d = x.shape
    half = a.shape[1]
    tm = min(tm, t)
    br = pl.BlockSpec((tm, half), lambda i: (i, 0))
    return pl.pallas_call(
        _out_kernel,
        grid=(t // tm,),
        in_specs=[
            pl.BlockSpec((tm, d), lambda i: (i, 0)),
            br, br, br, br,
            pl.BlockSpec((2 * half, d), lambda i: (0, 0)),
            pl.BlockSpec((1, d), lambda i: (0, 0)),
            pl.BlockSpec((None, 1, d), lambda i: ((i * tm) // rows_per_mod, 0, 0)),
        ],
        out_specs=pl.BlockSpec((tm, d), lambda i: (i, 0)),
        out_shape=jax.ShapeDtypeStruct((t, d), F32),
        compiler_params=_params(("arbitrary",)),
        name="out_proj",
    )(x, a, ga, b, gb, w, g_post, gate)


def _kvup_kernel(c_ref, g_ref, pe_ref, wk_ref, wv_ref, cn_ref, k_ref, v_ref, *, do_norm):
    c = c_ref[...]
    if do_norm:
        ms = jnp.mean(c * c, axis=-1, keepdims=True)
        c = c * lax.rsqrt(ms + EPS) * g_ref[...]
    cn_ref[...] = c
    cb = c.astype(BF16)
    heads = k_ref.shape[1] // LANES
    k = jnp.dot(cb, wk_ref[...], preferred_element_type=F32)
    k = k + jnp.concatenate([pe_ref[...]] * heads, axis=1)
    k_ref[...] = k.astype(k_ref.dtype)
    v_ref[...] = jnp.dot(cb, wv_ref[...], preferred_element_type=F32).astype(v_ref.dtype)


def _kvup(ckv, g, pe128, wk, wv, do_norm, tm=256):
    t, r = ckv.shape
    tm = min(tm, t)
    nk, nv = wk.shape[1], wv.shape[1]
    row = lambda width: pl.BlockSpec((tm, width), lambda i: (i, 0))
    full = lambda a: pl.BlockSpec(a.shape, lambda i: (0, 0))
    return pl.pallas_call(
        functools.partial(_kvup_kernel, do_norm=do_norm),
        grid=(t // tm,),
        in_specs=[row(r), full(g), row(LANES), full(wk), full(wv)],
        out_specs=[row(r), row(nk), row(nv)],
        out_shape=[jax.ShapeDtypeStruct((t, r), F32), jax.ShapeDtypeStruct((t, nk), BF16),
                   jax.ShapeDtypeStruct((t, nv), BF16)],
        compiler_params=_params(("arbitrary",)),
        name="kv_up",
    )(ckv, g, pe128, wk, wv)


def kernel(x_prompt, x_sample, cache_a_k, cache_a_v, cache_c_k, cache_c_v, cache_d_ckv, cache_d_kpe,
           c, c_ctx, w_mod, b_mod, g_pre, g_post, w_in_e, a_q_norm, a_k_norm, b_map, b_scale, w_out_e,
           w_in_o, c_rpb, d_q_norm, d_w_uq, d_kv_norm, d_w_ukv, w_out_o):
    bp, sp, d = x_prompt.shape
    bs, ss, _ = x_sample.shape
    past = cache_a_k.shape[2]
    depth = w_mod.shape[0]
    tp, ts = bp * sp, bs * ss
    branch = d // 2
    d_heads = 8
    d_nope, d_rope, d_v = 64, 32, 64
    q_lora, kv_lora = d_q_norm.shape[1], d_kv_norm.shape[1]
    kv_w = a_k_norm.shape[1] * 2

    xp = x_prompt.reshape(tp, d)
    xs = x_sample.reshape(ts, d)

    cond = jnp.concatenate([c_ctx[None, :], c, jnp.zeros((8 - 1 - bs, d), F32)], axis=0)
    mods = _mod_call(cond, w_mod, b_mod)

    def mod_rows(layer, part, lo_row, n_rows):
        return mods[layer, lo_row:lo_row + n_rows, part * d:(part + 1) * d].reshape(n_rows, 1, d)

    rope_a = _axial_tables(ss, HEAD_DIM, 0)
    rope_d = _axial_tables(ss, d_rope, d_nope)
    no_rope_p = tuple(jnp.zeros((sp, LANES), F32) for _ in range(3))
    zeros_mod = jnp.zeros((1, 1, q_lora), F32)

    outs = {name: [] for name in ("a_k", "a_v", "c_k", "c_v", "d_ckv", "d_kpe")}
    for layer in range(depth):
        j = layer // 2
        g_in = g_pre[layer].reshape(1, d)
        g_out = g_post[layer].reshape(1, d)
        paths = (("p", xp, bp, sp, 0, 1, tp), ("s", xs, bs, ss, 1, bs, ss))
        if layer % 2 == 0:
            w_in = w_in_e[j].astype(BF16)
            w_out = w_out_e[j].astype(BF16)
            qg = jnp.concatenate([a_q_norm[j]] * 2).reshape(1, LANES)
            kg = jnp.concatenate([a_k_norm[j]] * 2).reshape(1, LANES)
            o0 = branch
            segs = [(0, branch, F32, 1.0), (o0, kv_w, F32, 1.0), (o0 + kv_w, kv_w, F32, 1.0),
                    (o0 + 2 * kv_w, branch, BF16, 1.0), (o0 + 2 * kv_w + branch, branch, F32, 1.0),
                    (o0 + 2 * kv_w + 2 * branch, branch, BF16, 1.0)]
            new_x = []
            for name, x, nb, seq, row0, nrows, rpm in paths:
                shift, scale, gate = (mod_rows(layer, part, row0, nrows) for part in range(3))
                q, k, v, ga, ub, gb = _norm_mm(x, g_in, scale, shift, w_in, segs, rpm, name="in_proj_even")
                if name == "p":
                    qn = _head_prep(q, qg, no_rope_p, seq, BF16, True, 0, mul=LOG2E * HEAD_DIM ** -0.5)
                    kn = _head_prep(k, kg, no_rope_p, seq, F32, True, 0)
                    outs["a_k"].append(kn)
                    outs["a_v"].append(v)
                    kv = [(kn.reshape(nb, seq, kv_w), v.reshape(nb, seq, kv_w))]
                    o_a = _gqa_attention(qn, kv, nb, tq=256)
                else:
                    qn = _head_prep(q, qg, rope_a, seq, BF16, True, 16, mul=LOG2E * HEAD_DIM ** -0.5)
                    kn = _head_prep(k, kg, rope_a, seq, BF16, True, 16)
                    kv = [(cache_a_k[:, j].reshape(nb, past, kv_w), cache_a_v[:, j].reshape(nb, past, kv_w)),
                          (kn.reshape(nb, seq, kv_w), v.reshape(nb, seq, kv_w))]
                    o_a = _gqa_attention(qn, kv, nb, tq=128)
                o_b = _pool(ub, b_map[j], b_scale[j], nb)
                new_x.append(_out_proj(x, o_a, ga, o_b, gb, w_out, g_out, gate, rpm))
            xp, xs = new_x
        else:
            pad = lambda a, n: jnp.pad(a, ((0, 0), (0, n)))
            wi = w_in_o[j]
            off_kpe = 4 * branch + q_lora + kv_lora
            w_in = jnp.concatenate([wi[:, :off_kpe], jnp.zeros((d, d_nope), F32),
                                    wi[:, off_kpe:off_kpe + d_rope],
                                    jnp.zeros((d, LANES - d_nope - d_rope), F32),
                                    wi[:, off_kpe + d_rope:]], axis=1).astype(BF16)
            w_out = w_out_o[j].astype(BF16)
            dqk = d_nope + d_rope
            w_uq = jnp.pad(d_w_uq[j].reshape(q_lora, d_heads, dqk),
                           ((0, 0), (0, 0), (0, LANES - dqk))).reshape(q_lora, d_heads * LANES).astype(BF16)
            ukv = d_w_ukv[j].reshape(kv_lora, d_heads, d_nope + d_v)
            w_uk = jnp.pad(ukv[:, :, :d_nope], ((0, 0), (0, 0), (0, LANES - d_nope))
                           ).reshape(kv_lora, d_heads * LANES).astype(BF16)
            w_uv = ukv[:, :, d_nope:].reshape(kv_lora, d_heads * d_v).astype(BF16)
            qn_g = d_q_norm[j].reshape(1, q_lora)
            kvn_g = d_kv_norm[j].reshape(1, kv_lora)
            table = _nb_bias_table(c_rpb[j])
            b4 = 4 * branch
            new_x = []
            for name, x, nb, seq, row0, nrows, rpm in paths:
                shift, scale, gate = (mod_rows(layer, part, row0, nrows) for part in range(3))
                kv_dt = F32 if name == "p" else BF16
                segs = [(0, branch, BF16, LOG2E * HEAD_DIM ** -0.5), (branch, branch, kv_dt, 1.0),
                        (2 * branch, branch, kv_dt, 1.0), (3 * branch, branch, BF16, 1.0),
                        (b4, q_lora, F32, 1.0), (b4 + q_lora, kv_lora, F32, 1.0),
                        (b4 + q_lora + kv_lora, LANES, F32, 1.0),
                        (b4 + q_lora + kv_lora + LANES, branch, BF16, 1.0)]
                qc, kc, vc, gc, cq, ckv, kpe, gd = _norm_mm(x, g_in, scale, shift, w_in, segs, rpm,
                                                            name="in_proj_odd")
                (qd,) = _norm_mm(cq, qn_g, zeros_mod, zeros_mod, w_uq,
                                 [(0, d_heads * LANES, BF16 if name == "p" else F32, LOG2E * dqk ** -0.5)],
                                 cq.shape[0], name="q_up")
                if name == "p":
                    outs["c_k"].append(kc)
                    outs["c_v"].append(vc)
                    ckv_n, k_d, v_d = _kvup(ckv, kvn_g, kpe, w_uk, w_uv, True)
                    outs["d_ckv"].append(ckv_n)
                    outs["d_kpe"].append(kpe[:, d_nope:d_nope + d_rope])
                    o_c = _pair_attention(qc, [(kc.reshape(nb, seq, branch), vc.reshape(nb, seq, branch))],
                                          nb, HEAD_DIM, tq=256)
                    o_d = _pair_attention(qd, [(k_d.reshape(nb, seq, -1), v_d.reshape(nb, seq, -1))],
                                          nb, 2 * HEAD_DIM, tq=256)
                else:
                    o_c = _nb_attention(qc, kc.reshape(nb, seq, branch), vc.reshape(nb, seq, branch),
                                        cache_c_k[:, j].reshape(nb, past, branch),
                                        cache_c_v[:, j].reshape(nb, past, branch), table, nb)
                    ones = jnp.ones((1, LANES), F32)
                    qd = _head_prep(qd, ones, rope_d, seq, BF16, False, 8)
                    kpe_r = _head_prep(kpe, ones, rope_d, seq, F32, False, 8)
                    _, k_new, v_new = _kvup(ckv, kvn_g, kpe_r, w_uk, w_uv, True)
                    pe_ctx = jnp.pad(cache_d_kpe[:, j].reshape(nb * past, d_rope),
                                     ((0, 0), (d_nope, LANES - d_nope - d_rope)))
                    _, k_ctx, v_ctx = _kvup(cache_d_ckv[:, j].reshape(nb * past, kv_lora), kvn_g, pe_ctx,
                                            w_uk, w_uv, False)
                    kv = [(k_ctx.reshape(nb, past, -1), v_ctx.reshape(nb, past, -1)),
                          (k_new.reshape(nb, seq, -1), v_new.reshape(nb, seq, -1))]
                    o_d = _pair_attention(qd, kv, nb, 2 * HEAD_DIM, tq=256)
                new_x.append(_out_proj(x, o_c, gc, o_d, gd, w_out, g_out, gate, rpm))
            xp, xs = new_x

    kvh = kv_w // HEAD_DIM
    ch = branch // HEAD_DIM

    def stacked(name, *tail):
        return jnp.stack([a.reshape(bp, sp, *tail) for a in outs[name]], axis=1)

    return (xp.reshape(bp, sp, d), xs.reshape(bs, ss, d),
            stacked("a_k", kvh, HEAD_DIM), stacked("a_v", kvh, HEAD_DIM),
            stacked("c_k", ch, HEAD_DIM), stacked("c_v", ch, HEAD_DIM),
            stacked("d_ckv", kv_lora), stacked("d_kpe", d_rope))
```

```python
import functools

import jax
import jax.numpy as jnp
from jax import lax
from jax.experimental import pallas as pl
from jax.experimental.pallas import tpu as pltpu

F32 = jnp.float32
BF16 = jnp.bfloat16

LANES = 128
GRID_W = 64
HEAD_DIM = 64
NA_ROWS = 8
NA_COLS = 16
POOL_WINDOWS = (2, 4, 8, 16)
ROPE_THETA = 10000.0
EPS = 1e-6
NEG = -1e30
LOG2E = 1.4426950408889634
VMEM_LIMIT = 56 * 1024 * 1024


def _params(sem):
    return pltpu.CompilerParams(dimension_semantics=sem, vmem_limit_bytes=VMEM_LIMIT)


def _silu(x):
    return x / (1.0 + jnp.exp(-x))


def _rms_lanes(x, gain):
    ms = jnp.mean(x * x, axis=-1, keepdims=True)
    return x * lax.rsqrt(ms + EPS) * gain


def _mod_kernel(c_ref, w_ref, b_ref, o_ref):
    s = _silu(c_ref[...]).astype(BF16)
    o_ref[...] = jnp.dot(s, w_ref[...].astype(BF16), preferred_element_type=F32) + b_ref[...]


def _mod_call(cond, w_mod, b_mod):
    depth, d, n = w_mod.shape
    rows = cond.shape[0]
    tn = 1024
    return pl.pallas_call(
        _mod_kernel,
        grid=(depth, n // tn),
        in_specs=[
            pl.BlockSpec((rows, d), lambda l, j: (0, 0)),
            pl.BlockSpec((None, d, tn), lambda l, j: (l, 0, j)),
            pl.BlockSpec((None, 1, tn), lambda l, j: (l, 0, j)),
        ],
        out_specs=pl.BlockSpec((None, rows, tn), lambda l, j: (l, 0, j)),
        out_shape=jax.ShapeDtypeStruct((depth, rows, n), F32),
        compiler_params=_params(("arbitrary", "arbitrary")),
        name="adaln_mod",
    )(cond, w_mod, b_mod.reshape(depth, 1, n))


def _norm_mm_kernel(x_ref, g_ref, sc_ref, sh_ref, w_ref, *o_refs, segs, do_norm):
    x = x_ref[...].astype(F32)
    if do_norm:
        x = _rms_lanes(x, g_ref[...]) * (1.0 + sc_ref[...]) + sh_ref[...]
    z = jnp.dot(x.astype(BF16), w_ref[...], preferred_element_type=F32)
    for o_ref, (start, width) in zip(o_refs, segs):
        o_ref[...] = z[:, start:start + width].astype(o_ref.dtype)


def _norm_mm(x, g, scale, shift, w, segs, rows_per_mod, do_norm=True, tm=512, name="norm_mm"):
    t, k = x.shape
    n = w.shape[1]
    tm = min(tm, t)
    kern = functools.partial(_norm_mm_kernel, segs=tuple((s, wd) for s, wd, _ in segs), do_norm=do_norm)
    mod_map = lambda i: ((i * tm) // rows_per_mod, 0, 0)
    return pl.pallas_call(
        kern,
        grid=(t // tm,),
        in_specs=[
            pl.BlockSpec((tm, k), lambda i: (i, 0)),
            pl.BlockSpec((1, k), lambda i: (0, 0)),
            pl.BlockSpec((None, 1, k), mod_map),
            pl.BlockSpec((None, 1, k), mod_map),
            pl.BlockSpec((k, n), lambda i: (0, 0)),
        ],
        out_specs=[pl.BlockSpec((tm, wd), lambda i: (i, 0)) for _, wd, _ in segs],
        out_shape=[jax.ShapeDtypeStruct((t, wd), dt) for _, wd, dt in segs],
        compiler_params=_params(("arbitrary",)),
        name=name,
    )(x, g, scale, shift, w)


def _in_odd_kernel(x_ref, g_ref, sc_ref, sh_ref, w_ref, qg_ref, kvg_ref, wuq_ref, wuk_ref, wuv_ref,
                   cos_ref, sa_ref, sb_ref,
                   qc_ref, kc_ref, vc_ref, gc_ref, gd_ref, qd_ref, ckv_ref, kpe_ref, kd_ref, vd_ref,
                   *, branch, q_lora, kv_lora, rope_shift, qc_mul, qd_mul):
    x = _rms_lanes(x_ref[...], g_ref[...]) * (1.0 + sc_ref[...]) + sh_ref[...]
    z = jnp.dot(x.astype(BF16), w_ref[...], preferred_element_type=F32)
    b = branch
    qc_ref[...] = (z[:, 0:b] * qc_mul).astype(qc_ref.dtype)
    kc_ref[...] = z[:, b:2 * b].astype(kc_ref.dtype)
    vc_ref[...] = z[:, 2 * b:3 * b].astype(vc_ref.dtype)
    gc_ref[...] = z[:, 3 * b:4 * b].astype(gc_ref.dtype)
    o = 4 * b
    cq = _rms_lanes(z[:, o:o + q_lora], qg_ref[...])
    qd = jnp.dot(cq.astype(BF16), wuq_ref[...], preferred_element_type=F32)
    qd_ref[...] = (qd * qd_mul).astype(qd_ref.dtype)
    o += q_lora
    ckv = _rms_lanes(z[:, o:o + kv_lora], kvg_ref[...])
    ckv_ref[...] = ckv
    o += kv_lora
    kpe = z[:, o:o + LANES]
    if rope_shift:
        kpe = (kpe * cos_ref[...] + pltpu.roll(kpe, LANES - rope_shift, 1) * sa_ref[...]
               + pltpu.roll(kpe, rope_shift, 1) * sb_ref[...])
    kpe_ref[...] = kpe
    o += LANES
    gd_ref[...] = z[:, o:o + b].astype(gd_ref.dtype)
    cb = ckv.astype(BF16)
    heads = kd_ref.shape[1] // LANES
    kd = jnp.dot(cb, wuk_ref[...], preferred_element_type=F32) + jnp.concatenate([kpe] * heads, axis=1)
    kd_ref[...] = kd.astype(kd_ref.dtype)
    vd_ref[...] = jnp.dot(cb, wuv_ref[...], preferred_element_type=F32).astype(vd_ref.dtype)


def _in_odd(x, g, scale, shift, w, qg, kvg, wuq, wuk, wuv, tables, rows_per_mod, seq, rope_shift,
            kv_dtype, qd_dtype, branch, qc_mul, qd_mul, tm=512):
    t, k = x.shape
    tm = min(tm, seq)
    nseq = seq // tm
    q_lora, kv_lora = qg.shape[1], kvg.shape[1]
    widths = [(branch, BF16), (branch, kv_dtype), (branch, kv_dtype), (branch, BF16), (branch, BF16),
              (wuq.shape[1], qd_dtype), (kv_lora, F32), (LANES, F32), (wuk.shape[1], BF16), (wuv.shape[1], BF16)]
    mod_map = lambda i: ((i * tm) // rows_per_mod, 0, 0)
    full = lambda a: pl.BlockSpec(a.shape, lambda i: (0, 0))
    tab_spec = pl.BlockSpec((tm, LANES), lambda i: (i % nseq, 0))
    kern = functools.partial(_in_odd_kernel, branch=branch, q_lora=q_lora, kv_lora=kv_lora,
                             rope_shift=rope_shift, qc_mul=qc_mul, qd_mul=qd_mul)
    return pl.pallas_call(
        kern,
        grid=(t // tm,),
        in_specs=[pl.BlockSpec((tm, k), lambda i: (i, 0)), full(g),
                  pl.BlockSpec((None, 1, k), mod_map), pl.BlockSpec((None, 1, k), mod_map),
                  full(w), full(qg), full(kvg), full(wuq), full(wuk), full(wuv),
                  tab_spec, tab_spec, tab_spec],
        out_specs=[pl.BlockSpec((tm, wd), lambda i: (i, 0)) for wd, _ in widths],
        out_shape=[jax.ShapeDtypeStruct((t, wd), dt) for wd, dt in widths],
        compiler_params=_params(("arbitrary",)),
        name="in_proj_odd",
    )(x, g, scale, shift, w, qg, kvg, wuq, wuk, wuv, *tables)


def _axial_angles(seq, dims):
    half = dims // 2
    inv = ROPE_THETA ** (-jnp.arange(0, half, 2, dtype=F32) / half)
    t = jnp.arange(seq)
    ang_r = (t // GRID_W).astype(F32)[:, None] * inv
    ang_c = (t % GRID_W).astype(F32)[:, None] * inv
    return ang_r, ang_c


def _axial_tables_lanes(seq, dims, lane0):
    ang_r, ang_c = _axial_angles(seq, dims)
    cos = jnp.concatenate([jnp.cos(ang_r)] * 2 + [jnp.cos(ang_c)] * 2, axis=-1)
    sin = jnp.concatenate([jnp.sin(ang_r)] * 2 + [jnp.sin(ang_c)] * 2, axis=-1)
    first = (jnp.arange(dims) % (dims // 2)) < dims // 4
    sa = jnp.where(first, -sin, 0.0)
    sb = jnp.where(first, 0.0, sin)
    pad_l, pad_r = lane0, LANES - lane0 - dims
    return (jnp.pad(cos, ((0, 0), (pad_l, pad_r)), constant_values=1.0),
            jnp.pad(sa, ((0, 0), (pad_l, pad_r))), jnp.pad(sb, ((0, 0), (pad_l, pad_r))))


def _axial_tables_rows(seq, dims):
    ang_r, ang_c = _axial_angles(seq, dims)
    cos = jnp.concatenate([jnp.cos(ang_r)] * 2 + [jnp.cos(ang_c)] * 2, axis=-1).T
    sin = jnp.concatenate([-jnp.sin(ang_r), jnp.sin(ang_r), -jnp.sin(ang_c), jnp.sin(ang_c)], axis=-1).T
    return cos, sin


def _rms_rows(x, gain):
    ms = jnp.mean(x * x, axis=0, keepdims=True)
    return x * lax.rsqrt(ms + EPS) * gain


def _rope_rows(x, cos, sin):
    q = x.shape[0] // 4
    partner = jnp.concatenate([x[q:2 * q], x[0:q], x[3 * q:4 * q], x[2 * q:3 * q]], axis=0)
    return x * cos + partner * sin


def _attend_t(streams, k_refs, vt_ref, s_ref, p_ref, tk):
    chunks = []
    off = 0
    for k_ref in k_refs:
        n = k_ref.shape[0]
        assert n % tk == 0
        for c in range(n // tk):
            k_fn = lambda i, k_ref=k_ref, r=c * tk: k_ref[r:r + tk, streams[i][1]].astype(BF16)
            vt_fn = lambda col=off + c * tk: vt_ref[:, col:col + tk]
            chunks.append((k_fn, vt_fn, None))
        off += n
    return _pipeline([q_t for q_t, _ in streams], chunks, s_ref, p_ref)


def _pipeline(q_ts, chunks, s_ref, p_ref):
    nc = len(chunks)
    ns = len(q_ts)
    mq = q_ts[0].shape[1]
    m = [jnp.full((1, mq), -jnp.inf, F32) for _ in range(ns)]
    l = [jnp.zeros((1, mq), F32) for _ in range(ns)]
    acc = [jnp.zeros((LANES, mq), F32) for _ in range(ns)]
    m_chunk = [None] * ns
    alpha = [[None] * ns for _ in range(2)]
    for t in range(nc + 2):
        if t < nc:
            k_fn, _, bias_fn = chunks[t]
            for i, q_t in enumerate(q_ts):
                s = jnp.dot(k_fn(i), q_t, preferred_element_type=F32)
                if bias_fn is not None:
                    s = s + bias_fn(i)
                m_chunk[i] = jnp.max(s, axis=0, keepdims=True)
                s_ref[t % 2, i] = s
        if 0 <= t - 2 < nc:
            vt = chunks[t - 2][1]()
            for i in range(ns):
                pv = jnp.dot(vt, p_ref[t % 2, i], preferred_element_type=F32)
                acc[i] = alpha[t % 2][i] * acc[i] + pv
        if 0 <= t - 1 < nc:
            for i in range(ns):
                m_new = jnp.maximum(m[i], m_prev_chunk[i])
                alpha[(t - 1) % 2][i] = jnp.exp2(m[i] - m_new)
                p = jnp.exp2(s_ref[(t - 1) % 2, i] - m_new)
                l[i] = alpha[(t - 1) % 2][i] * l[i] + jnp.sum(p, axis=0, keepdims=True)
                p_ref[(t - 1) % 2, i] = p.astype(BF16)
                m[i] = m_new
        m_prev_chunk = list(m_chunk)
    return [acc[i] / l[i] for i in range(ns)]


def _fill_vt(vt_ref, v_refs, chunk=512):
    off = 0
    for v_ref in v_refs:
        n = v_ref.shape[0]
        step = min(chunk, n)
        for c in range(n // step):
            blk = v_ref[c * step:(c + 1) * step, :].astype(F32)
            vt_ref[:, off + c * step:off + (c + 1) * step] = blk.T.astype(vt_ref.dtype)
        off += n


def _attn_scratch(n_total, n_streams, tk, mq):
    return [pltpu.VMEM((LANES, n_total), BF16),
            pltpu.VMEM((2, n_streams, tk, mq), F32),
            pltpu.VMEM((2, n_streams, tk, mq), BF16)]


def _gqa_kernel(*refs, n_src, tk, rope, emit_k, q_mul):
    q_ref = refs[0]
    k_refs = [refs[1 + 2 * i] for i in range(n_src)]
    v_refs = [refs[2 + 2 * i] for i in range(n_src)]
    pos = 1 + 2 * n_src
    qg_ref, kg_ref = refs[pos], refs[pos + 1]
    pos += 2
    if rope:
        cosq_ref, sinq_ref, cosk_ref, sink_ref = refs[pos:pos + 4]
        pos += 4
    o_ref = refs[pos]
    pos += 1
    if emit_k:
        kn_ref = refs[pos]
        pos += 1
    kp_ref, vt_ref, s_ref, p_ref = refs[pos:pos + 4]
    tq = q_ref.shape[0]

    @pl.when(pl.program_id(1) == 0)
    def _():
        _fill_vt(vt_ref, v_refs)
        raw = k_refs[-1]
        for c in range(raw.shape[0] // tk):
            rows = slice(c * tk, (c + 1) * tk)
            k_t = raw[rows, :].astype(F32).T
            halves = []
            for hh in range(2):
                x = _rms_rows(k_t[hh * HEAD_DIM:(hh + 1) * HEAD_DIM], kg_ref[...])
                if rope:
                    x = _rope_rows(x, cosk_ref[:, rows], sink_ref[:, rows])
                halves.append(x)
            kn = jnp.concatenate(halves, axis=0).T
            kp_ref[rows, :] = kn.astype(kp_ref.dtype)
            if emit_k:
                kn_ref[rows, :] = kn

    zeros = jnp.zeros((HEAD_DIM, tq), F32)
    parts = []
    for j in range(4):
        q_t = q_ref[:, j * LANES:(j + 1) * LANES].astype(F32).T
        for hh in range(2):
            x = _rms_rows(q_t[hh * HEAD_DIM:(hh + 1) * HEAD_DIM], qg_ref[...])
            if rope:
                x = _rope_rows(x, cosq_ref[...], sinq_ref[...])
            parts.append(x * q_mul)
    streams = []
    for g in range(2):
        cols = [jnp.concatenate([parts[4 * g + n], zeros] if g == 0 else [zeros, parts[4 * g + n]], axis=0)
                for n in range(4)]
        streams.append((jnp.concatenate(cols, axis=1).astype(BF16), slice(None)))
    outs = _attend_t(streams, k_refs[:-1] + [kp_ref], vt_ref, s_ref, p_ref, tk)
    for g in range(2):
        o_t = outs[g][g * HEAD_DIM:(g + 1) * HEAD_DIM]
        for idx, j in enumerate((2 * g, 2 * g + 1)):
            blk_t = jnp.concatenate([o_t[:, (2 * idx) * tq:(2 * idx + 1) * tq],
                                     o_t[:, (2 * idx + 1) * tq:(2 * idx + 2) * tq]], axis=0)
            o_ref[:, j * LANES:(j + 1) * LANES] = blk_t.T.astype(o_ref.dtype)


def _gqa_attention(q, kv, q_gain, k_gain, tables, batch, tq, emit_k, tk=512):
    t, w = q.shape
    s = t // batch
    tq = min(tq, s)
    nq = s // tq
    tk = min([tk] + [k.shape[1] for k, _ in kv])
    in_specs = [pl.BlockSpec((tq, w), lambda b, i: (b * nq + i, 0))]
    args = [q]
    n_total = 0
    for k, v in kv:
        n = k.shape[1]
        n_total += n
        spec = pl.BlockSpec((None, n, LANES), lambda b, i: (b, 0, 0))
        in_specs += [spec, spec]
        args += [k, v]
    n_raw = kv[-1][0].shape[1]
    const = lambda a: pl.BlockSpec(a.shape, lambda b, i: (0, 0))
    qg = jnp.broadcast_to(q_gain[:, None], (HEAD_DIM, tq))
    kg = jnp.broadcast_to(k_gain[:, None], (HEAD_DIM, tk))
    in_specs += [const(qg), const(kg)]
    args += [qg, kg]
    if tables is not None:
        cos, sin = tables
        blk = pl.BlockSpec((HEAD_DIM, tq), lambda b, i: (0, i))
        in_specs += [blk, blk, const(cos), const(sin)]
        args += [cos, sin, cos, sin]
    out_specs = [pl.BlockSpec((tq, w), lambda b, i: (b * nq + i, 0))]
    out_shape = [jax.ShapeDtypeStruct((t, w), BF16)]
    if emit_k:
        out_specs.append(pl.BlockSpec((n_raw, LANES), lambda b, i: (b, 0)))
        out_shape.append(jax.ShapeDtypeStruct((batch * n_raw, LANES), F32))
    res = pl.pallas_call(
        functools.partial(_gqa_kernel, n_src=len(kv), tk=tk, rope=tables is not None, emit_k=emit_k,
                          q_mul=LOG2E * HEAD_DIM ** -0.5),
        grid=(batch, nq),
        in_specs=in_specs,
        out_specs=out_specs,
        out_shape=out_shape,
        scratch_shapes=[pltpu.VMEM((n_raw, LANES), BF16)] + _attn_scratch(n_total, 2, tk, 4 * tq),
        compiler_params=_params(("arbitrary", "arbitrary")),
        name="gqa_attention",
    )(*args)
    return res if emit_k else res[0]


def _pair_kernel(*refs, n_src, tk, dk, rope):
    q_ref = refs[0]
    k_refs = [refs[1 + 2 * i] for i in range(n_src)]
    v_refs = [refs[2 + 2 * i] for i in range(n_src)]
    pos = 1 + 2 * n_src
    if rope:
        cos_ref, sin_ref = refs[pos:pos + 2]
        pos += 2
    o_ref, vt_ref, s_ref, p_ref = refs[pos:pos + 4]
    tq = q_ref.shape[0]

    @pl.when(pl.program_id(2) == 0)
    def _():
        _fill_vt(vt_ref, v_refs)

    top = lax.broadcasted_iota(jnp.int32, (LANES, tq), 0) < HEAD_DIM
    streams = []
    for h in range(2):
        if dk == HEAD_DIM:
            both = q_ref[...].astype(F32).T
            q_t = (jnp.where(top, both, 0.0) if h == 0 else jnp.where(top, 0.0, both)).astype(BF16)
            streams.append((q_t, slice(None)))
        else:
            q_t = q_ref[:, h * LANES:(h + 1) * LANES].astype(F32).T
            if rope:
                n_pe = cos_ref.shape[0]
                pe = _rope_rows(q_t[HEAD_DIM:HEAD_DIM + n_pe], cos_ref[...], sin_ref[...])
                q_t = jnp.concatenate([q_t[:HEAD_DIM], pe, q_t[HEAD_DIM + n_pe:]], axis=0)
            streams.append((q_t.astype(BF16), slice(h * LANES, (h + 1) * LANES)))
    outs = _attend_t(streams, k_refs, vt_ref, s_ref, p_ref, tk)
    o_ref[...] = jnp.where(top, outs[0], outs[1]).T.astype(o_ref.dtype)


def _pair_attention(q, kv, batch, dk, tq, tables=None, tk=512):
    t, w = q.shape
    qw = LANES if dk == HEAD_DIM else 2 * LANES
    pairs = w // qw
    s = t // batch
    tq = min(tq, s)
    nq = s // tq
    tk = min([tk] + [k.shape[1] for k, _ in kv])
    in_specs = [pl.BlockSpec((tq, qw), lambda b, p, i: (b * nq + i, p))]
    args = [q]
    n_total = 0
    for k, v in kv:
        n = k.shape[1]
        n_total += n
        in_specs += [pl.BlockSpec((None, n, qw), lambda b, p, i: (b, 0, p)),
                     pl.BlockSpec((None, n, LANES), lambda b, p, i: (b, 0, p))]
        args += [k, v]
    if tables is not None:
        blk = pl.BlockSpec((tables[0].shape[0], tq), lambda b, p, i: (0, i))
        in_specs += [blk, blk]
        args += list(tables)
    return pl.pallas_call(
        functools.partial(_pair_kernel, n_src=len(kv), tk=tk, dk=dk, rope=tables is not None),
        grid=(batch, pairs, nq),
        in_specs=in_specs,
        out_specs=pl.BlockSpec((tq, LANES), lambda b, p, i: (b * nq + i, p)),
        out_shape=jax.ShapeDtypeStruct((t, pairs * LANES), BF16),
        scratch_shapes=_attn_scratch(n_total, 2, tk, tq),
        compiler_params=_params(("arbitrary", "arbitrary", "arbitrary")),
        name="pair_attention",
    )(*args)


NB_QROWS = 4
NB_KROWS = 12
NB_TK = NB_QROWS * GRID_W
NB_INVALID = 2 * NA_ROWS - 1


def _nb_kernel(q_ref, kc_ref, vc_ref, k_ref, v_ref, tab_ref, o_ref, vtc_ref, vt_ref, bias_ref, s_ref, p_ref):
    i = pl.program_id(2)
    rows = k_ref.shape[0] // GRID_W
    half = NA_ROWS // 2
    a = i * NB_QROWS
    base_blk = jnp.clip(i - half // NB_QROWS, 0, (rows - NB_KROWS) // NB_QROWS)
    base = base_blk * NB_QROWS
    tq = q_ref.shape[0]

    @pl.when(i == 0)
    def _():
        for ref, src in ((vtc_ref, vc_ref), (vt_ref, v_ref)):
            for c in range(ref.shape[0]):
                ref[c] = src[c * NB_TK:(c + 1) * NB_TK, :].astype(F32).T.astype(ref.dtype)

    def unclipped(first_row):
        return jnp.logical_and(first_row - half >= 0, first_row + NB_QROWS - 1 - half <= rows - NA_ROWS)

    @pl.when(jnp.logical_not(jnp.logical_and(unclipped(a), unclipped(a - NB_QROWS))))
    def _():
        lo_tile = lax.broadcasted_iota(jnp.int32, (GRID_W, LANES), 1) < HEAD_DIM
        for h in range(2):
            for j in range(NB_KROWS):
                kr = base + j
                for ip in range(NB_QROWS // 2):
                    halves = []
                    for qi in (2 * ip, 2 * ip + 1):
                        qr = a + qi
                        r0 = jnp.clip(qr - half, 0, rows - NA_ROWS)
                        valid = jnp.logical_and(kr >= r0, kr < r0 + NA_ROWS)
                        halves.append(tab_ref[h, jnp.where(valid, kr - qr + NA_ROWS - 1, NB_INVALID)])
                    bias_ref[h, j * GRID_W:(j + 1) * GRID_W, ip * LANES:(ip + 1) * LANES] = (
                        jnp.where(lo_tile, halves[0], halves[1]))

    top = lax.broadcasted_iota(jnp.int32, (LANES, tq), 0) < HEAD_DIM
    both = q_ref[...].astype(F32).T
    q_ts = [jnp.where(top, both, 0.0).astype(BF16), jnp.where(top, 0.0, both).astype(BF16)]
    chunks = []
    for c in range(vtc_ref.shape[0]):
        chunks.append((lambda i_, c=c: kc_ref[c * NB_TK:(c + 1) * NB_TK, :].astype(BF16),
                       lambda c=c: vtc_ref[c], None))
    for c in range(NB_KROWS // NB_QROWS):
        start = pl.multiple_of((base + c * NB_QROWS) * GRID_W, NB_TK)
        chunks.append((lambda i_, start=start: k_ref[pl.ds(start, NB_TK), :].astype(BF16),
                       lambda c=c: vt_ref[base_blk + c],
                       lambda i_, c=c: bias_ref[i_, c * NB_TK:(c + 1) * NB_TK, :]))
    outs = _pipeline(q_ts, chunks, s_ref, p_ref)
    o_ref[...] = jnp.where(top, outs[0], outs[1]).T.astype(o_ref.dtype)


def _nb_attention(q, k, v, kctx, vctx, table, batch):
    t, w = q.shape
    s = t // batch
    pairs = w // LANES
    tq = NB_TK
    nq = s // tq
    l = kctx.shape[1]
    assert l % NB_TK == 0 and (s // GRID_W) % NB_QROWS == 0
    return pl.pallas_call(
        _nb_kernel,
        grid=(batch, pairs, nq),
        in_specs=[
            pl.BlockSpec((tq, LANES), lambda b, p, i: (b * nq + i, p)),
            pl.BlockSpec((None, l, LANES), lambda b, p, i: (b, 0, p)),
            pl.BlockSpec((None, l, LANES), lambda b, p, i: (b, 0, p)),
            pl.BlockSpec((None, s, LANES), lambda b, p, i: (b, 0, p)),
            pl.BlockSpec((None, s, LANES), lambda b, p, i: (b, 0, p)),
            pl.BlockSpec((2, NB_INVALID + 1, GRID_W, LANES), lambda b, p, i: (p, 0, 0, 0)),
        ],
        out_specs=pl.BlockSpec((tq, LANES), lambda b, p, i: (b * nq + i, p)),
        out_shape=jax.ShapeDtypeStruct((t, w), BF16),
        scratch_shapes=[pltpu.VMEM((l // NB_TK, LANES, NB_TK), BF16),
                        pltpu.VMEM((s // NB_TK, LANES, NB_TK), BF16),
                        pltpu.VMEM((2, NB_KROWS * GRID_W, tq), F32),
                        pltpu.VMEM((2, 2, NB_TK, tq), F32),
                        pltpu.VMEM((2, 2, NB_TK, tq), BF16)],
        compiler_params=_params(("arbitrary", "arbitrary", "arbitrary")),
        name="nb_attention",
    )(q, kctx, vctx, k, v, table)


def _nb_bias_table(rpb):
    h = rpb.shape[0]
    qc = jnp.arange(GRID_W)[None, :]
    kc = jnp.arange(GRID_W)[:, None]
    c0 = jnp.clip(qc - NA_COLS // 2, 0, GRID_W - NA_COLS)
    inwin = jnp.logical_and(kc >= c0, kc < c0 + NA_COLS)
    dc = jnp.clip(kc - qc + NA_COLS - 1, 0, 2 * NA_COLS - 2)
    t = jnp.where(inwin[None, None], rpb[:, :, dc].astype(F32) * LOG2E, NEG)
    t = jnp.concatenate([t, jnp.full((h, 1, GRID_W, GRID_W), NEG, F32)], axis=1)
    return jnp.concatenate([t, t], axis=-1)


PAD = 8


def _pool_kernel(u_ref, map_ref, sc_ref, o_ref, pad_ref):
    s = u_ref.shape[0]
    g = pl.program_id(1)
    u = u_ref[...]
    pad_ref[0:PAD, :] = jnp.zeros((PAD, LANES), F32)
    pad_ref[PAD + s:PAD + s + PAD, :] = jnp.zeros((PAD, LANES), F32)
    pad_ref[PAD:PAD + s, :] = u
    t = lax.broadcasted_iota(jnp.int32, (s, LANES), 0)
    for gi, win in enumerate(POOL_WINDOWS):
        @pl.when(g == gi)
        def _():
            acc = jnp.zeros((s, LANES), F32)
            for d in range(-(win // 2), win - win // 2):
                acc = acc + pad_ref[PAD + d:PAD + d + s, :]
            cnt = jnp.minimum(t + (win - win // 2), s) - jnp.maximum(t - win // 2, 0)
            diff = acc / cnt.astype(F32) - u
            y = jnp.dot(diff.astype(BF16), map_ref[...].astype(BF16), preferred_element_type=F32)
            o_ref[...] = (y * sc_ref[...]).astype(o_ref.dtype)


def _pool(u, b_map, b_scale, batch):
    t, w = u.shape
    s = t // batch
    groups = w // LANES
    return pl.pallas_call(
        _pool_kernel,
        grid=(batch, groups),
        in_specs=[
            pl.BlockSpec((s, LANES), lambda b, g: (b, g)),
            pl.BlockSpec((None, LANES, LANES), lambda b, g: (g, 0, 0)),
            pl.BlockSpec((1, LANES), lambda b, g: (0, g)),
        ],
        out_specs=pl.BlockSpec((s, LANES), lambda b, g: (b, g)),
        out_shape=jax.ShapeDtypeStruct((t, w), BF16),
        scratch_shapes=[pltpu.VMEM((s + 2 * PAD, LANES), F32)],
        compiler_params=_params(("arbitrary", "arbitrary")),
        name="pool",
    )(u, b_map, b_scale.reshape(1, w))


def _out_kernel(x_ref, a_ref, ga_ref, b_ref, gb_ref, w_ref, g_ref, gate_ref, o_ref):
    half = a_ref.shape[1]
    ta = (a_ref[...].astype(F32) * _silu(ga_ref[...].astype(F32))).astype(BF16)
    tb = (b_ref[...].astype(F32) * _silu(gb_ref[...].astype(F32))).astype(BF16)
    y = (jnp.dot(ta, w_ref[0:half, :], preferred_element_type=F32)
         + jnp.dot(tb, w_ref[half:2 * half, :], preferred_element_type=F32))
    o_ref[...] = x_ref[...] + gate_ref[...] * _rms_lanes(y, g_ref[...])


def _out_proj(x, a, ga, b, gb, w, g_post, gate, rows_per_mod, tm=512):
    t, d = x.shape
    half = a.shape[1]
    tm = min(tm, t)
    br = pl.BlockSpec((tm, half), lambda i: (i, 0))
    return pl.pallas_call(
        _out_kernel,
        grid=(t // tm,),
        in_specs=[
            pl.BlockSpec((tm, d), lambda i: (i, 0)),
            br, br, br, br,
            pl.BlockSpec((2 * half, d), lambda i: (0, 0)),
            pl.BlockSpec((1, d), lambda i: (0, 0)),
            pl.BlockSpec((None, 1, d), lambda i: ((i * tm) // rows_per_mod, 0, 0)),
        ],
        out_specs=pl.BlockSpec((tm, d), lambda i: (i, 0)),
        out_shape=jax.ShapeDtypeStruct((t, d), F32),
        compiler_params=_params(("arbitrary",)),
        name="out_proj",
    )(x, a, ga, b, gb, w, g_post, gate)


def _kvup_kernel(c_ref, pe_ref, wk_ref, wv_ref, k_ref, v_ref):
    cb = c_ref[...].astype(BF16)
    heads = k_ref.shape[1] // LANES
    k = jnp.dot(cb, wk_ref[...], preferred_element_type=F32)
    k = k + jnp.concatenate([pe_ref[...]] * heads, axis=1)
    k_ref[...] = k.astype(k_ref.dtype)
    v_ref[...] = jnp.dot(cb, wv_ref[...], preferred_element_type=F32).astype(v_ref.dtype)


def _kvup(ckv, pe128, wk, wv, tm=512):
    t, r = ckv.shape
    tm = min(tm, t)
    nk, nv = wk.shape[1], wv.shape[1]
    row = lambda width: pl.BlockSpec((tm, width), lambda i: (i, 0))
    full = lambda a: pl.BlockSpec(a.shape, lambda i: (0, 0))
    return pl.pallas_call(
        _kvup_kernel,
        grid=(t // tm,),
        in_specs=[row(r), row(LANES), full(wk), full(wv)],
        out_specs=[row(nk), row(nv)],
        out_shape=[jax.ShapeDtypeStruct((t, nk), BF16), jax.ShapeDtypeStruct((t, nv), BF16)],
        compiler_params=_params(("arbitrary",)),
        name="kv_up",
    )(ckv, pe128, wk, wv)


def kernel(x_prompt, x_sample, cache_a_k, cache_a_v, cache_c_k, cache_c_v, cache_d_ckv, cache_d_kpe,
           c, c_ctx, w_mod, b_mod, g_pre, g_post, w_in_e, a_q_norm, a_k_norm, b_map, b_scale, w_out_e,
           w_in_o, c_rpb, d_q_norm, d_w_uq, d_kv_norm, d_w_ukv, w_out_o):
    bp, sp, d = x_prompt.shape
    bs, ss, _ = x_sample.shape
    past = cache_a_k.shape[2]
    depth = w_mod.shape[0]
    tp, ts = bp * sp, bs * ss
    branch = d // 2
    d_heads = 8
    d_nope, d_rope, d_v = 64, 32, 64
    dqk = d_nope + d_rope
    q_lora, kv_lora = d_q_norm.shape[1], d_kv_norm.shape[1]
    kv_w = a_k_norm.shape[1] * 2

    xp = x_prompt.reshape(tp, d)
    xs = x_sample.reshape(ts, d)

    cond = jnp.concatenate([c_ctx[None, :], c, jnp.zeros((8 - 1 - bs, d), F32)], axis=0)
    mods = _mod_call(cond, w_mod, b_mod)

    def mod_rows(layer, part, lo_row, n_rows):
        return mods[layer, lo_row:lo_row + n_rows, part * d:(part + 1) * d].reshape(n_rows, 1, d)

    rope_a = _axial_tables_rows(ss, HEAD_DIM)
    rope_d = _axial_tables_rows(ss, d_rope)
    rope_d_lanes = _axial_tables_lanes(ss, d_rope, d_nope)
    no_rope_p = tuple(jnp.zeros((sp, LANES), F32) for _ in range(3))

    outs = {name: [] for name in ("a_k", "a_v", "c_k", "c_v", "d_ckv", "d_kpe")}
    for layer in range(depth):
        j = layer // 2
        g_in = g_pre[layer].reshape(1, d)
        g_out = g_post[layer].reshape(1, d)
        paths = (("p", xp, bp, sp, 0, 1, tp), ("s", xs, bs, ss, 1, bs, ss))
        if layer % 2 == 0:
            w_in = w_in_e[j].astype(BF16)
            w_out = w_out_e[j].astype(BF16)
            o0 = branch
            new_x = []
            for name, x, nb, seq, row0, nrows, rpm in paths:
                shift, scale, gate = (mod_rows(layer, part, row0, nrows) for part in range(3))
                v_dt = F32 if name == "p" else BF16
                segs = [(0, branch, F32), (o0, kv_w, F32), (o0 + kv_w, kv_w, v_dt),
                        (o0 + 2 * kv_w, branch, BF16), (o0 + 2 * kv_w + branch, branch, F32),
                        (o0 + 2 * kv_w + 2 * branch, branch, BF16)]
                q, k, v, ga, ub, gb = _norm_mm(x, g_in, scale, shift, w_in, segs, rpm, name="in_proj_even")
                k3, v3 = k.reshape(nb, seq, kv_w), v.reshape(nb, seq, kv_w)
                if name == "p":
                    o_a, kn = _gqa_attention(q, [(k3, v3)], a_q_norm[j], a_k_norm[j], None, nb, tq=256,
                                             emit_k=True)
                    outs["a_k"].append(kn)
                    outs["a_v"].append(v)
                else:
                    kv = [(cache_a_k[:, j].reshape(nb, past, kv_w), cache_a_v[:, j].reshape(nb, past, kv_w)),
                          (k3, v3)]
                    o_a = _gqa_attention(q, kv, a_q_norm[j], a_k_norm[j], rope_a, nb, tq=128, emit_k=False)
                o_b = _pool(ub, b_map[j], b_scale[j], nb)
                new_x.append(_out_proj(x, o_a, ga, o_b, gb, w_out, g_out, gate, rpm))
            xp, xs = new_x
        else:
            wi = w_in_o[j]
            off_kpe = 4 * branch + q_lora + kv_lora
            w_in = jnp.concatenate([wi[:, :off_kpe], jnp.zeros((d, d_nope), F32),
                                    wi[:, off_kpe:off_kpe + d_rope],
                                    jnp.zeros((d, LANES - dqk), F32),
                                    wi[:, off_kpe + d_rope:]], axis=1).astype(BF16)
            w_out = w_out_o[j].astype(BF16)
            w_uq = jnp.pad(d_w_uq[j].reshape(q_lora, d_heads, dqk),
                           ((0, 0), (0, 0), (0, LANES - dqk))).reshape(q_lora, d_heads * LANES).astype(BF16)
            ukv = d_w_ukv[j].reshape(kv_lora, d_heads, d_nope + d_v)
            w_uk = jnp.pad(ukv[:, :, :d_nope], ((0, 0), (0, 0), (0, LANES - d_nope))
                           ).reshape(kv_lora, d_heads * LANES).astype(BF16)
            w_uv = ukv[:, :, d_nope:].reshape(kv_lora, d_heads * d_v).astype(BF16)
            qn_g = d_q_norm[j].reshape(1, q_lora)
            kvn_g = d_kv_norm[j].reshape(1, kv_lora)
            table = _nb_bias_table(c_rpb[j])
            new_x = []
            for name, x, nb, seq, row0, nrows, rpm in paths:
                shift, scale, gate = (mod_rows(layer, part, row0, nrows) for part in range(3))
                latent = name == "s"
                qc, kc, vc, gc, gd, qd, ckv_n, kpe, k_d, v_d = _in_odd(
                    x, g_in, scale, shift, w_in, qn_g, kvn_g, w_uq, w_uk, w_uv,
                    rope_d_lanes if latent else no_rope_p, rpm, seq, d_rope // 4 if latent else 0,
                    BF16 if latent else F32, F32 if latent else BF16, branch,
                    LOG2E * HEAD_DIM ** -0.5, LOG2E * dqk ** -0.5)
                kc3, vc3 = kc.reshape(nb, seq, branch), vc.reshape(nb, seq, branch)
                kd3, vd3 = k_d.reshape(nb, seq, -1), v_d.reshape(nb, seq, -1)
                if not latent:
                    outs["c_k"].append(kc)
                    outs["c_v"].append(vc)
                    outs["d_ckv"].append(ckv_n)
                    outs["d_kpe"].append(kpe[:, d_nope:dqk])
                    o_c = _pair_attention(qc, [(kc3, vc3)], nb, HEAD_DIM, tq=256)
                    o_d = _pair_attention(qd, [(kd3, vd3)], nb, 2 * HEAD_DIM, tq=256)
                else:
                    o_c = _nb_attention(qc, kc3, vc3, cache_c_k[:, j].reshape(nb, past, branch),
                                        cache_c_v[:, j].reshape(nb, past, branch), table, nb)
                    pe_ctx = jnp.pad(cache_d_kpe[:, j].reshape(nb * past, d_rope),
                                     ((0, 0), (d_nope, LANES - dqk)))
                    k_ctx, v_ctx = _kvup(cache_d_ckv[:, j].reshape(nb * past, kv_lora), pe_ctx, w_uk, w_uv)
                    kv = [(k_ctx.reshape(nb, past, -1), v_ctx.reshape(nb, past, -1)), (kd3, vd3)]
                    o_d = _pair_attention(qd, kv, nb, 2 * HEAD_DIM, tq=256, tables=rope_d)
                new_x.append(_out_proj(x, o_c, gc, o_d, gd, w_out, g_out, gate, rpm))
            xp, xs = new_x

    kvh = kv_w // HEAD_DIM
    ch = branch // HEAD_DIM

    def stacked(name, *tail):
        return jnp.stack([a.reshape(bp, sp, *tail) for a in outs[name]], axis=1)

    return (xp.reshape(bp, sp, d), xs.reshape(bs, ss, d),
            stacked("a_k", kvh, HEAD_DIM), stacked("a_v", kvh, HEAD_DIM),
            stacked("c_k", ch, HEAD_DIM), stacked("c_v", ch, HEAD_DIM),
            stacked("d_ckv", kv_lora), stacked("d_kpe", d_rope))
```

```python
import functools

import jax
import jax.numpy as jnp
from jax import lax
from jax.experimental import pallas as pl
from jax.experimental.pallas import tpu as pltpu

F32 = jnp.float32
BF16 = jnp.bfloat16

LANES = 128
GRID_W = 64
HEAD_DIM = 64
NA_ROWS = 8
NA_COLS = 16
POOL_WINDOWS = (2, 4, 8, 16)
ROPE_THETA = 10000.0
EPS = 1e-6
NEG = -1e30
LOG2E = 1.4426950408889634
VMEM_LIMIT = 56 * 1024 * 1024


def _params(sem):
    return pltpu.CompilerParams(dimension_semantics=sem, vmem_limit_bytes=VMEM_LIMIT)


def _silu(x):
    return x / (1.0 + jnp.exp(-x))


def _rms_lanes(x, gain):
    ms = jnp.mean(x * x, axis=-1, keepdims=True)
    return x * lax.rsqrt(ms + EPS) * gain


def _mod_kernel(c_ref, w_ref, b_ref, o_ref):
    s = _silu(c_ref[...]).astype(BF16)
    o_ref[...] = jnp.dot(s, w_ref[...].astype(BF16), preferred_element_type=F32) + b_ref[...]


def _mod_call(cond, w_mod, b_mod):
    depth, d, n = w_mod.shape
    rows = cond.shape[0]
    tn = 1024
    return pl.pallas_call(
        _mod_kernel,
        grid=(depth, n // tn),
        in_specs=[
            pl.BlockSpec((rows, d), lambda l, j: (0, 0)),
            pl.BlockSpec((None, d, tn), lambda l, j: (l, 0, j)),
            pl.BlockSpec((None, 1, tn), lambda l, j: (l, 0, j)),
        ],
        out_specs=pl.BlockSpec((None, rows, tn), lambda l, j: (l, 0, j)),
        out_shape=jax.ShapeDtypeStruct((depth, rows, n), F32),
        compiler_params=_params(("arbitrary", "arbitrary")),
        name="adaln_mod",
    )(cond, w_mod, b_mod.reshape(depth, 1, n))


def _norm_mm_kernel(x_ref, g_ref, sc_ref, sh_ref, w_ref, *o_refs, segs, do_norm):
    x = x_ref[...].astype(F32)
    if do_norm:
        x = _rms_lanes(x, g_ref[...]) * (1.0 + sc_ref[...]) + sh_ref[...]
    z = jnp.dot(x.astype(BF16), w_ref[...], preferred_element_type=F32)
    for o_ref, (start, width) in zip(o_refs, segs):
        o_ref[...] = z[:, start:start + width].astype(o_ref.dtype)


def _norm_mm(x, g, scale, shift, w, segs, rows_per_mod, do_norm=True, tm=512, name="norm_mm"):
    t, k = x.shape
    n = w.shape[1]
    tm = min(tm, t)
    kern = functools.partial(_norm_mm_kernel, segs=tuple((s, wd) for s, wd, _ in segs), do_norm=do_norm)
    mod_map = lambda i: ((i * tm) // rows_per_mod, 0, 0)
    return pl.pallas_call(
        kern,
        grid=(t // tm,),
        in_specs=[
            pl.BlockSpec((tm, k), lambda i: (i, 0)),
            pl.BlockSpec((1, k), lambda i: (0, 0)),
            pl.BlockSpec((None, 1, k), mod_map),
            pl.BlockSpec((None, 1, k), mod_map),
            pl.BlockSpec((k, n), lambda i: (0, 0)),
        ],
        out_specs=[pl.BlockSpec((tm, wd), lambda i: (i, 0)) for _, wd, _ in segs],
        out_shape=[jax.ShapeDtypeStruct((t, wd), dt) for _, wd, dt in segs],
        compiler_params=_params(("arbitrary",)),
        name=name,
    )(x, g, scale, shift, w)


def _in_odd_kernel(x_ref, g_ref, sc_ref, sh_ref, w_ref, qg_ref, kvg_ref, wuq_ref, wuk_ref, wuv_ref,
                   cos_ref, sa_ref, sb_ref,
                   qc_ref, kc_ref, vc_ref, gc_ref, gd_ref, qd_ref, ckv_ref, kpe_ref, kd_ref, vd_ref,
                   *, branch, q_lora, kv_lora, rope_shift, qc_mul, qd_mul):
    x = _rms_lanes(x_ref[...], g_ref[...]) * (1.0 + sc_ref[...]) + sh_ref[...]
    z = jnp.dot(x.astype(BF16), w_ref[...], preferred_element_type=F32)
    b = branch
    qc_ref[...] = (z[:, 0:b] * qc_mul).astype(qc_ref.dtype)
    kc_ref[...] = z[:, b:2 * b].astype(kc_ref.dtype)
    vc_ref[...] = z[:, 2 * b:3 * b].astype(vc_ref.dtype)
    gc_ref[...] = z[:, 3 * b:4 * b].astype(gc_ref.dtype)
    o = 4 * b
    cq = _rms_lanes(z[:, o:o + q_lora], qg_ref[...])
    qd = jnp.dot(cq.astype(BF16), wuq_ref[...], preferred_element_type=F32)
    qd_ref[...] = (qd * qd_mul).astype(qd_ref.dtype)
    o += q_lora
    ckv = _rms_lanes(z[:, o:o + kv_lora], kvg_ref[...])
    ckv_ref[...] = ckv
    o += kv_lora
    kpe = z[:, o:o + LANES]
    if rope_shift:
        kpe = (kpe * cos_ref[...] + pltpu.roll(kpe, LANES - rope_shift, 1) * sa_ref[...]
               + pltpu.roll(kpe, rope_shift, 1) * sb_ref[...])
    kpe_ref[...] = kpe
    o += LANES
    gd_ref[...] = z[:, o:o + b].astype(gd_ref.dtype)
    cb = ckv.astype(BF16)
    heads = kd_ref.shape[1] // LANES
    kd = jnp.dot(cb, wuk_ref[...], preferred_element_type=F32) + jnp.concatenate([kpe] * heads, axis=1)
    kd_ref[...] = kd.astype(kd_ref.dtype)
    vd_ref[...] = jnp.dot(cb, wuv_ref[...], preferred_element_type=F32).astype(vd_ref.dtype)


def _in_odd(x, g, scale, shift, w, qg, kvg, wuq, wuk, wuv, tables, rows_per_mod, seq, rope_shift,
            kv_dtype, qd_dtype, branch, qc_mul, qd_mul, tm=512):
    t, k = x.shape
    tm = min(tm, seq)
    nseq = seq // tm
    q_lora, kv_lora = qg.shape[1], kvg.shape[1]
    widths = [(branch, BF16), (branch, kv_dtype), (branch, kv_dtype), (branch, BF16), (branch, BF16),
              (wuq.shape[1], qd_dtype), (kv_lora, F32), (LANES, F32), (wuk.shape[1], BF16), (wuv.shape[1], BF16)]
    mod_map = lambda i: ((i * tm) // rows_per_mod, 0, 0)
    full = lambda a: pl.BlockSpec(a.shape, lambda i: (0, 0))
    tab_spec = pl.BlockSpec((tm, LANES), lambda i: (i % nseq, 0))
    kern = functools.partial(_in_odd_kernel, branch=branch, q_lora=q_lora, kv_lora=kv_lora,
                             rope_shift=rope_shift, qc_mul=qc_mul, qd_mul=qd_mul)
    return pl.pallas_call(
        kern,
        grid=(t // tm,),
        in_specs=[pl.BlockSpec((tm, k), lambda i: (i, 0)), full(g),
                  pl.BlockSpec((None, 1, k), mod_map), pl.BlockSpec((None, 1, k), mod_map),
                  full(w), full(qg), full(kvg), full(wuq), full(wuk), full(wuv),
                  tab_spec, tab_spec, tab_spec],
        out_specs=[pl.BlockSpec((tm, wd), lambda i: (i, 0)) for wd, _ in widths],
        out_shape=[jax.ShapeDtypeStruct((t, wd), dt) for wd, dt in widths],
        compiler_params=_params(("arbitrary",)),
        name="in_proj_odd",
    )(x, g, scale, shift, w, qg, kvg, wuq, wuk, wuv, *tables)


def _axial_angles(seq, dims):
    half = dims // 2
    inv = ROPE_THETA ** (-jnp.arange(0, half, 2, dtype=F32) / half)
    t = jnp.arange(seq)
    ang_r = (t // GRID_W).astype(F32)[:, None] * inv
    ang_c = (t % GRID_W).astype(F32)[:, None] * inv
    return ang_r, ang_c


def _axial_tables_lanes(seq, dims, lane0):
    ang_r, ang_c = _axial_angles(seq, dims)
    cos = jnp.concatenate([jnp.cos(ang_r)] * 2 + [jnp.cos(ang_c)] * 2, axis=-1)
    sin = jnp.concatenate([jnp.sin(ang_r)] * 2 + [jnp.sin(ang_c)] * 2, axis=-1)
    first = (jnp.arange(dims) % (dims // 2)) < dims // 4
    sa = jnp.where(first, -sin, 0.0)
    sb = jnp.where(first, 0.0, sin)
    pad_l, pad_r = lane0, LANES - lane0 - dims
    return (jnp.pad(cos, ((0, 0), (pad_l, pad_r)), constant_values=1.0),
            jnp.pad(sa, ((0, 0), (pad_l, pad_r))), jnp.pad(sb, ((0, 0), (pad_l, pad_r))))


def _axial_tables_rows(seq, dims):
    ang_r, ang_c = _axial_angles(seq, dims)
    cos = jnp.concatenate([jnp.cos(ang_r)] * 2 + [jnp.cos(ang_c)] * 2, axis=-1).T
    sin = jnp.concatenate([-jnp.sin(ang_r), jnp.sin(ang_r), -jnp.sin(ang_c), jnp.sin(ang_c)], axis=-1).T
    return cos, sin


def _rms_rows(x, gain):
    ms = jnp.mean(x * x, axis=0, keepdims=True)
    return x * lax.rsqrt(ms + EPS) * gain


def _rope_rows(x, cos, sin):
    q = x.shape[0] // 4
    partner = jnp.concatenate([x[q:2 * q], x[0:q], x[3 * q:4 * q], x[2 * q:3 * q]], axis=0)
    return x * cos + partner * sin


def _attend_t(streams, k_refs, vt_ref, s_ref, p_ref, tk):
    chunks = []
    off = 0
    for k_ref in k_refs:
        n = k_ref.shape[0]
        assert n % tk == 0
        for c in range(n // tk):
            k_fn = lambda i, k_ref=k_ref, r=c * tk: k_ref[r:r + tk, streams[i][1]].astype(BF16)
            vt_fn = lambda i, col=off + c * tk: vt_ref[i, :, col:col + tk]
            chunks.append((k_fn, vt_fn, None))
        off += n
    return _pipeline([q_t for q_t, _ in streams], chunks, s_ref, p_ref)


def _pipeline(q_ts, chunks, s_ref, p_ref):
    nc = len(chunks)
    ns = len(q_ts)
    mq = q_ts[0].shape[1]
    m = [jnp.full((1, mq), -jnp.inf, F32) for _ in range(ns)]
    acc = [jnp.zeros((VT_ROWS, mq), F32) for _ in range(ns)]
    m_chunk = [None] * ns
    alpha = [[None] * ns for _ in range(2)]
    for t in range(nc + 2):
        if t < nc:
            k_fn, _, bias_fn = chunks[t]
            for i, q_t in enumerate(q_ts):
                s = jnp.dot(k_fn(i), q_t, preferred_element_type=F32)
                if bias_fn is not None:
                    s = s + bias_fn(i)
                m_chunk[i] = jnp.max(s, axis=0, keepdims=True)
                s_ref[(t % 2) * ns + i] = s
        if 0 <= t - 2 < nc:
            for i in range(ns):
                pv = jnp.dot(chunks[t - 2][1](i), p_ref[(t % 2) * ns + i], preferred_element_type=F32)
                acc[i] = alpha[t % 2][i] * acc[i] + pv
        if 0 <= t - 1 < nc:
            for i in range(ns):
                m_new = jnp.maximum(m[i], m_prev_chunk[i])
                alpha[(t - 1) % 2][i] = jnp.exp2(m[i] - m_new)
                p = jnp.exp2(s_ref[((t - 1) % 2) * ns + i] - m_new)
                p_ref[((t - 1) % 2) * ns + i] = p.astype(BF16)
                m[i] = m_new
        m_prev_chunk = list(m_chunk)
    return [acc[i][:HEAD_DIM] / acc[i][HEAD_DIM:HEAD_DIM + 1] for i in range(ns)]


VT_ROWS = HEAD_DIM + 16


def _vt_tiles(blk):
    n = blk.shape[0]
    v_t = blk.astype(F32).T
    tail = (lax.broadcasted_iota(jnp.int32, (VT_ROWS - HEAD_DIM, n), 0) == 0).astype(F32)
    return [jnp.concatenate([v_t[h * HEAD_DIM:(h + 1) * HEAD_DIM], tail], axis=0).astype(BF16)
            for h in range(2)]


def _fill_vt(vt_ref, v_refs, chunk=512):
    off = 0
    for v_ref in v_refs:
        n = v_ref.shape[0]
        step = min(chunk, n)
        for c in range(n // step):
            tiles = _vt_tiles(v_ref[c * step:(c + 1) * step, :])
            for h in range(2):
                vt_ref[h, :, off + c * step:off + (c + 1) * step] = tiles[h]
        off += n


def _pipe_scratch(n_streams, tk, mq):
    return [pltpu.VMEM((2 * n_streams, tk, mq), F32), pltpu.VMEM((2 * n_streams, tk, mq), BF16)]


def _attn_scratch(n_total, n_streams, tk, mq):
    return [pltpu.VMEM((2, VT_ROWS, n_total), BF16)] + _pipe_scratch(n_streams, tk, mq)


def _gqa_kernel(*refs, n_src, tk, rope, emit_k, q_mul):
    q_ref = refs[0]
    k_refs = [refs[1 + 2 * i] for i in range(n_src)]
    v_refs = [refs[2 + 2 * i] for i in range(n_src)]
    pos = 1 + 2 * n_src
    qg_ref, kg_ref = refs[pos], refs[pos + 1]
    pos += 2
    if rope:
        cosq_ref, sinq_ref, cosk_ref, sink_ref = refs[pos:pos + 4]
        pos += 4
    o_ref = refs[pos]
    pos += 1
    if emit_k:
        kn_ref = refs[pos]
        pos += 1
    kp_ref, vt_ref, s_ref, p_ref = refs[pos:pos + 4]
    tq = q_ref.shape[0]

    @pl.when(pl.program_id(1) == 0)
    def _():
        _fill_vt(vt_ref, v_refs)
        raw = k_refs[-1]
        for c in range(raw.shape[0] // tk):
            rows = slice(c * tk, (c + 1) * tk)
            k_t = raw[rows, :].astype(F32).T
            halves = []
            for hh in range(2):
                x = _rms_rows(k_t[hh * HEAD_DIM:(hh + 1) * HEAD_DIM], kg_ref[...])
                if rope:
                    x = _rope_rows(x, cosk_ref[:, rows], sink_ref[:, rows])
                halves.append(x)
            kn = jnp.concatenate(halves, axis=0).T
            kp_ref[rows, :] = kn.astype(kp_ref.dtype)
            if emit_k:
                kn_ref[rows, :] = kn

    zeros = jnp.zeros((HEAD_DIM, tq), F32)
    parts = []
    for j in range(4):
        q_t = q_ref[:, j * LANES:(j + 1) * LANES].astype(F32).T
        for hh in range(2):
            x = _rms_rows(q_t[hh * HEAD_DIM:(hh + 1) * HEAD_DIM], qg_ref[...])
            if rope:
                x = _rope_rows(x, cosq_ref[...], sinq_ref[...])
            parts.append(x * q_mul)
    streams = []
    for g in range(2):
        cols = [jnp.concatenate([parts[4 * g + n], zeros] if g == 0 else [zeros, parts[4 * g + n]], axis=0)
                for n in range(4)]
        streams.append((jnp.concatenate(cols, axis=1).astype(BF16), slice(None)))
    outs = _attend_t(streams, k_refs[:-1] + [kp_ref], vt_ref, s_ref, p_ref, tk)
    for g in range(2):
        o_t = outs[g]
        for idx, j in enumerate((2 * g, 2 * g + 1)):
            blk_t = jnp.concatenate([o_t[:, (2 * idx) * tq:(2 * idx + 1) * tq],
                                     o_t[:, (2 * idx + 1) * tq:(2 * idx + 2) * tq]], axis=0)
            o_ref[:, j * LANES:(j + 1) * LANES] = blk_t.T.astype(o_ref.dtype)


def _gqa_attention(q, kv, q_gain, k_gain, tables, batch, tq, emit_k, tk=256):
    t, w = q.shape
    s = t // batch
    tq = min(tq, s)
    nq = s // tq
    tk = min([tk] + [k.shape[1] for k, _ in kv])
    in_specs = [pl.BlockSpec((tq, w), lambda b, i: (b * nq + i, 0))]
    args = [q]
    n_total = 0
    for k, v in kv:
        n = k.shape[1]
        n_total += n
        spec = pl.BlockSpec((None, n, LANES), lambda b, i: (b, 0, 0))
        in_specs += [spec, spec]
        args += [k, v]
    n_raw = kv[-1][0].shape[1]
    const = lambda a: pl.BlockSpec(a.shape, lambda b, i: (0, 0))
    qg = jnp.broadcast_to(q_gain[:, None], (HEAD_DIM, tq))
    kg = jnp.broadcast_to(k_gain[:, None], (HEAD_DIM, tk))
    in_specs += [const(qg), const(kg)]
    args += [qg, kg]
    if tables is not None:
        cos, sin = tables
        blk = pl.BlockSpec((HEAD_DIM, tq), lambda b, i: (0, i))
        in_specs += [blk, blk, const(cos), const(sin)]
        args += [cos, sin, cos, sin]
    out_specs = [pl.BlockSpec((tq, w), lambda b, i: (b * nq + i, 0))]
    out_shape = [jax.ShapeDtypeStruct((t, w), BF16)]
    if emit_k:
        out_specs.append(pl.BlockSpec((n_raw, LANES), lambda b, i: (b, 0)))
        out_shape.append(jax.ShapeDtypeStruct((batch * n_raw, LANES), F32))
    res = pl.pallas_call(
        functools.partial(_gqa_kernel, n_src=len(kv), tk=tk, rope=tables is not None, emit_k=emit_k,
                          q_mul=LOG2E * HEAD_DIM ** -0.5),
        grid=(batch, nq),
        in_specs=in_specs,
        out_specs=out_specs,
        out_shape=out_shape,
        scratch_shapes=[pltpu.VMEM((n_raw, LANES), BF16)] + _attn_scratch(n_total, 2, tk, 4 * tq),
        compiler_params=_params(("arbitrary", "arbitrary")),
        name="gqa_attention",
    )(*args)
    return res if emit_k else res[0]


def _pair_kernel(*refs, n_src, tk, dk, rope):
    q_ref = refs[0]
    k_refs = [refs[1 + 2 * i] for i in range(n_src)]
    v_refs = [refs[2 + 2 * i] for i in range(n_src)]
    pos = 1 + 2 * n_src
    if rope:
        cos_ref, sin_ref = refs[pos:pos + 2]
        pos += 2
    o_ref, vt_ref, s_ref, p_ref = refs[pos:pos + 4]
    tq = q_ref.shape[0]

    @pl.when(pl.program_id(2) == 0)
    def _():
        _fill_vt(vt_ref, v_refs)

    top = lax.broadcasted_iota(jnp.int32, (LANES, tq), 0) < HEAD_DIM
    streams = []
    for h in range(2):
        if dk == HEAD_DIM:
            both = q_ref[...].astype(F32).T
            q_t = (jnp.where(top, both, 0.0) if h == 0 else jnp.where(top, 0.0, both)).astype(BF16)
            streams.append((q_t, slice(None)))
        else:
            q_t = q_ref[:, h * LANES:(h + 1) * LANES].astype(F32).T
            if rope:
                n_pe = cos_ref.shape[0]
                pe = _rope_rows(q_t[HEAD_DIM:HEAD_DIM + n_pe], cos_ref[...], sin_ref[...])
                q_t = jnp.concatenate([q_t[:HEAD_DIM], pe, q_t[HEAD_DIM + n_pe:]], axis=0)
            streams.append((q_t.astype(BF16), slice(h * LANES, (h + 1) * LANES)))
    outs = _attend_t(streams, k_refs, vt_ref, s_ref, p_ref, tk)
    o_ref[...] = jnp.concatenate(outs, axis=0).T.astype(o_ref.dtype)


def _pair_attention(q, kv, batch, dk, tq, tables=None, tk=512):
    t, w = q.shape
    qw = LANES if dk == HEAD_DIM else 2 * LANES
    pairs = w // qw
    s = t // batch
    tq = min(tq, s)
    nq = s // tq
    tk = min([tk] + [k.shape[1] for k, _ in kv])
    in_specs = [pl.BlockSpec((tq, qw), lambda b, p, i: (b * nq + i, p))]
    args = [q]
    n_total = 0
    for k, v in kv:
        n = k.shape[1]
        n_total += n
        in_specs += [pl.BlockSpec((None, n, qw), lambda b, p, i: (b, 0, p)),
                     pl.BlockSpec((None, n, LANES), lambda b, p, i: (b, 0, p))]
        args += [k, v]
    if tables is not None:
        blk = pl.BlockSpec((tables[0].shape[0], tq), lambda b, p, i: (0, i))
        in_specs += [blk, blk]
        args += list(tables)
    return pl.pallas_call(
        functools.partial(_pair_kernel, n_src=len(kv), tk=tk, dk=dk, rope=tables is not None),
        grid=(batch, pairs, nq),
        in_specs=in_specs,
        out_specs=pl.BlockSpec((tq, LANES), lambda b, p, i: (b * nq + i, p)),
        out_shape=jax.ShapeDtypeStruct((t, pairs * LANES), BF16),
        scratch_shapes=_attn_scratch(n_total, 2, tk, tq),
        compiler_params=_params(("arbitrary", "arbitrary", "arbitrary")),
        name="pair_attention",
    )(*args)


NB_QROWS = 4
NB_KROWS = 12
NB_TK = NB_QROWS * GRID_W
NB_INVALID = 2 * NA_ROWS - 1


def _nb_kernel(q_ref, kc_ref, vc_ref, k_ref, v_ref, tab_ref, o_ref, vtc_ref, vt_ref, bias_ref, s_ref, p_ref):
    i = pl.program_id(2)
    rows = k_ref.shape[0] // GRID_W
    half = NA_ROWS // 2
    a = i * NB_QROWS
    base_blk = jnp.clip(i - half // NB_QROWS, 0, (rows - NB_KROWS) // NB_QROWS)
    base = base_blk * NB_QROWS
    tq = q_ref.shape[0]

    @pl.when(i == 0)
    def _():
        for ref, src in ((vtc_ref, vc_ref), (vt_ref, v_ref)):
            for c in range(ref.shape[1]):
                tiles = _vt_tiles(src[c * NB_TK:(c + 1) * NB_TK, :])
                for h in range(2):
                    ref[h, c] = tiles[h]

    def unclipped(first_row):
        return jnp.logical_and(first_row - half >= 0, first_row + NB_QROWS - 1 - half <= rows - NA_ROWS)

    @pl.when(jnp.logical_not(jnp.logical_and(unclipped(a), unclipped(a - NB_QROWS))))
    def _():
        lo_tile = lax.broadcasted_iota(jnp.int32, (GRID_W, LANES), 1) < HEAD_DIM
        for h in range(2):
            for j in range(NB_KROWS):
                kr = base + j
                for ip in range(NB_QROWS // 2):
                    halves = []
                    for qi in (2 * ip, 2 * ip + 1):
                        qr = a + qi
                        r0 = jnp.clip(qr - half, 0, rows - NA_ROWS)
                        valid = jnp.logical_and(kr >= r0, kr < r0 + NA_ROWS)
                        halves.append(tab_ref[h, jnp.where(valid, kr - qr + NA_ROWS - 1, NB_INVALID)])
                    bias_ref[h, j * GRID_W:(j + 1) * GRID_W, ip * LANES:(ip + 1) * LANES] = (
                        jnp.where(lo_tile, halves[0], halves[1]))

    top = lax.broadcasted_iota(jnp.int32, (LANES, tq), 0) < HEAD_DIM
    both = q_ref[...].astype(F32).T
    q_ts = [jnp.where(top, both, 0.0).astype(BF16), jnp.where(top, 0.0, both).astype(BF16)]
    chunks = []
    for c in range(vtc_ref.shape[1]):
        chunks.append((lambda i_, c=c: kc_ref[c * NB_TK:(c + 1) * NB_TK, :].astype(BF16),
                       lambda i_, c=c: vtc_ref[i_, c], None))
    for c in range(NB_KROWS // NB_QROWS):
        start = pl.multiple_of((base + c * NB_QROWS) * GRID_W, NB_TK)
        chunks.append((lambda i_, start=start: k_ref[pl.ds(start, NB_TK), :].astype(BF16),
                       lambda i_, c=c: vt_ref[i_, base_blk + c],
                       lambda i_, c=c: bias_ref[i_, c * NB_TK:(c + 1) * NB_TK, :]))
    outs = _pipeline(q_ts, chunks, s_ref, p_ref)
    o_ref[...] = jnp.concatenate(outs, axis=0).T.astype(o_ref.dtype)


def _nb_attention(q, k, v, kctx, vctx, table, batch):
    t, w = q.shape
    s = t // batch
    pairs = w // LANES
    tq = NB_TK
    nq = s // tq
    l = kctx.shape[1]
    assert l % NB_TK == 0 and (s // GRID_W) % NB_QROWS == 0
    return pl.pallas_call(
        _nb_kernel,
        grid=(batch, pairs, nq),
        in_specs=[
            pl.BlockSpec((tq, LANES), lambda b, p, i: (b * nq + i, p)),
            pl.BlockSpec((None, l, LANES), lambda b, p, i: (b, 0, p)),
            pl.BlockSpec((None, l, LANES), lambda b, p, i: (b, 0, p)),
            pl.BlockSpec((None, s, LANES), lambda b, p, i: (b, 0, p)),
            pl.BlockSpec((None, s, LANES), lambda b, p, i: (b, 0, p)),
            pl.BlockSpec((2, NB_INVALID + 1, GRID_W, LANES), lambda b, p, i: (p, 0, 0, 0)),
        ],
        out_specs=pl.BlockSpec((tq, LANES), lambda b, p, i: (b * nq + i, p)),
        out_shape=jax.ShapeDtypeStruct((t, w), BF16),
        scratch_shapes=[pltpu.VMEM((2, l // NB_TK, VT_ROWS, NB_TK), BF16),
                        pltpu.VMEM((2, s // NB_TK, VT_ROWS, NB_TK), BF16),
                        pltpu.VMEM((2, NB_KROWS * GRID_W, tq), F32),
                        *_pipe_scratch(2, NB_TK, tq)],
        compiler_params=_params(("arbitrary", "arbitrary", "arbitrary")),
        name="nb_attention",
    )(q, kctx, vctx, k, v, table)


def _nb_bias_table(rpb):
    h = rpb.shape[0]
    qc = jnp.arange(GRID_W)[None, :]
    kc = jnp.arange(GRID_W)[:, None]
    c0 = jnp.clip(qc - NA_COLS // 2, 0, GRID_W - NA_COLS)
    inwin = jnp.logical_and(kc >= c0, kc < c0 + NA_COLS)
    dc = jnp.clip(kc - qc + NA_COLS - 1, 0, 2 * NA_COLS - 2)
    t = jnp.where(inwin[None, None], rpb[:, :, dc].astype(F32) * LOG2E, NEG)
    t = jnp.concatenate([t, jnp.full((h, 1, GRID_W, GRID_W), NEG, F32)], axis=1)
    return jnp.concatenate([t, t], axis=-1)


PAD = 8


def _pool_kernel(u_ref, map_ref, sc_ref, o_ref, pad_ref):
    s = u_ref.shape[0]
    g = pl.program_id(1)
    u = u_ref[...]
    pad_ref[0:PAD, :] = jnp.zeros((PAD, LANES), F32)
    pad_ref[PAD + s:PAD + s + PAD, :] = jnp.zeros((PAD, LANES), F32)
    pad_ref[PAD:PAD + s, :] = u
    t = lax.broadcasted_iota(jnp.int32, (s, LANES), 0)
    for gi, win in enumerate(POOL_WINDOWS):
        @pl.when(g == gi)
        def _():
            acc = jnp.zeros((s, LANES), F32)
            for d in range(-(win // 2), win - win // 2):
                acc = acc + pad_ref[PAD + d:PAD + d + s, :]
            cnt = jnp.minimum(t + (win - win // 2), s) - jnp.maximum(t - win // 2, 0)
            diff = acc / cnt.astype(F32) - u
            y = jnp.dot(diff.astype(BF16), map_ref[...].astype(BF16), preferred_element_type=F32)
            o_ref[...] = (y * sc_ref[...]).astype(o_ref.dtype)


def _pool(u, b_map, b_scale, batch):
    t, w = u.shape
    s = t // batch
    groups = w // LANES
    return pl.pallas_call(
        _pool_kernel,
        grid=(batch, groups),
        in_specs=[
            pl.BlockSpec((s, LANES), lambda b, g: (b, g)),
            pl.BlockSpec((None, LANES, LANES), lambda b, g: (g, 0, 0)),
            pl.BlockSpec((1, LANES), lambda b, g: (0, g)),
        ],
        out_specs=pl.BlockSpec((s, LANES), lambda b, g: (b, g)),
        out_shape=jax.ShapeDtypeStruct((t, w), BF16),
        scratch_shapes=[pltpu.VMEM((s + 2 * PAD, LANES), F32)],
        compiler_params=_params(("arbitrary", "arbitrary")),
        name="pool",
    )(u, b_map, b_scale.reshape(1, w))


def _out_kernel(x_ref, a_ref, ga_ref, b_ref, gb_ref, w_ref, g_ref, gate_ref, o_ref):
    half = a_ref.shape[1]
    ta = (a_ref[...].astype(F32) * _silu(ga_ref[...].astype(F32))).astype(BF16)
    tb = (b_ref[...].astype(F32) * _silu(gb_ref[...].astype(F32))).astype(BF16)
    y = (jnp.dot(ta, w_ref[0:half, :], preferred_element_type=F32)
         + jnp.dot(tb, w_ref[half:2 * half, :], preferred_element_type=F32))
    o_ref[...] = x_ref[...] + gate_ref[...] * _rms_lanes(y, g_ref[...])


def _out_proj(x, a, ga, b, gb, w, g_post, gate, rows_per_mod, tm=512):
    t, d = x.shape
    half = a.shape[1]
    tm = min(tm, t)
    br = pl.BlockSpec((tm, half), lambda i: (i, 0))
    return pl.pallas_call(
        _out_kernel,
        grid=(t // tm,),
        in_specs=[
            pl.BlockSpec((tm, d), lambda i: (i, 0)),
            br, br, br, br,
            pl.BlockSpec((2 * half, d), lambda i: (0, 0)),
            pl.BlockSpec((1, d), lambda i: (0, 0)),
            pl.BlockSpec((None, 1, d), lambda i: ((i * tm) // rows_per_mod, 0, 0)),
        ],
        out_specs=pl.BlockSpec((tm, d), lambda i: (i, 0)),
        out_shape=jax.ShapeDtypeStruct((t, d), F32),
        compiler_params=_params(("arbitrary",)),
        name="out_proj",
    )(x, a, ga, b, gb, w, g_post, gate)


def _kvup_kernel(c_ref, pe_ref, wk_ref, wv_ref, k_ref, v_ref):
    cb = c_ref[...].astype(BF16)
    heads = k_ref.shape[1] // LANES
    k = jnp.dot(cb, wk_ref[...], preferred_element_type=F32)
    k = k + jnp.concatenate([pe_ref[...]] * heads, axis=1)
    k_ref[...] = k.astype(k_ref.dtype)
    v_ref[...] = jnp.dot(cb, wv_ref[...], preferred_element_type=F32).astype(v_ref.dtype)


def _kvup(ckv, pe128, wk, wv, tm=512):
    t, r = ckv.shape
    tm = min(tm, t)
    nk, nv = wk.shape[1], wv.shape[1]
    row = lambda width: pl.BlockSpec((tm, width), lambda i: (i, 0))
    full = lambda a: pl.BlockSpec(a.shape, lambda i: (0, 0))
    return pl.pallas_call(
        _kvup_kernel,
        grid=(t // tm,),
        in_specs=[row(r), row(LANES), full(wk), full(wv)],
        out_specs=[row(nk), row(nv)],
        out_shape=[jax.ShapeDtypeStruct((t, nk), BF16), jax.ShapeDtypeStruct((t, nv), BF16)],
        compiler_params=_params(("arbitrary",)),
        name="kv_up",
    )(ckv, pe128, wk, wv)


def kernel(x_prompt, x_sample, cache_a_k, cache_a_v, cache_c_k, cache_c_v, cache_d_ckv, cache_d_kpe,
           c, c_ctx, w_mod, b_mod, g_pre, g_post, w_in_e, a_q_norm, a_k_norm, b_map, b_scale, w_out_e,
           w_in_o, c_rpb, d_q_norm, d_w_uq, d_kv_norm, d_w_ukv, w_out_o):
    bp, sp, d = x_prompt.shape
    bs, ss, _ = x_sample.shape
    past = cache_a_k.shape[2]
    depth = w_mod.shape[0]
    tp, ts = bp * sp, bs * ss
    branch = d // 2
    d_heads = 8
    d_nope, d_rope, d_v = 64, 32, 64
    dqk = d_nope + d_rope
    q_lora, kv_lora = d_q_norm.shape[1], d_kv_norm.shape[1]
    kv_w = a_k_norm.shape[1] * 2

    xp = x_prompt.reshape(tp, d)
    xs = x_sample.reshape(ts, d)

    cond = jnp.concatenate([c_ctx[None, :], c, jnp.zeros((8 - 1 - bs, d), F32)], axis=0)
    mods = _mod_call(cond, w_mod, b_mod)

    def mod_rows(layer, part, lo_row, n_rows):
        return mods[layer, lo_row:lo_row + n_rows, part * d:(part + 1) * d].reshape(n_rows, 1, d)

    rope_a = _axial_tables_rows(ss, HEAD_DIM)
    rope_d = _axial_tables_rows(ss, d_rope)
    rope_d_lanes = _axial_tables_lanes(ss, d_rope, d_nope)
    no_rope_p = tuple(jnp.zeros((sp, LANES), F32) for _ in range(3))

    outs = {name: [] for name in ("a_k", "a_v", "c_k", "c_v", "d_ckv", "d_kpe")}
    for layer in range(depth):
        j = layer // 2
        g_in = g_pre[layer].reshape(1, d)
        g_out = g_post[layer].reshape(1, d)
        paths = (("p", xp, bp, sp, 0, 1, tp), ("s", xs, bs, ss, 1, bs, ss))
        if layer % 2 == 0:
            w_in = w_in_e[j].astype(BF16)
            w_out = w_out_e[j].astype(BF16)
            o0 = branch
            new_x = []
            for name, x, nb, seq, row0, nrows, rpm in paths:
                shift, scale, gate = (mod_rows(layer, part, row0, nrows) for part in range(3))
                v_dt = F32 if name == "p" else BF16
                segs = [(0, branch, F32), (o0, kv_w, F32), (o0 + kv_w, kv_w, v_dt),
                        (o0 + 2 * kv_w, branch, BF16), (o0 + 2 * kv_w + branch, branch, F32),
                        (o0 + 2 * kv_w + 2 * branch, branch, BF16)]
                q, k, v, ga, ub, gb = _norm_mm(x, g_in, scale, shift, w_in, segs, rpm, name="in_proj_even")
                k3, v3 = k.reshape(nb, seq, kv_w), v.reshape(nb, seq, kv_w)
                if name == "p":
                    o_a, kn = _gqa_attention(q, [(k3, v3)], a_q_norm[j], a_k_norm[j], None, nb, tq=256,
                                             emit_k=True)
                    outs["a_k"].append(kn)
                    outs["a_v"].append(v)
                else:
                    kv = [(cache_a_k[:, j].reshape(nb, past, kv_w), cache_a_v[:, j].reshape(nb, past, kv_w)),
                          (k3, v3)]
                    o_a = _gqa_attention(q, kv, a_q_norm[j], a_k_norm[j], rope_a, nb, tq=128, emit_k=False)
                o_b = _pool(ub, b_map[j], b_scale[j], nb)
                new_x.append(_out_proj(x, o_a, ga, o_b, gb, w_out, g_out, gate, rpm))
            xp, xs = new_x
        else:
            wi = w_in_o[j]
            off_kpe = 4 * branch + q_lora + kv_lora
            w_in = jnp.concatenate([wi[:, :off_kpe], jnp.zeros((d, d_nope), F32),
                                    wi[:, off_kpe:off_kpe + d_rope],
                                    jnp.zeros((d, LANES - dqk), F32),
                                    wi[:, off_kpe + d_rope:]], axis=1).astype(BF16)
            w_out = w_out_o[j].astype(BF16)
            w_uq = jnp.pad(d_w_uq[j].reshape(q_lora, d_heads, dqk),
                           ((0, 0), (0, 0), (0, LANES - dqk))).reshape(q_lora, d_heads * LANES).astype(BF16)
            ukv = d_w_ukv[j].reshape(kv_lora, d_heads, d_nope + d_v)
            w_uk = jnp.pad(ukv[:, :, :d_nope], ((0, 0), (0, 0), (0, LANES - d_nope))
                           ).reshape(kv_lora, d_heads * LANES).astype(BF16)
            w_uv = ukv[:, :, d_nope:].reshape(kv_lora, d_heads * d_v).astype(BF16)
            qn_g = d_q_norm[j].reshape(1, q_lora)
            kvn_g = d_kv_norm[j].reshape(1, kv_lora)
            table = _nb_bias_table(c_rpb[j])
            new_x = []
            for name, x, nb, seq, row0, nrows, rpm in paths:
                shift, scale, gate = (mod_rows(layer, part, row0, nrows) for part in range(3))
                latent = name == "s"
                qc, kc, vc, gc, gd, qd, ckv_n, kpe, k_d, v_d = _in_odd(
                    x, g_in, scale, shift, w_in, qn_g, kvn_g, w_uq, w_uk, w_uv,
                    rope_d_lanes if latent else no_rope_p, rpm, seq, d_rope // 4 if latent else 0,
                    BF16 if latent else F32, F32 if latent else BF16, branch,
                    LOG2E * HEAD_DIM ** -0.5, LOG2E * dqk ** -0.5)
                kc3, vc3 = kc.reshape(nb, seq, branch), vc.reshape(nb, seq, branch)
                kd3, vd3 = k_d.reshape(nb, seq, -1), v_d.reshape(nb, seq, -1)
                if not latent:
                    outs["c_k"].append(kc)
                    outs["c_v"].append(vc)
                    outs["d_ckv"].append(ckv_n)
                    outs["d_kpe"].append(kpe[:, d_nope:dqk])
                    o_c = _pair_attention(qc, [(kc3, vc3)], nb, HEAD_DIM, tq=256)
                    o_d = _pair_attention(qd, [(kd3, vd3)], nb, 2 * HEAD_DIM, tq=256)
                else:
                    o_c = _nb_attention(qc, kc3, vc3, cache_c_k[:, j].reshape(nb, past, branch),
                                        cache_c_v[:, j].reshape(nb, past, branch), table, nb)
                    pe_ctx = jnp.pad(cache_d_kpe[:, j].reshape(nb * past, d_rope),
                                     ((0, 0), (d_nope, LANES - dqk)))
                    k_ctx, v_ctx = _kvup(cache_d_ckv[:, j].reshape(nb * past, kv_lora), pe_ctx, w_uk, w_uv)
                    kv = [(k_ctx.reshape(nb, past, -1), v_ctx.reshape(nb, past, -1)), (kd3, vd3)]
                    o_d = _pair_attention(qd, kv, nb, 2 * HEAD_DIM, tq=512, tables=rope_d, tk=256)
                new_x.append(_out_proj(x, o_c, gc, o_d, gd, w_out, g_out, gate, rpm))
            xp, xs = new_x

    kvh = kv_w // HEAD_DIM
    ch = branch // HEAD_DIM

    def stacked(name, *tail):
        return jnp.stack([a.reshape(bp, sp, *tail) for a in outs[name]], axis=1)

    return (xp.reshape(bp, sp, d), xs.reshape(bs, ss, d),
            stacked("a_k", kvh, HEAD_DIM), stacked("a_v", kvh, HEAD_DIM),
            stacked("c_k", ch, HEAD_DIM), stacked("c_v", ch, HEAD_DIM),
            stacked("d_ckv", kv_lora), stacked("d_kpe", d_rope))
```

```python
import functools

import jax
import jax.numpy as jnp
from jax import lax
from jax.experimental import pallas as pl
from jax.experimental.pallas import tpu as pltpu

F32 = jnp.float32
BF16 = jnp.bfloat16

LANES = 128
GRID_W = 64
HEAD_DIM = 64
NA_ROWS = 8
NA_COLS = 16
POOL_WINDOWS = (2, 4, 8, 16)
ROPE_THETA = 10000.0
EPS = 1e-6
NEG = -1e30
LOG2E = 1.4426950408889634
VMEM_LIMIT = 56 * 1024 * 1024


def _params(sem):
    return pltpu.CompilerParams(dimension_semantics=sem, vmem_limit_bytes=VMEM_LIMIT)


def _silu(x):
    return x / (1.0 + jnp.exp(-x))


def _rms_lanes(x, gain):
    ms = jnp.mean(x * x, axis=-1, keepdims=True)
    return x * lax.rsqrt(ms + EPS) * gain


def _mod_kernel(c_ref, w_ref, b_ref, o_ref):
    s = _silu(c_ref[...]).astype(BF16)
    o_ref[...] = jnp.dot(s, w_ref[...].astype(BF16), preferred_element_type=F32) + b_ref[...]


def _mod_call(cond, w_mod, b_mod):
    depth, d, n = w_mod.shape
    rows = cond.shape[0]
    tn = 1024
    return pl.pallas_call(
        _mod_kernel,
        grid=(depth, n // tn),
        in_specs=[
            pl.BlockSpec((rows, d), lambda l, j: (0, 0)),
            pl.BlockSpec((None, d, tn), lambda l, j: (l, 0, j)),
            pl.BlockSpec((None, 1, tn), lambda l, j: (l, 0, j)),
        ],
        out_specs=pl.BlockSpec((None, rows, tn), lambda l, j: (l, 0, j)),
        out_shape=jax.ShapeDtypeStruct((depth, rows, n), F32),
        compiler_params=_params(("arbitrary", "arbitrary")),
        name="adaln_mod",
    )(cond, w_mod, b_mod.reshape(depth, 1, n))


def _norm_mm_kernel(x_ref, g_ref, sc_ref, sh_ref, w_ref, *o_refs, segs, do_norm):
    x = x_ref[...].astype(F32)
    if do_norm:
        x = _rms_lanes(x, g_ref[...]) * (1.0 + sc_ref[...]) + sh_ref[...]
    z = jnp.dot(x.astype(BF16), w_ref[...], preferred_element_type=F32)
    for o_ref, (start, width) in zip(o_refs, segs):
        o_ref[...] = z[:, start:start + width].astype(o_ref.dtype)


def _norm_mm(x, g, scale, shift, w, segs, rows_per_mod, do_norm=True, tm=512, name="norm_mm"):
    t, k = x.shape
    n = w.shape[1]
    tm = min(tm, t)
    kern = functools.partial(_norm_mm_kernel, segs=tuple((s, wd) for s, wd, _ in segs), do_norm=do_norm)
    mod_map = lambda i: ((i * tm) // rows_per_mod, 0, 0)
    return pl.pallas_call(
        kern,
        grid=(t // tm,),
        in_specs=[
            pl.BlockSpec((tm, k), lambda i: (i, 0)),
            pl.BlockSpec((1, k), lambda i: (0, 0)),
            pl.BlockSpec((None, 1, k), mod_map),
            pl.BlockSpec((None, 1, k), mod_map),
            pl.BlockSpec((k, n), lambda i: (0, 0)),
        ],
        out_specs=[pl.BlockSpec((tm, wd), lambda i: (i, 0)) for _, wd, _ in segs],
        out_shape=[jax.ShapeDtypeStruct((t, wd), dt) for _, wd, dt in segs],
        compiler_params=_params(("arbitrary",)),
        name=name,
    )(x, g, scale, shift, w)


def _in_odd_kernel(x_ref, g_ref, sc_ref, sh_ref, w_ref, qg_ref, kvg_ref, wuq_ref, wuk_ref, wuv_ref,
                   cos_ref, sa_ref, sb_ref,
                   qc_ref, kc_ref, vc_ref, gc_ref, gd_ref, qd_ref, ckv_ref, kpe_ref, kd_ref, vd_ref,
                   *, branch, q_lora, kv_lora, rope_shift, qc_mul, qd_mul):
    x = _rms_lanes(x_ref[...], g_ref[...]) * (1.0 + sc_ref[...]) + sh_ref[...]
    z = jnp.dot(x.astype(BF16), w_ref[...], preferred_element_type=F32)
    b = branch
    qc_ref[...] = (z[:, 0:b] * qc_mul).astype(qc_ref.dtype)
    kc_ref[...] = z[:, b:2 * b].astype(kc_ref.dtype)
    vc_ref[...] = z[:, 2 * b:3 * b].astype(vc_ref.dtype)
    gc_ref[...] = z[:, 3 * b:4 * b].astype(gc_ref.dtype)
    o = 4 * b
    cq = _rms_lanes(z[:, o:o + q_lora], qg_ref[...])
    qd = jnp.dot(cq.astype(BF16), wuq_ref[...], preferred_element_type=F32)
    qd_ref[...] = (qd * qd_mul).astype(qd_ref.dtype)
    o += q_lora
    ckv = _rms_lanes(z[:, o:o + kv_lora], kvg_ref[...])
    ckv_ref[...] = ckv
    o += kv_lora
    kpe = z[:, o:o + LANES]
    if rope_shift:
        kpe = (kpe * cos_ref[...] + pltpu.roll(kpe, LANES - rope_shift, 1) * sa_ref[...]
               + pltpu.roll(kpe, rope_shift, 1) * sb_ref[...])
    kpe_ref[...] = kpe
    o += LANES
    gd_ref[...] = z[:, o:o + b].astype(gd_ref.dtype)
    cb = ckv.astype(BF16)
    heads = kd_ref.shape[1] // LANES
    kd = jnp.dot(cb, wuk_ref[...], preferred_element_type=F32) + jnp.concatenate([kpe] * heads, axis=1)
    kd_ref[...] = kd.astype(kd_ref.dtype)
    vd_ref[...] = jnp.dot(cb, wuv_ref[...], preferred_element_type=F32).astype(vd_ref.dtype)


def _in_odd(x, g, scale, shift, w, qg, kvg, wuq, wuk, wuv, tables, rows_per_mod, seq, rope_shift,
            kv_dtype, qd_dtype, branch, qc_mul, qd_mul, tm=512):
    t, k = x.shape
    tm = min(tm, seq)
    nseq = seq // tm
    q_lora, kv_lora = qg.shape[1], kvg.shape[1]
    widths = [(branch, BF16), (branch, kv_dtype), (branch, kv_dtype), (branch, BF16), (branch, BF16),
              (wuq.shape[1], qd_dtype), (kv_lora, F32), (LANES, F32), (wuk.shape[1], BF16), (wuv.shape[1], BF16)]
    mod_map = lambda i: ((i * tm) // rows_per_mod, 0, 0)
    full = lambda a: pl.BlockSpec(a.shape, lambda i: (0, 0))
    tab_spec = pl.BlockSpec((tm, LANES), lambda i: (i % nseq, 0))
    kern = functools.partial(_in_odd_kernel, branch=branch, q_lora=q_lora, kv_lora=kv_lora,
                             rope_shift=rope_shift, qc_mul=qc_mul, qd_mul=qd_mul)
    return pl.pallas_call(
        kern,
        grid=(t // tm,),
        in_specs=[pl.BlockSpec((tm, k), lambda i: (i, 0)), full(g),
                  pl.BlockSpec((None, 1, k), mod_map), pl.BlockSpec((None, 1, k), mod_map),
                  full(w), full(qg), full(kvg), full(wuq), full(wuk), full(wuv),
                  tab_spec, tab_spec, tab_spec],
        out_specs=[pl.BlockSpec((tm, wd), lambda i: (i, 0)) for wd, _ in widths],
        out_shape=[jax.ShapeDtypeStruct((t, wd), dt) for wd, dt in widths],
        compiler_params=_params(("arbitrary",)),
        name="in_proj_odd",
    )(x, g, scale, shift, w, qg, kvg, wuq, wuk, wuv, *tables)


def _axial_angles(seq, dims):
    half = dims // 2
    inv = ROPE_THETA ** (-jnp.arange(0, half, 2, dtype=F32) / half)
    t = jnp.arange(seq)
    ang_r = (t // GRID_W).astype(F32)[:, None] * inv
    ang_c = (t % GRID_W).astype(F32)[:, None] * inv
    return ang_r, ang_c


def _axial_tables_lanes(seq, dims, lane0):
    ang_r, ang_c = _axial_angles(seq, dims)
    cos = jnp.concatenate([jnp.cos(ang_r)] * 2 + [jnp.cos(ang_c)] * 2, axis=-1)
    sin = jnp.concatenate([jnp.sin(ang_r)] * 2 + [jnp.sin(ang_c)] * 2, axis=-1)
    first = (jnp.arange(dims) % (dims // 2)) < dims // 4
    sa = jnp.where(first, -sin, 0.0)
    sb = jnp.where(first, 0.0, sin)
    pad_l, pad_r = lane0, LANES - lane0 - dims
    return (jnp.pad(cos, ((0, 0), (pad_l, pad_r)), constant_values=1.0),
            jnp.pad(sa, ((0, 0), (pad_l, pad_r))), jnp.pad(sb, ((0, 0), (pad_l, pad_r))))


def _axial_tables_rows(seq, dims):
    ang_r, ang_c = _axial_angles(seq, dims)
    cos = jnp.concatenate([jnp.cos(ang_r)] * 2 + [jnp.cos(ang_c)] * 2, axis=-1).T
    sin = jnp.concatenate([-jnp.sin(ang_r), jnp.sin(ang_r), -jnp.sin(ang_c), jnp.sin(ang_c)], axis=-1).T
    return cos, sin


def _rms_rows(x, gain):
    ms = jnp.mean(x * x, axis=0, keepdims=True)
    return x * lax.rsqrt(ms + EPS) * gain


def _rope_rows(x, cos, sin):
    q = x.shape[0] // 4
    partner = jnp.concatenate([x[q:2 * q], x[0:q], x[3 * q:4 * q], x[2 * q:3 * q]], axis=0)
    return x * cos + partner * sin


def _attend_t(streams, k_refs, vt_ref, s_ref, p_ref, tk):
    chunks = []
    off = 0
    for k_ref in k_refs:
        n = k_ref.shape[0]
        assert n % tk == 0
        for c in range(n // tk):
            k_fn = lambda i, k_ref=k_ref, r=c * tk: k_ref[r:r + tk, streams[i][1]].astype(BF16)
            vt_fn = lambda i, col=off + c * tk: vt_ref[i, :, col:col + tk]
            chunks.append((k_fn, vt_fn, None))
        off += n
    return _pipeline([q_t for q_t, _ in streams], chunks, s_ref, p_ref)


def _pipeline(q_ts, chunks, s_ref, p_ref):
    nc = len(chunks)
    ns = len(q_ts)
    mq = q_ts[0].shape[1]
    m = [jnp.full((1, mq), -jnp.inf, F32) for _ in range(ns)]
    acc = [jnp.zeros((VT_ROWS, mq), F32) for _ in range(ns)]
    m_chunk = [None] * ns
    alpha = [[None] * ns for _ in range(2)]
    for t in range(nc + 2):
        if t < nc:
            k_fn, _, bias_fn = chunks[t]
            for i, q_t in enumerate(q_ts):
                s = jnp.dot(k_fn(i), q_t, preferred_element_type=F32)
                if bias_fn is not None:
                    s = s + bias_fn(i)
                m_chunk[i] = jnp.max(s, axis=0, keepdims=True)
                s_ref[(t % 2) * ns + i] = s
        if 0 <= t - 2 < nc:
            for i in range(ns):
                pv = jnp.dot(chunks[t - 2][1](i), p_ref[(t % 2) * ns + i], preferred_element_type=F32)
                acc[i] = alpha[t % 2][i] * acc[i] + pv
        if 0 <= t - 1 < nc:
            for i in range(ns):
                m_new = jnp.maximum(m[i], m_prev_chunk[i])
                alpha[(t - 1) % 2][i] = jnp.exp2(m[i] - m_new)
                p = jnp.exp2(s_ref[((t - 1) % 2) * ns + i] - m_new)
                p_ref[((t - 1) % 2) * ns + i] = p.astype(BF16)
                m[i] = m_new
        m_prev_chunk = list(m_chunk)
    return [acc[i][:HEAD_DIM] / acc[i][HEAD_DIM:HEAD_DIM + 1] for i in range(ns)]


VT_ROWS = HEAD_DIM + 16


def _vt_tiles(blk):
    n = blk.shape[0]
    v_t = blk.astype(F32).T
    tail = (lax.broadcasted_iota(jnp.int32, (VT_ROWS - HEAD_DIM, n), 0) == 0).astype(F32)
    return [jnp.concatenate([v_t[h * HEAD_DIM:(h + 1) * HEAD_DIM], tail], axis=0).astype(BF16)
            for h in range(2)]


def _fill_vt(vt_ref, v_refs, chunk=512):
    off = 0
    for v_ref in v_refs:
        n = v_ref.shape[0]
        step = min(chunk, n)
        for c in range(n // step):
            for pp in range(v_ref.shape[1] // LANES):
                tiles = _vt_tiles(v_ref[c * step:(c + 1) * step, pp * LANES:(pp + 1) * LANES])
                for h in range(2):
                    vt_ref[2 * pp + h, :, off + c * step:off + (c + 1) * step] = tiles[h]
        off += n


def _pipe_scratch(n_streams, tk, mq):
    return [pltpu.VMEM((2 * n_streams, tk, mq), F32), pltpu.VMEM((2 * n_streams, tk, mq), BF16)]


def _attn_scratch(n_total, n_streams, tk, mq):
    return [pltpu.VMEM((n_streams, VT_ROWS, n_total), BF16)] + _pipe_scratch(n_streams, tk, mq)


def _gqa_kernel(*refs, n_src, tk, rope, emit_k, q_mul):
    q_ref = refs[0]
    k_refs = [refs[1 + 2 * i] for i in range(n_src)]
    v_refs = [refs[2 + 2 * i] for i in range(n_src)]
    pos = 1 + 2 * n_src
    qg_ref, kg_ref = refs[pos], refs[pos + 1]
    pos += 2
    if rope:
        cosq_ref, sinq_ref, cosk_ref, sink_ref = refs[pos:pos + 4]
        pos += 4
    o_ref = refs[pos]
    pos += 1
    if emit_k:
        kn_ref = refs[pos]
        pos += 1
    kp_ref, vt_ref, s_ref, p_ref = refs[pos:pos + 4]
    tq = q_ref.shape[0]

    @pl.when(pl.program_id(1) == 0)
    def _():
        _fill_vt(vt_ref, v_refs)
        raw = k_refs[-1]
        for c in range(raw.shape[0] // tk):
            rows = slice(c * tk, (c + 1) * tk)
            k_t = raw[rows, :].astype(F32).T
            halves = []
            for hh in range(2):
                x = _rms_rows(k_t[hh * HEAD_DIM:(hh + 1) * HEAD_DIM], kg_ref[...])
                if rope:
                    x = _rope_rows(x, cosk_ref[:, rows], sink_ref[:, rows])
                halves.append(x)
            kn = jnp.concatenate(halves, axis=0).T
            kp_ref[rows, :] = kn.astype(kp_ref.dtype)
            if emit_k:
                kn_ref[rows, :] = kn

    zeros = jnp.zeros((HEAD_DIM, tq), F32)
    parts = []
    for j in range(4):
        q_t = q_ref[:, j * LANES:(j + 1) * LANES].astype(F32).T
        for hh in range(2):
            x = _rms_rows(q_t[hh * HEAD_DIM:(hh + 1) * HEAD_DIM], qg_ref[...])
            if rope:
                x = _rope_rows(x, cosq_ref[...], sinq_ref[...])
            parts.append(x * q_mul)
    streams = []
    for g in range(2):
        cols = [jnp.concatenate([parts[4 * g + n], zeros] if g == 0 else [zeros, parts[4 * g + n]], axis=0)
                for n in range(4)]
        streams.append((jnp.concatenate(cols, axis=1).astype(BF16), slice(None)))
    outs = _attend_t(streams, k_refs[:-1] + [kp_ref], vt_ref, s_ref, p_ref, tk)
    for g in range(2):
        o_t = outs[g]
        for idx, j in enumerate((2 * g, 2 * g + 1)):
            blk_t = jnp.concatenate([o_t[:, (2 * idx) * tq:(2 * idx + 1) * tq],
                                     o_t[:, (2 * idx + 1) * tq:(2 * idx + 2) * tq]], axis=0)
            o_ref[:, j * LANES:(j + 1) * LANES] = blk_t.T.astype(o_ref.dtype)


def _gqa_attention(q, kv, q_gain, k_gain, tables, batch, tq, emit_k, tk=128):
    t, w = q.shape
    s = t // batch
    tq = min(tq, s)
    nq = s // tq
    tk = min([tk] + [k.shape[1] for k, _ in kv])
    in_specs = [pl.BlockSpec((tq, w), lambda b, i: (b * nq + i, 0))]
    args = [q]
    n_total = 0
    for k, v in kv:
        n = k.shape[1]
        n_total += n
        spec = pl.BlockSpec((None, n, LANES), lambda b, i: (b, 0, 0))
        in_specs += [spec, spec]
        args += [k, v]
    n_raw = kv[-1][0].shape[1]
    const = lambda a: pl.BlockSpec(a.shape, lambda b, i: (0, 0))
    qg = jnp.broadcast_to(q_gain[:, None], (HEAD_DIM, tq))
    kg = jnp.broadcast_to(k_gain[:, None], (HEAD_DIM, tk))
    in_specs += [const(qg), const(kg)]
    args += [qg, kg]
    if tables is not None:
        cos, sin = tables
        blk = pl.BlockSpec((HEAD_DIM, tq), lambda b, i: (0, i))
        in_specs += [blk, blk, const(cos), const(sin)]
        args += [cos, sin, cos, sin]
    out_specs = [pl.BlockSpec((tq, w), lambda b, i: (b * nq + i, 0))]
    out_shape = [jax.ShapeDtypeStruct((t, w), BF16)]
    if emit_k:
        out_specs.append(pl.BlockSpec((n_raw, LANES), lambda b, i: (b, 0)))
        out_shape.append(jax.ShapeDtypeStruct((batch * n_raw, LANES), F32))
    res = pl.pallas_call(
        functools.partial(_gqa_kernel, n_src=len(kv), tk=tk, rope=tables is not None, emit_k=emit_k,
                          q_mul=LOG2E * HEAD_DIM ** -0.5),
        grid=(batch, nq),
        in_specs=in_specs,
        out_specs=out_specs,
        out_shape=out_shape,
        scratch_shapes=[pltpu.VMEM((n_raw, LANES), BF16)] + _attn_scratch(n_total, 2, tk, 4 * tq),
        compiler_params=_params(("arbitrary", "arbitrary")),
        name="gqa_attention",
    )(*args)
    return res if emit_k else res[0]


def _pair_kernel(*refs, n_src, tk, dk, rope):
    q_ref = refs[0]
    k_refs = [refs[1 + 2 * i] for i in range(n_src)]
    v_refs = [refs[2 + 2 * i] for i in range(n_src)]
    pos = 1 + 2 * n_src
    if rope:
        cos_ref, sin_ref = refs[pos:pos + 2]
        pos += 2
    o_ref, vt_ref, s_ref, p_ref = refs[pos:pos + 4]
    tq = q_ref.shape[0]

    @pl.when(pl.program_id(2) == 0)
    def _():
        _fill_vt(vt_ref, v_refs)

    top = lax.broadcasted_iota(jnp.int32, (LANES, tq), 0) < HEAD_DIM
    pps = o_ref.shape[1] // LANES
    streams = []
    for pp in range(pps):
        for h in range(2):
            if dk == HEAD_DIM:
                lanes = slice(pp * LANES, (pp + 1) * LANES)
                both = q_ref[:, lanes].astype(F32).T
                q_t = (jnp.where(top, both, 0.0) if h == 0 else jnp.where(top, 0.0, both)).astype(BF16)
                streams.append((q_t, lanes))
            else:
                lanes = slice((2 * pp + h) * LANES, (2 * pp + h + 1) * LANES)
                q_t = q_ref[:, lanes].astype(F32).T
                if rope:
                    n_pe = cos_ref.shape[0]
                    pe = _rope_rows(q_t[HEAD_DIM:HEAD_DIM + n_pe], cos_ref[...], sin_ref[...])
                    q_t = jnp.concatenate([q_t[:HEAD_DIM], pe, q_t[HEAD_DIM + n_pe:]], axis=0)
                streams.append((q_t.astype(BF16), lanes))
    outs = _attend_t(streams, k_refs, vt_ref, s_ref, p_ref, tk)
    for pp in range(pps):
        o_ref[:, pp * LANES:(pp + 1) * LANES] = (
            jnp.concatenate(outs[2 * pp:2 * pp + 2], axis=0).T.astype(o_ref.dtype))


def _pair_attention(q, kv, batch, dk, tq, tables=None, tk=128, pps=1):
    t, w = q.shape
    qw = (LANES if dk == HEAD_DIM else 2 * LANES) * pps
    vw = LANES * pps
    groups = w // qw
    s = t // batch
    tq = min(tq, s)
    nq = s // tq
    tk = min([tk] + [k.shape[1] for k, _ in kv])
    in_specs = [pl.BlockSpec((tq, qw), lambda b, p, i: (b * nq + i, p))]
    args = [q]
    n_total = 0
    for k, v in kv:
        n = k.shape[1]
        n_total += n
        in_specs += [pl.BlockSpec((None, n, qw), lambda b, p, i: (b, 0, p)),
                     pl.BlockSpec((None, n, vw), lambda b, p, i: (b, 0, p))]
        args += [k, v]
    if tables is not None:
        blk = pl.BlockSpec((tables[0].shape[0], tq), lambda b, p, i: (0, i))
        in_specs += [blk, blk]
        args += list(tables)
    return pl.pallas_call(
        functools.partial(_pair_kernel, n_src=len(kv), tk=tk, dk=dk, rope=tables is not None),
        grid=(batch, groups, nq),
        in_specs=in_specs,
        out_specs=pl.BlockSpec((tq, vw), lambda b, p, i: (b * nq + i, p)),
        out_shape=jax.ShapeDtypeStruct((t, groups * vw), BF16),
        scratch_shapes=_attn_scratch(n_total, 2 * pps, tk, tq),
        compiler_params=_params(("arbitrary", "arbitrary", "arbitrary")),
        name="pair_attention",
    )(*args)


NB_QROWS = 4
NB_KROWS = 12
NB_TK = NB_QROWS * GRID_W
NB_INVALID = 2 * NA_ROWS - 1


def _nb_kernel(q_ref, kc_ref, vc_ref, k_ref, v_ref, tab_ref, o_ref, vtc_ref, vt_ref, bias_ref, s_ref, p_ref):
    i = pl.program_id(2)
    rows = k_ref.shape[0] // GRID_W
    half = NA_ROWS // 2
    a = i * NB_QROWS
    base_blk = jnp.clip(i - half // NB_QROWS, 0, (rows - NB_KROWS) // NB_QROWS)
    base = base_blk * NB_QROWS
    tq = q_ref.shape[0]

    @pl.when(i == 0)
    def _():
        for ref, src in ((vtc_ref, vc_ref), (vt_ref, v_ref)):
            for c in range(ref.shape[1]):
                for pp in range(src.shape[1] // LANES):
                    tiles = _vt_tiles(src[c * NB_TK:(c + 1) * NB_TK, pp * LANES:(pp + 1) * LANES])
                    for h in range(2):
                        ref[2 * pp + h, c] = tiles[h]

    def unclipped(first_row):
        return jnp.logical_and(first_row - half >= 0, first_row + NB_QROWS - 1 - half <= rows - NA_ROWS)

    @pl.when(jnp.logical_not(jnp.logical_and(unclipped(a), unclipped(a - NB_QROWS))))
    def _():
        lo_tile = lax.broadcasted_iota(jnp.int32, (GRID_W, LANES), 1) < HEAD_DIM
        for h in range(bias_ref.shape[0]):
            for j in range(NB_KROWS):
                kr = base + j
                for ip in range(NB_QROWS // 2):
                    halves = []
                    for qi in (2 * ip, 2 * ip + 1):
                        qr = a + qi
                        r0 = jnp.clip(qr - half, 0, rows - NA_ROWS)
                        valid = jnp.logical_and(kr >= r0, kr < r0 + NA_ROWS)
                        halves.append(tab_ref[h, jnp.where(valid, kr - qr + NA_ROWS - 1, NB_INVALID)])
                    bias_ref[h, j * GRID_W:(j + 1) * GRID_W, ip * LANES:(ip + 1) * LANES] = (
                        jnp.where(lo_tile, halves[0], halves[1]))

    top = lax.broadcasted_iota(jnp.int32, (LANES, tq), 0) < HEAD_DIM
    pps = q_ref.shape[1] // LANES
    q_ts = []
    for pp in range(pps):
        both = q_ref[:, pp * LANES:(pp + 1) * LANES].astype(F32).T
        q_ts += [jnp.where(top, both, 0.0).astype(BF16), jnp.where(top, 0.0, both).astype(BF16)]
    pair_lanes = lambda i_: slice((i_ // 2) * LANES, (i_ // 2 + 1) * LANES)
    chunks = []
    for c in range(vtc_ref.shape[1]):
        chunks.append((lambda i_, c=c: kc_ref[c * NB_TK:(c + 1) * NB_TK, pair_lanes(i_)].astype(BF16),
                       lambda i_, c=c: vtc_ref[i_, c], None))
    for c in range(NB_KROWS // NB_QROWS):
        start = pl.multiple_of((base + c * NB_QROWS) * GRID_W, NB_TK)
        chunks.append((lambda i_, start=start: k_ref[pl.ds(start, NB_TK), pair_lanes(i_)].astype(BF16),
                       lambda i_, c=c: vt_ref[i_, base_blk + c],
                       lambda i_, c=c: bias_ref[i_, c * NB_TK:(c + 1) * NB_TK, :]))
    outs = _pipeline(q_ts, chunks, s_ref, p_ref)
    for pp in range(pps):
        o_ref[:, pp * LANES:(pp + 1) * LANES] = (
            jnp.concatenate(outs[2 * pp:2 * pp + 2], axis=0).T.astype(o_ref.dtype))


def _nb_attention(q, k, v, kctx, vctx, table, batch, pps=2):
    t, w = q.shape
    s = t // batch
    pw = LANES * pps
    groups = w // pw
    heads = 2 * pps
    tq = NB_TK
    nq = s // tq
    l = kctx.shape[1]
    assert l % NB_TK == 0 and (s // GRID_W) % NB_QROWS == 0
    return pl.pallas_call(
        _nb_kernel,
        grid=(batch, groups, nq),
        in_specs=[
            pl.BlockSpec((tq, pw), lambda b, p, i: (b * nq + i, p)),
            pl.BlockSpec((None, l, pw), lambda b, p, i: (b, 0, p)),
            pl.BlockSpec((None, l, pw), lambda b, p, i: (b, 0, p)),
            pl.BlockSpec((None, s, pw), lambda b, p, i: (b, 0, p)),
            pl.BlockSpec((None, s, pw), lambda b, p, i: (b, 0, p)),
            pl.BlockSpec((heads, NB_INVALID + 1, GRID_W, LANES), lambda b, p, i: (p, 0, 0, 0)),
        ],
        out_specs=pl.BlockSpec((tq, pw), lambda b, p, i: (b * nq + i, p)),
        out_shape=jax.ShapeDtypeStruct((t, w), BF16),
        scratch_shapes=[pltpu.VMEM((heads, l // NB_TK, VT_ROWS, NB_TK), BF16),
                        pltpu.VMEM((heads, s // NB_TK, VT_ROWS, NB_TK), BF16),
                        pltpu.VMEM((heads, NB_KROWS * GRID_W, tq), F32),
                        *_pipe_scratch(heads, NB_TK, tq)],
        compiler_params=_params(("arbitrary", "arbitrary", "arbitrary")),
        name="nb_attention",
    )(q, kctx, vctx, k, v, table)


def _nb_bias_table(rpb):
    h = rpb.shape[0]
    qc = jnp.arange(GRID_W)[None, :]
    kc = jnp.arange(GRID_W)[:, None]
    c0 = jnp.clip(qc - NA_COLS // 2, 0, GRID_W - NA_COLS)
    inwin = jnp.logical_and(kc >= c0, kc < c0 + NA_COLS)
    dc = jnp.clip(kc - qc + NA_COLS - 1, 0, 2 * NA_COLS - 2)
    t = jnp.where(inwin[None, None], rpb[:, :, dc].astype(F32) * LOG2E, NEG)
    t = jnp.concatenate([t, jnp.full((h, 1, GRID_W, GRID_W), NEG, F32)], axis=1)
    return jnp.concatenate([t, t], axis=-1)


PAD = 8


def _pool_kernel(u_ref, map_ref, sc_ref, o_ref, pad_ref):
    s = u_ref.shape[0]
    g = pl.program_id(1)
    u = u_ref[...]
    pad_ref[0:PAD, :] = jnp.zeros((PAD, LANES), F32)
    pad_ref[PAD + s:PAD + s + PAD, :] = jnp.zeros((PAD, LANES), F32)
    pad_ref[PAD:PAD + s, :] = u
    t = lax.broadcasted_iota(jnp.int32, (s, LANES), 0)
    for gi, win in enumerate(POOL_WINDOWS):
        @pl.when(g == gi)
        def _():
            acc = jnp.zeros((s, LANES), F32)
            for d in range(-(win // 2), win - win // 2):
                acc = acc + pad_ref[PAD + d:PAD + d + s, :]
            cnt = jnp.minimum(t + (win - win // 2), s) - jnp.maximum(t - win // 2, 0)
            diff = acc / cnt.astype(F32) - u
            y = jnp.dot(diff.astype(BF16), map_ref[...].astype(BF16), preferred_element_type=F32)
            o_ref[...] = (y * sc_ref[...]).astype(o_ref.dtype)


def _pool(u, b_map, b_scale, batch):
    t, w = u.shape
    s = t // batch
    groups = w // LANES
    return pl.pallas_call(
        _pool_kernel,
        grid=(batch, groups),
        in_specs=[
            pl.BlockSpec((s, LANES), lambda b, g: (b, g)),
            pl.BlockSpec((None, LANES, LANES), lambda b, g: (g, 0, 0)),
            pl.BlockSpec((1, LANES), lambda b, g: (0, g)),
        ],
        out_specs=pl.BlockSpec((s, LANES), lambda b, g: (b, g)),
        out_shape=jax.ShapeDtypeStruct((t, w), BF16),
        scratch_shapes=[pltpu.VMEM((s + 2 * PAD, LANES), F32)],
        compiler_params=_params(("arbitrary", "arbitrary")),
        name="pool",
    )(u, b_map, b_scale.reshape(1, w))


def _out_kernel(x_ref, a_ref, ga_ref, b_ref, gb_ref, w_ref, g_ref, gate_ref, o_ref):
    half = a_ref.shape[1]
    ta = (a_ref[...].astype(F32) * _silu(ga_ref[...].astype(F32))).astype(BF16)
    tb = (b_ref[...].astype(F32) * _silu(gb_ref[...].astype(F32))).astype(BF16)
    y = (jnp.dot(ta, w_ref[0:half, :], preferred_element_type=F32)
         + jnp.dot(tb, w_ref[half:2 * half, :], preferred_element_type=F32))
    o_ref[...] = x_ref[...] + gate_ref[...] * _rms_lanes(y, g_ref[...])


def _out_proj(x, a, ga, b, gb, w, g_post, gate, rows_per_mod, tm=512):
    t, d = x.shape
    half = a.shape[1]
    tm = min(tm, t)
    br = pl.BlockSpec((tm, half), lambda i: (i, 0))
    return pl.pallas_call(
        _out_kernel,
        grid=(t // tm,),
        in_specs=[
            pl.BlockSpec((tm, d), lambda i: (i, 0)),
            br, br, br, br,
            pl.BlockSpec((2 * half, d), lambda i: (0, 0)),
            pl.BlockSpec((1, d), lambda i: (0, 0)),
            pl.BlockSpec((None, 1, d), lambda i: ((i * tm) // rows_per_mod, 0, 0)),
        ],
        out_specs=pl.BlockSpec((tm, d), lambda i: (i, 0)),
        out_shape=jax.ShapeDtypeStruct((t, d), F32),
        compiler_params=_params(("arbitrary",)),
        name="out_proj",
    )(x, a, ga, b, gb, w, g_post, gate)


def _kvup_kernel(c_ref, pe_ref, wk_ref, wv_ref, k_ref, v_ref):
    cb = c_ref[...].astype(BF16)
    heads = k_ref.shape[1] // LANES
    k = jnp.dot(cb, wk_ref[...], preferred_element_type=F32)
    k = k + jnp.concatenate([pe_ref[...]] * heads, axis=1)
    k_ref[...] = k.astype(k_ref.dtype)
    v_ref[...] = jnp.dot(cb, wv_ref[...], preferred_element_type=F32).astype(v_ref.dtype)


def _kvup(ckv, pe128, wk, wv, tm=512):
    t, r = ckv.shape
    tm = min(tm, t)
    nk, nv = wk.shape[1], wv.shape[1]
    row = lambda width: pl.BlockSpec((tm, width), lambda i: (i, 0))
    full = lambda a: pl.BlockSpec(a.shape, lambda i: (0, 0))
    return pl.pallas_call(
        _kvup_kernel,
        grid=(t // tm,),
        in_specs=[row(r), row(LANES), full(wk), full(wv)],
        out_specs=[row(nk), row(nv)],
        out_shape=[jax.ShapeDtypeStruct((t, nk), BF16), jax.ShapeDtypeStruct((t, nv), BF16)],
        compiler_params=_params(("arbitrary",)),
        name="kv_up",
    )(ckv, pe128, wk, wv)


def kernel(x_prompt, x_sample, cache_a_k, cache_a_v, cache_c_k, cache_c_v, cache_d_ckv, cache_d_kpe,
           c, c_ctx, w_mod, b_mod, g_pre, g_post, w_in_e, a_q_norm, a_k_norm, b_map, b_scale, w_out_e,
           w_in_o, c_rpb, d_q_norm, d_w_uq, d_kv_norm, d_w_ukv, w_out_o):
    bp, sp, d = x_prompt.shape
    bs, ss, _ = x_sample.shape
    past = cache_a_k.shape[2]
    depth = w_mod.shape[0]
    tp, ts = bp * sp, bs * ss
    branch = d // 2
    d_heads = 8
    d_nope, d_rope, d_v = 64, 32, 64
    dqk = d_nope + d_rope
    q_lora, kv_lora = d_q_norm.shape[1], d_kv_norm.shape[1]
    kv_w = a_k_norm.shape[1] * 2

    xp = x_prompt.reshape(tp, d)
    xs = x_sample.reshape(ts, d)

    cond = jnp.concatenate([c_ctx[None, :], c, jnp.zeros((8 - 1 - bs, d), F32)], axis=0)
    mods = _mod_call(cond, w_mod, b_mod)

    def mod_rows(layer, part, lo_row, n_rows):
        return mods[layer, lo_row:lo_row + n_rows, part * d:(part + 1) * d].reshape(n_rows, 1, d)

    rope_a = _axial_tables_rows(ss, HEAD_DIM)
    rope_d = _axial_tables_rows(ss, d_rope)
    rope_d_lanes = _axial_tables_lanes(ss, d_rope, d_nope)
    no_rope_p = tuple(jnp.zeros((sp, LANES), F32) for _ in range(3))

    outs = {name: [] for name in ("a_k", "a_v", "c_k", "c_v", "d_ckv", "d_kpe")}
    for layer in range(depth):
        j = layer // 2
        g_in = g_pre[layer].reshape(1, d)
        g_out = g_post[layer].reshape(1, d)
        paths = (("p", xp, bp, sp, 0, 1, tp), ("s", xs, bs, ss, 1, bs, ss))
        if layer % 2 == 0:
            w_in = w_in_e[j].astype(BF16)
            w_out = w_out_e[j].astype(BF16)
            o0 = branch
            new_x = []
            for name, x, nb, seq, row0, nrows, rpm in paths:
                shift, scale, gate = (mod_rows(layer, part, row0, nrows) for part in range(3))
                v_dt = F32 if name == "p" else BF16
                segs = [(0, branch, F32), (o0, kv_w, F32), (o0 + kv_w, kv_w, v_dt),
                        (o0 + 2 * kv_w, branch, BF16), (o0 + 2 * kv_w + branch, branch, F32),
                        (o0 + 2 * kv_w + 2 * branch, branch, BF16)]
                q, k, v, ga, ub, gb = _norm_mm(x, g_in, scale, shift, w_in, segs, rpm, name="in_proj_even")
                k3, v3 = k.reshape(nb, seq, kv_w), v.reshape(nb, seq, kv_w)
                if name == "p":
                    o_a, kn = _gqa_attention(q, [(k3, v3)], a_q_norm[j], a_k_norm[j], None, nb, tq=256,
                                             emit_k=True)
                    outs["a_k"].append(kn)
                    outs["a_v"].append(v)
                else:
                    kv = [(cache_a_k[:, j].reshape(nb, past, kv_w), cache_a_v[:, j].reshape(nb, past, kv_w)),
                          (k3, v3)]
                    o_a = _gqa_attention(q, kv, a_q_norm[j], a_k_norm[j], rope_a, nb, tq=128, emit_k=False)
                o_b = _pool(ub, b_map[j], b_scale[j], nb)
                new_x.append(_out_proj(x, o_a, ga, o_b, gb, w_out, g_out, gate, rpm))
            xp, xs = new_x
        else:
            wi = w_in_o[j]
            off_kpe = 4 * branch + q_lora + kv_lora
            w_in = jnp.concatenate([wi[:, :off_kpe], jnp.zeros((d, d_nope), F32),
                                    wi[:, off_kpe:off_kpe + d_rope],
                                    jnp.zeros((d, LANES - dqk), F32),
                                    wi[:, off_kpe + d_rope:]], axis=1).astype(BF16)
            w_out = w_out_o[j].astype(BF16)
            w_uq = jnp.pad(d_w_uq[j].reshape(q_lora, d_heads, dqk),
                           ((0, 0), (0, 0), (0, LANES - dqk))).reshape(q_lora, d_heads * LANES).astype(BF16)
            ukv = d_w_ukv[j].reshape(kv_lora, d_heads, d_nope + d_v)
            w_uk = jnp.pad(ukv[:, :, :d_nope], ((0, 0), (0, 0), (0, LANES - d_nope))
                           ).reshape(kv_lora, d_heads * LANES).astype(BF16)
            w_uv = ukv[:, :, d_nope:].reshape(kv_lora, d_heads * d_v).astype(BF16)
            qn_g = d_q_norm[j].reshape(1, q_lora)
            kvn_g = d_kv_norm[j].reshape(1, kv_lora)
            table = _nb_bias_table(c_rpb[j])
            new_x = []
            for name, x, nb, seq, row0, nrows, rpm in paths:
                shift, scale, gate = (mod_rows(layer, part, row0, nrows) for part in range(3))
                latent = name == "s"
                qc, kc, vc, gc, gd, qd, ckv_n, kpe, k_d, v_d = _in_odd(
                    x, g_in, scale, shift, w_in, qn_g, kvn_g, w_uq, w_uk, w_uv,
                    rope_d_lanes if latent else no_rope_p, rpm, seq, d_rope // 4 if latent else 0,
                    BF16 if latent else F32, F32 if latent else BF16, branch,
                    LOG2E * HEAD_DIM ** -0.5, LOG2E * dqk ** -0.5)
                kc3, vc3 = kc.reshape(nb, seq, branch), vc.reshape(nb, seq, branch)
                kd3, vd3 = k_d.reshape(nb, seq, -1), v_d.reshape(nb, seq, -1)
                if not latent:
                    outs["c_k"].append(kc)
                    outs["c_v"].append(vc)
                    outs["d_ckv"].append(ckv_n)
                    outs["d_kpe"].append(kpe[:, d_nope:dqk])
                    all_pairs = branch // LANES
                    o_c = _pair_attention(qc, [(kc3, vc3)], nb, HEAD_DIM, tq=256, pps=all_pairs)
                    o_d = _pair_attention(qd, [(kd3, vd3)], nb, 2 * HEAD_DIM, tq=256, pps=all_pairs)
                else:
                    o_c = _nb_attention(qc, kc3, vc3, cache_c_k[:, j].reshape(nb, past, branch),
                                        cache_c_v[:, j].reshape(nb, past, branch), table, nb)
                    pe_ctx = jnp.pad(cache_d_kpe[:, j].reshape(nb * past, d_rope),
                                     ((0, 0), (d_nope, LANES - dqk)))
                    k_ctx, v_ctx = _kvup(cache_d_ckv[:, j].reshape(nb * past, kv_lora), pe_ctx, w_uk, w_uv)
                    kv = [(k_ctx.reshape(nb, past, -1), v_ctx.reshape(nb, past, -1)), (kd3, vd3)]
                    o_d = _pair_attention(qd, kv, nb, 2 * HEAD_DIM, tq=512, tables=rope_d)
                new_x.append(_out_proj(x, o_c, gc, o_d, gd, w_out, g_out, gate, rpm))
            xp, xs = new_x

    kvh = kv_w // HEAD_DIM
    ch = branch // HEAD_DIM

    def stacked(name, *tail):
        return jnp.stack([a.reshape(bp, sp, *tail) for a in outs[name]], axis=1)

    return (xp.reshape(bp, sp, d), xs.reshape(bs, ss, d),
            stacked("a_k", kvh, HEAD_DIM), stacked("a_v", kvh, HEAD_DIM),
            stacked("c_k", ch, HEAD_DIM), stacked("c_v", ch, HEAD_DIM),
            stacked("d_ckv", kv_lora), stacked("d_kpe", d_rope))
```

```python
import functools

import jax
import jax.numpy as jnp
from jax import lax
from jax.experimental import pallas as pl
from jax.experimental.pallas import tpu as pltpu

F32 = jnp.float32
BF16 = jnp.bfloat16

LANES = 128
GRID_W = 64
HEAD_DIM = 64
NA_ROWS = 8
NA_COLS = 16
POOL_WINDOWS = (2, 4, 8, 16)
ROPE_THETA = 10000.0
EPS = 1e-6
NEG = -1e30
LOG2E = 1.4426950408889634
VMEM_LIMIT = 56 * 1024 * 1024


def _params(sem):
    return pltpu.CompilerParams(dimension_semantics=sem, vmem_limit_bytes=VMEM_LIMIT)


def _silu(x):
    h = 0.5 * x
    return h + h * jnp.tanh(h)


def _rms_lanes(x, gain):
    ms = jnp.mean(x * x, axis=-1, keepdims=True)
    return x * lax.rsqrt(ms + EPS) * gain


def _mod_kernel(c_ref, w_ref, b_ref, o_ref):
    s = _silu(c_ref[...]).astype(BF16)
    o_ref[...] = jnp.dot(s, w_ref[...].astype(BF16), preferred_element_type=F32) + b_ref[...]


def _mod_call(cond, w_mod, b_mod):
    depth, d, n = w_mod.shape
    rows = cond.shape[0]
    tn = 1024
    return pl.pallas_call(
        _mod_kernel,
        grid=(depth, n // tn),
        in_specs=[
            pl.BlockSpec((rows, d), lambda l, j: (0, 0)),
            pl.BlockSpec((None, d, tn), lambda l, j: (l, 0, j)),
            pl.BlockSpec((None, 1, tn), lambda l, j: (l, 0, j)),
        ],
        out_specs=pl.BlockSpec((None, rows, tn), lambda l, j: (l, 0, j)),
        out_shape=jax.ShapeDtypeStruct((depth, rows, n), F32),
        compiler_params=_params(("arbitrary", "arbitrary")),
        name="adaln_mod",
    )(cond, w_mod, b_mod.reshape(depth, 1, n))


def _norm_mm_kernel(x_ref, g_ref, sc_ref, sh_ref, w_ref, *o_refs, segs, do_norm):
    x = x_ref[...].astype(F32)
    if do_norm:
        x = _rms_lanes(x, g_ref[...]) * (1.0 + sc_ref[...]) + sh_ref[...]
    z = jnp.dot(x.astype(BF16), w_ref[...], preferred_element_type=F32)
    for o_ref, (start, width) in zip(o_refs, segs):
        o_ref[...] = z[:, start:start + width].astype(o_ref.dtype)


def _norm_mm(x, g, scale, shift, w, segs, rows_per_mod, do_norm=True, tm=512, name="norm_mm"):
    t, k = x.shape
    n = w.shape[1]
    tm = min(tm, t)
    kern = functools.partial(_norm_mm_kernel, segs=tuple((s, wd) for s, wd, _ in segs), do_norm=do_norm)
    mod_map = lambda i: ((i * tm) // rows_per_mod, 0, 0)
    return pl.pallas_call(
        kern,
        grid=(t // tm,),
        in_specs=[
            pl.BlockSpec((tm, k), lambda i: (i, 0)),
            pl.BlockSpec((1, k), lambda i: (0, 0)),
            pl.BlockSpec((None, 1, k), mod_map),
            pl.BlockSpec((None, 1, k), mod_map),
            pl.BlockSpec((k, n), lambda i: (0, 0)),
        ],
        out_specs=[pl.BlockSpec((tm, wd), lambda i: (i, 0)) for _, wd, _ in segs],
        out_shape=[jax.ShapeDtypeStruct((t, wd), dt) for _, wd, dt in segs],
        compiler_params=_params(("arbitrary",)),
        name=name,
    )(x, g, scale, shift, w)


def _in_odd_kernel(x_ref, g_ref, sc_ref, sh_ref, w_ref, qg_ref, kvg_ref, wuq_ref, wuk_ref, wuv_ref,
                   cos_ref, sa_ref, sb_ref,
                   qc_ref, kc_ref, vc_ref, gc_ref, gd_ref, qd_ref, ckv_ref, kpe_ref, kd_ref, vd_ref,
                   *, branch, q_lora, kv_lora, rope_shift, qc_mul, qd_mul):
    x = _rms_lanes(x_ref[...], g_ref[...]) * (1.0 + sc_ref[...]) + sh_ref[...]
    z = jnp.dot(x.astype(BF16), w_ref[...], preferred_element_type=F32)
    b = branch
    qc_ref[...] = (z[:, 0:b] * qc_mul).astype(qc_ref.dtype)
    kc_ref[...] = z[:, b:2 * b].astype(kc_ref.dtype)
    vc_ref[...] = z[:, 2 * b:3 * b].astype(vc_ref.dtype)
    gc_ref[...] = z[:, 3 * b:4 * b].astype(gc_ref.dtype)
    o = 4 * b
    cq = _rms_lanes(z[:, o:o + q_lora], qg_ref[...])
    qd = jnp.dot(cq.astype(BF16), wuq_ref[...], preferred_element_type=F32)
    qd_ref[...] = (qd * qd_mul).astype(qd_ref.dtype)
    o += q_lora
    ckv = _rms_lanes(z[:, o:o + kv_lora], kvg_ref[...])
    ckv_ref[...] = ckv
    o += kv_lora
    kpe = z[:, o:o + LANES]
    if rope_shift:
        kpe = (kpe * cos_ref[...] + pltpu.roll(kpe, LANES - rope_shift, 1) * sa_ref[...]
               + pltpu.roll(kpe, rope_shift, 1) * sb_ref[...])
    kpe_ref[...] = kpe
    o += LANES
    gd_ref[...] = z[:, o:o + b].astype(gd_ref.dtype)
    cb = ckv.astype(BF16)
    heads = kd_ref.shape[1] // LANES
    kd = jnp.dot(cb, wuk_ref[...], preferred_element_type=F32) + jnp.concatenate([kpe] * heads, axis=1)
    kd_ref[...] = kd.astype(kd_ref.dtype)
    vd_ref[...] = jnp.dot(cb, wuv_ref[...], preferred_element_type=F32).astype(vd_ref.dtype)


def _in_odd(x, g, scale, shift, w, qg, kvg, wuq, wuk, wuv, tables, rows_per_mod, seq, rope_shift,
            kv_dtype, qd_dtype, branch, qc_mul, qd_mul, tm=512):
    t, k = x.shape
    tm = min(tm, seq)
    nseq = seq // tm
    q_lora, kv_lora = qg.shape[1], kvg.shape[1]
    widths = [(branch, BF16), (branch, kv_dtype), (branch, kv_dtype), (branch, BF16), (branch, BF16),
              (wuq.shape[1], qd_dtype), (kv_lora, F32), (LANES, F32), (wuk.shape[1], BF16), (wuv.shape[1], BF16)]
    mod_map = lambda i: ((i * tm) // rows_per_mod, 0, 0)
    full = lambda a: pl.BlockSpec(a.shape, lambda i: (0, 0))
    tab_spec = pl.BlockSpec((tm, LANES), lambda i: (i % nseq, 0))
    kern = functools.partial(_in_odd_kernel, branch=branch, q_lora=q_lora, kv_lora=kv_lora,
                             rope_shift=rope_shift, qc_mul=qc_mul, qd_mul=qd_mul)
    return pl.pallas_call(
        kern,
        grid=(t // tm,),
        in_specs=[pl.BlockSpec((tm, k), lambda i: (i, 0)), full(g),
                  pl.BlockSpec((None, 1, k), mod_map), pl.BlockSpec((None, 1, k), mod_map),
                  full(w), full(qg), full(kvg), full(wuq), full(wuk), full(wuv),
                  tab_spec, tab_spec, tab_spec],
        out_specs=[pl.BlockSpec((tm, wd), lambda i: (i, 0)) for wd, _ in widths],
        out_shape=[jax.ShapeDtypeStruct((t, wd), dt) for wd, dt in widths],
        compiler_params=_params(("arbitrary",)),
        name="in_proj_odd",
    )(x, g, scale, shift, w, qg, kvg, wuq, wuk, wuv, *tables)


def _axial_angles(seq, dims):
    half = dims // 2
    inv = ROPE_THETA ** (-jnp.arange(0, half, 2, dtype=F32) / half)
    t = jnp.arange(seq)
    ang_r = (t // GRID_W).astype(F32)[:, None] * inv
    ang_c = (t % GRID_W).astype(F32)[:, None] * inv
    return ang_r, ang_c


def _axial_tables_lanes(seq, dims, lane0):
    ang_r, ang_c = _axial_angles(seq, dims)
    cos = jnp.concatenate([jnp.cos(ang_r)] * 2 + [jnp.cos(ang_c)] * 2, axis=-1)
    sin = jnp.concatenate([jnp.sin(ang_r)] * 2 + [jnp.sin(ang_c)] * 2, axis=-1)
    first = (jnp.arange(dims) % (dims // 2)) < dims // 4
    sa = jnp.where(first, -sin, 0.0)
    sb = jnp.where(first, 0.0, sin)
    pad_l, pad_r = lane0, LANES - lane0 - dims
    return (jnp.pad(cos, ((0, 0), (pad_l, pad_r)), constant_values=1.0),
            jnp.pad(sa, ((0, 0), (pad_l, pad_r))), jnp.pad(sb, ((0, 0), (pad_l, pad_r))))


def _axial_tables_rows(seq, dims):
    ang_r, ang_c = _axial_angles(seq, dims)
    cos = jnp.concatenate([jnp.cos(ang_r)] * 2 + [jnp.cos(ang_c)] * 2, axis=-1).T
    sin = jnp.concatenate([-jnp.sin(ang_r), jnp.sin(ang_r), -jnp.sin(ang_c), jnp.sin(ang_c)], axis=-1).T
    return cos, sin


def _rms_rows(x, gain):
    ms = jnp.mean(x * x, axis=0, keepdims=True)
    return x * lax.rsqrt(ms + EPS) * gain


def _rope_rows(x, cos, sin):
    q = x.shape[0] // 4
    partner = jnp.concatenate([x[q:2 * q], x[0:q], x[3 * q:4 * q], x[2 * q:3 * q]], axis=0)
    return x * cos + partner * sin


def _attend_t(streams, k_refs, vt_ref, s_ref, p_ref, tk):
    chunks = []
    off = 0
    for k_ref in k_refs:
        n = k_ref.shape[0]
        assert n % tk == 0
        for c in range(n // tk):
            k_fn = lambda i, k_ref=k_ref, r=c * tk: k_ref[r:r + tk, streams[i][1]].astype(BF16)
            vt_fn = lambda i, col=off + c * tk: vt_ref[i, :, col:col + tk]
            chunks.append((k_fn, vt_fn, None))
        off += n
    return _pipeline([q_t for q_t, _ in streams], chunks, s_ref, p_ref)


def _pipeline(q_ts, chunks, s_ref, p_ref):
    nc = len(chunks)
    ns = len(q_ts)
    mq = q_ts[0].shape[1]
    m = [jnp.full((1, mq), -jnp.inf, F32) for _ in range(ns)]
    acc = [jnp.zeros((VT_ROWS, mq), F32) for _ in range(ns)]
    m_chunk = [None] * ns
    alpha = [[None] * ns for _ in range(2)]
    for t in range(nc + 2):
        if t < nc:
            k_fn, _, bias_fn = chunks[t]
            for i, q_t in enumerate(q_ts):
                s = jnp.dot(k_fn(i), q_t, preferred_element_type=F32)
                if bias_fn is not None:
                    s = s + bias_fn(i)
                m_chunk[i] = jnp.max(s, axis=0, keepdims=True)
                s_ref[(t % 2) * ns + i] = s
        if 0 <= t - 2 < nc:
            for i in range(ns):
                pv = jnp.dot(chunks[t - 2][1](i), p_ref[(t % 2) * ns + i], preferred_element_type=F32)
                acc[i] = alpha[t % 2][i] * acc[i] + pv
        if 0 <= t - 1 < nc:
            for i in range(ns):
                m_new = jnp.maximum(m[i], m_prev_chunk[i])
                alpha[(t - 1) % 2][i] = jnp.exp2(m[i] - m_new)
                p = jnp.exp2(s_ref[((t - 1) % 2) * ns + i] - m_new)
                p_ref[((t - 1) % 2) * ns + i] = p.astype(BF16)
                m[i] = m_new
        m_prev_chunk = list(m_chunk)
    return [acc[i][:HEAD_DIM] / acc[i][HEAD_DIM:HEAD_DIM + 1] for i in range(ns)]


VT_ROWS = HEAD_DIM + 16

def _vt_tiles(blk):
    n = blk.shape[0]
    v_t = blk.astype(F32).T
    tail = (lax.broadcasted_iota(jnp.int32, (VT_ROWS - HEAD_DIM, n), 0) == 0).astype(F32)
    return [jnp.concatenate([v_t[h * HEAD_DIM:(h + 1) * HEAD_DIM], tail], axis=0).astype(BF16)
            for h in range(2)]


def _fill_vt(vt_ref, v_refs, chunk=512):
    off = 0
    for v_ref in v_refs:
        n = v_ref.shape[0]
        step = min(chunk, n)
        for c in range(n // step):
            for pp in range(v_ref.shape[1] // LANES):
                tiles = _vt_tiles(v_ref[c * step:(c + 1) * step, pp * LANES:(pp + 1) * LANES])
                for h in range(2):
                    vt_ref[2 * pp + h, :, off + c * step:off + (c + 1) * step] = tiles[h]
        off += n


def _pipe_scratch(n_streams, tk, mq):
    return [pltpu.VMEM((2 * n_streams, tk, mq), F32), pltpu.VMEM((2 * n_streams, tk, mq), BF16)]


def _attn_scratch(n_total, n_streams, tk, mq):
    return [pltpu.VMEM((n_streams, VT_ROWS, n_total), BF16)] + _pipe_scratch(n_streams, tk, mq)


def _gqa_kernel(*refs, n_src, tk, rope, emit_k, q_mul):
    q_ref = refs[0]
    k_refs = [refs[1 + 2 * i] for i in range(n_src)]
    v_refs = [refs[2 + 2 * i] for i in range(n_src)]
    pos = 1 + 2 * n_src
    qg_ref, kg_ref = refs[pos], refs[pos + 1]
    pos += 2
    if rope:
        cosq_ref, sinq_ref, cosk_ref, sink_ref = refs[pos:pos + 4]
        pos += 4
    o_ref = refs[pos]
    pos += 1
    if emit_k:
        kn_ref = refs[pos]
        pos += 1
    kp_ref, vt_ref, s_ref, p_ref = refs[pos:pos + 4]
    tq = q_ref.shape[0]

    @pl.when(pl.program_id(1) == 0)
    def _():
        _fill_vt(vt_ref, v_refs)
        raw = k_refs[-1]
        for c in range(raw.shape[0] // tk):
            rows = slice(c * tk, (c + 1) * tk)
            k_t = raw[rows, :].astype(F32).T
            halves = []
            for hh in range(2):
                x = _rms_rows(k_t[hh * HEAD_DIM:(hh + 1) * HEAD_DIM], kg_ref[...])
                if rope:
                    x = _rope_rows(x, cosk_ref[:, rows], sink_ref[:, rows])
                halves.append(x)
            kn = jnp.concatenate(halves, axis=0).T
            kp_ref[rows, :] = kn.astype(kp_ref.dtype)
            if emit_k:
                kn_ref[rows, :] = kn

    zeros = jnp.zeros((HEAD_DIM, tq), F32)
    parts = []
    for j in range(4):
        q_t = q_ref[:, j * LANES:(j + 1) * LANES].astype(F32).T
        for hh in range(2):
            x = _rms_rows(q_t[hh * HEAD_DIM:(hh + 1) * HEAD_DIM], qg_ref[...])
            if rope:
                x = _rope_rows(x, cosq_ref[...], sinq_ref[...])
            parts.append(x * q_mul)
    streams = []
    for g in range(2):
        cols = [jnp.concatenate([parts[4 * g + n], zeros] if g == 0 else [zeros, parts[4 * g + n]], axis=0)
                for n in range(4)]
        streams.append((jnp.concatenate(cols, axis=1).astype(BF16), slice(None)))
    outs = _attend_t(streams, k_refs[:-1] + [kp_ref], vt_ref, s_ref, p_ref, tk)
    for g in range(2):
        o_t = outs[g]
        for idx, j in enumerate((2 * g, 2 * g + 1)):
            blk_t = jnp.concatenate([o_t[:, (2 * idx) * tq:(2 * idx + 1) * tq],
                                     o_t[:, (2 * idx + 1) * tq:(2 * idx + 2) * tq]], axis=0)
            o_ref[:, j * LANES:(j + 1) * LANES] = blk_t.T.astype(o_ref.dtype)


def _gqa_attention(q, kv, q_gain, k_gain, tables, batch, tq, emit_k, tk=128):
    t, w = q.shape
    s = t // batch
    tq = min(tq, s)
    nq = s // tq
    tk = min([tk] + [k.shape[1] for k, _ in kv])
    in_specs = [pl.BlockSpec((tq, w), lambda b, i: (b * nq + i, 0))]
    args = [q]
    n_total = 0
    for k, v in kv:
        n = k.shape[1]
        n_total += n
        spec = pl.BlockSpec((None, n, LANES), lambda b, i: (b, 0, 0))
        in_specs += [spec, spec]
        args += [k, v]
    n_raw = kv[-1][0].shape[1]
    const = lambda a: pl.BlockSpec(a.shape, lambda b, i: (0, 0))
    qg = jnp.broadcast_to(q_gain[:, None], (HEAD_DIM, tq))
    kg = jnp.broadcast_to(k_gain[:, None], (HEAD_DIM, tk))
    in_specs += [const(qg), const(kg)]
    args += [qg, kg]
    if tables is not None:
        cos, sin = tables
        blk = pl.BlockSpec((HEAD_DIM, tq), lambda b, i: (0, i))
        in_specs += [blk, blk, const(cos), const(sin)]
        args += [cos, sin, cos, sin]
    out_specs = [pl.BlockSpec((tq, w), lambda b, i: (b * nq + i, 0))]
    out_shape = [jax.ShapeDtypeStruct((t, w), BF16)]
    if emit_k:
        out_specs.append(pl.BlockSpec((n_raw, LANES), lambda b, i: (b, 0)))
        out_shape.append(jax.ShapeDtypeStruct((batch * n_raw, LANES), F32))
    res = pl.pallas_call(
        functools.partial(_gqa_kernel, n_src=len(kv), tk=tk, rope=tables is not None, emit_k=emit_k,
                          q_mul=LOG2E * HEAD_DIM ** -0.5),
        grid=(batch, nq),
        in_specs=in_specs,
        out_specs=out_specs,
        out_shape=out_shape,
        scratch_shapes=[pltpu.VMEM((n_raw, LANES), BF16)] + _attn_scratch(n_total, 2, tk, 4 * tq),
        compiler_params=_params(("arbitrary", "arbitrary")),
        name="gqa_attention",
    )(*args)
    return res if emit_k else res[0]


def _pair_kernel(*refs, n_src, tk, dk, rope):
    q_ref = refs[0]
    k_refs = [refs[1 + 2 * i] for i in range(n_src)]
    v_refs = [refs[2 + 2 * i] for i in range(n_src)]
    pos = 1 + 2 * n_src
    if rope:
        cos_ref, sin_ref = refs[pos:pos + 2]
        pos += 2
    o_ref, vt_ref, s_ref, p_ref = refs[pos:pos + 4]
    tq = q_ref.shape[0]

    @pl.when(pl.program_id(2) == 0)
    def _():
        _fill_vt(vt_ref, v_refs)

    top = lax.broadcasted_iota(jnp.int32, (LANES, tq), 0) < HEAD_DIM
    pps = o_ref.shape[1] // LANES
    streams = []
    for pp in range(pps):
        for h in range(2):
            if dk == HEAD_DIM:
                lanes = slice(pp * LANES, (pp + 1) * LANES)
                both = q_ref[:, lanes].astype(F32).T
                q_t = (jnp.where(top, both, 0.0) if h == 0 else jnp.where(top, 0.0, both)).astype(BF16)
                streams.append((q_t, lanes))
            else:
                lanes = slice((2 * pp + h) * LANES, (2 * pp + h + 1) * LANES)
                q_t = q_ref[:, lanes].astype(F32).T
                if rope:
                    n_pe = cos_ref.shape[0]
                    pe = _rope_rows(q_t[HEAD_DIM:HEAD_DIM + n_pe], cos_ref[...], sin_ref[...])
                    q_t = jnp.concatenate([q_t[:HEAD_DIM], pe, q_t[HEAD_DIM + n_pe:]], axis=0)
                streams.append((q_t.astype(BF16), lanes))
    outs = _attend_t(streams, k_refs, vt_ref, s_ref, p_ref, tk)
    for pp in range(pps):
        o_ref[:, pp * LANES:(pp + 1) * LANES] = (
            jnp.concatenate(outs[2 * pp:2 * pp + 2], axis=0).T.astype(o_ref.dtype))


def _pair_attention(q, kv, batch, dk, tq, tables=None, tk=128, pps=1):
    t, w = q.shape
    qw = (LANES if dk == HEAD_DIM else 2 * LANES) * pps
    vw = LANES * pps
    groups = w // qw
    s = t // batch
    tq = min(tq, s)
    nq = s // tq
    tk = min([tk] + [k.shape[1] for k, _ in kv])
    in_specs = [pl.BlockSpec((tq, qw), lambda b, p, i: (b * nq + i, p))]
    args = [q]
    n_total = 0
    for k, v in kv:
        n = k.shape[1]
        n_total += n
        in_specs += [pl.BlockSpec((None, n, qw), lambda b, p, i: (b, 0, p)),
                     pl.BlockSpec((None, n, vw), lambda b, p, i: (b, 0, p))]
        args += [k, v]
    if tables is not None:
        blk = pl.BlockSpec((tables[0].shape[0], tq), lambda b, p, i: (0, i))
        in_specs += [blk, blk]
        args += list(tables)
    return pl.pallas_call(
        functools.partial(_pair_kernel, n_src=len(kv), tk=tk, dk=dk, rope=tables is not None),
        grid=(batch, groups, nq),
        in_specs=in_specs,
        out_specs=pl.BlockSpec((tq, vw), lambda b, p, i: (b * nq + i, p)),
        out_shape=jax.ShapeDtypeStruct((t, groups * vw), BF16),
        scratch_shapes=_attn_scratch(n_total, 2 * pps, tk, tq),
        compiler_params=_params(("arbitrary", "arbitrary", "arbitrary")),
        name="pair_attention",
    )(*args)


NB_QROWS = 4
NB_KROWS = 12
NB_TQ = NB_QROWS * GRID_W
NB_CROWS = 4
NB_TK = NB_CROWS * GRID_W
NB_INVALID = 2 * NA_ROWS - 1


def _nb_kernel(q_ref, kc_ref, vc_ref, k_ref, v_ref, tab_ref, o_ref, vtc_ref, vt_ref, bias_ref, s_ref, p_ref):
    i = pl.program_id(2)
    rows = k_ref.shape[0] // GRID_W
    half = NA_ROWS // 2
    a = i * NB_QROWS
    base_blk = jnp.clip(i - half // NB_QROWS, 0, (rows - NB_KROWS) // NB_QROWS)
    base = base_blk * NB_QROWS
    base_chunk = base_blk * (NB_QROWS // NB_CROWS)
    tq = q_ref.shape[0]

    @pl.when(i == 0)
    def _():
        for ref, src in ((vtc_ref, vc_ref), (vt_ref, v_ref)):
            for c in range(ref.shape[1]):
                for pp in range(src.shape[1] // LANES):
                    tiles = _vt_tiles(src[c * NB_TK:(c + 1) * NB_TK, pp * LANES:(pp + 1) * LANES])
                    for h in range(2):
                        ref[2 * pp + h, c] = tiles[h]

    def unclipped(first_row):
        return jnp.logical_and(first_row - half >= 0, first_row + NB_QROWS - 1 - half <= rows - NA_ROWS)

    @pl.when(jnp.logical_not(jnp.logical_and(unclipped(a), unclipped(a - NB_QROWS))))
    def _():
        lo_tile = lax.broadcasted_iota(jnp.int32, (GRID_W, LANES), 1) < HEAD_DIM
        for h in range(bias_ref.shape[0]):
            for j in range(NB_KROWS):
                kr = base + j
                for ip in range(NB_QROWS // 2):
                    halves = []
                    for qi in (2 * ip, 2 * ip + 1):
                        qr = a + qi
                        r0 = jnp.clip(qr - half, 0, rows - NA_ROWS)
                        valid = jnp.logical_and(kr >= r0, kr < r0 + NA_ROWS)
                        halves.append(tab_ref[h, jnp.where(valid, kr - qr + NA_ROWS - 1, NB_INVALID)])
                    bias_ref[h, j * GRID_W:(j + 1) * GRID_W, ip * LANES:(ip + 1) * LANES] = (
                        jnp.where(lo_tile, halves[0], halves[1]))

    top = lax.broadcasted_iota(jnp.int32, (LANES, tq), 0) < HEAD_DIM
    pps = q_ref.shape[1] // LANES
    q_ts = []
    for pp in range(pps):
        both = q_ref[:, pp * LANES:(pp + 1) * LANES].astype(F32).T
        q_ts += [jnp.where(top, both, 0.0).astype(BF16), jnp.where(top, 0.0, both).astype(BF16)]
    pair_lanes = lambda i_: slice((i_ // 2) * LANES, (i_ // 2 + 1) * LANES)
    chunks = []
    for c in range(vtc_ref.shape[1]):
        chunks.append((lambda i_, c=c: kc_ref[c * NB_TK:(c + 1) * NB_TK, pair_lanes(i_)].astype(BF16),
                       lambda i_, c=c: vtc_ref[i_, c], None))
    for c in range(NB_KROWS // NB_CROWS):
        start = pl.multiple_of((base + c * NB_CROWS) * GRID_W, NB_TK)
        chunks.append((lambda i_, start=start: k_ref[pl.ds(start, NB_TK), pair_lanes(i_)].astype(BF16),
                       lambda i_, c=c: vt_ref[i_, base_chunk + c],
                       lambda i_, c=c: bias_ref[i_, c * NB_TK:(c + 1) * NB_TK, :]))
    outs = _pipeline(q_ts, chunks, s_ref, p_ref)
    for pp in range(pps):
        o_ref[:, pp * LANES:(pp + 1) * LANES] = (
            jnp.concatenate(outs[2 * pp:2 * pp + 2], axis=0).T.astype(o_ref.dtype))


def _nb_attention(q, k, v, kctx, vctx, table, batch, pps=2):
    t, w = q.shape
    s = t // batch
    pw = LANES * pps
    groups = w // pw
    heads = 2 * pps
    tq = NB_TQ
    nq = s // tq
    l = kctx.shape[1]
    assert l % NB_TK == 0 and (s // GRID_W) % NB_QROWS == 0
    return pl.pallas_call(
        _nb_kernel,
        grid=(batch, groups, nq),
        in_specs=[
            pl.BlockSpec((tq, pw), lambda b, p, i: (b * nq + i, p)),
            pl.BlockSpec((None, l, pw), lambda b, p, i: (b, 0, p)),
            pl.BlockSpec((None, l, pw), lambda b, p, i: (b, 0, p)),
            pl.BlockSpec((None, s, pw), lambda b, p, i: (b, 0, p)),
            pl.BlockSpec((None, s, pw), lambda b, p, i: (b, 0, p)),
            pl.BlockSpec((heads, NB_INVALID + 1, GRID_W, LANES), lambda b, p, i: (p, 0, 0, 0)),
        ],
        out_specs=pl.BlockSpec((tq, pw), lambda b, p, i: (b * nq + i, p)),
        out_shape=jax.ShapeDtypeStruct((t, w), BF16),
        scratch_shapes=[pltpu.VMEM((heads, l // NB_TK, VT_ROWS, NB_TK), BF16),
                        pltpu.VMEM((heads, s // NB_TK, VT_ROWS, NB_TK), BF16),
                        pltpu.VMEM((heads, NB_KROWS * GRID_W, tq), F32),
                        *_pipe_scratch(heads, NB_TK, tq)],
        compiler_params=_params(("arbitrary", "arbitrary", "arbitrary")),
        name="nb_attention",
    )(q, kctx, vctx, k, v, table)


def _nb_bias_table(rpb):
    h, n_dr, n_dc = rpb.shape
    qc = jnp.arange(GRID_W)[None, :]
    kc = jnp.arange(GRID_W)[:, None]
    c0 = jnp.clip(qc - NA_COLS // 2, 0, GRID_W - NA_COLS)
    inwin = jnp.logical_and(kc >= c0, kc < c0 + NA_COLS)
    span = 2 * GRID_W - 1
    left = GRID_W - NA_COLS
    row = jnp.pad(rpb.astype(F32), ((0, 0), (0, 0), (left, span - n_dc - left)))
    shifted = jnp.tile(row, (1, 1, GRID_W + 1))[:, :, :GRID_W * (span + 1)]
    shifted = shifted.reshape(h, n_dr, GRID_W, span + 1)[..., :GRID_W]
    rel = jnp.swapaxes(jnp.flip(shifted, axis=2), 2, 3)
    t = jnp.where(inwin[None, None], rel * LOG2E, NEG)
    t = jnp.concatenate([t, jnp.full((h, 1, GRID_W, GRID_W), NEG, F32)], axis=1)
    return jnp.concatenate([t, t], axis=-1)


PAD = 8


def _window_sum(pad_ref, tmp_ref, lanes, win, tm):
    def rows(lo, n):
        return pad_ref[PAD + lo:PAD + lo + n, lanes]

    if win <= 4:
        acc = rows(-(win // 2), tm)
        for d in range(-(win // 2) + 1, win // 2):
            acc = acc + rows(d, tm)
        return acc
    ext = win // 2 - 2
    n = tm + 2 * ext
    cur = rows(-ext - 2, n)
    for d in (-1, 0, 1):
        cur = cur + rows(-ext + d, n)
    width = 4
    while width < win:
        tmp_ref[0:n, :] = cur
        h = width // 2
        prev_ext, ext = ext, ext - h
        n = tm + 2 * ext
        lo = prev_ext - ext
        cur = tmp_ref[lo - h:lo - h + n, :] + tmp_ref[lo + h:lo + h + n, :]
        width *= 2
    return cur


def _project_out(x_ref, ta, tb, w_ref, g_ref, gate_ref, o_ref):
    half = ta.shape[1]
    y = (jnp.dot(ta, w_ref[0:half, :], preferred_element_type=F32)
         + jnp.dot(tb, w_ref[half:2 * half, :], preferred_element_type=F32))
    o_ref[...] = x_ref[...] + gate_ref[...] * _rms_lanes(y, g_ref[...])


def _out_kernel(x_ref, a_ref, ga_ref, b_ref, gb_ref, w_ref, g_ref, gate_ref, o_ref):
    ta = (a_ref[...].astype(F32) * _silu(ga_ref[...].astype(F32))).astype(BF16)
    tb = (b_ref[...].astype(F32) * _silu(gb_ref[...].astype(F32))).astype(BF16)
    _project_out(x_ref, ta, tb, w_ref, g_ref, gate_ref, o_ref)


def _out_even_kernel(x_ref, a_ref, ga_ref, u_ref, up_ref, un_ref, gb_ref, map_ref, sc_ref, w_ref, g_ref,
                     gate_ref, o_ref, pad_ref, tmp_ref, *, nseq, seq):
    tm = a_ref.shape[0]
    pos = pl.program_id(0) % nseq
    u = u_ref[...]
    pad_ref[PAD:PAD + tm, :] = u
    pad_ref[0:PAD, :] = jnp.where(pos == 0, 0.0, up_ref[...])
    pad_ref[PAD + tm:PAD + tm + PAD, :] = jnp.where(pos == nseq - 1, 0.0, un_ref[...])
    t = pos * tm + lax.broadcasted_iota(jnp.int32, (tm, LANES), 0)
    ys = []
    for g, win in enumerate(POOL_WINDOWS):
        lanes = slice(g * LANES, (g + 1) * LANES)
        acc = _window_sum(pad_ref, tmp_ref, lanes, win, tm)
        cnt = jnp.minimum(t + win // 2, seq) - jnp.maximum(t - win // 2, 0)
        diff = acc / cnt.astype(F32) - u[:, lanes]
        ys.append(jnp.dot(diff.astype(BF16), map_ref[g].astype(BF16), preferred_element_type=F32))
    o_b = jnp.concatenate(ys, axis=1) * sc_ref[...]
    ta = (a_ref[...].astype(F32) * _silu(ga_ref[...].astype(F32))).astype(BF16)
    tb = (o_b * _silu(gb_ref[...].astype(F32))).astype(BF16)
    _project_out(x_ref, ta, tb, w_ref, g_ref, gate_ref, o_ref)


def _out_even(x, a, ga, u, gb, b_map, b_scale, w, g_post, gate, rows_per_mod, seq, tm=512):
    t, d = x.shape
    half = a.shape[1]
    tm = min(tm, seq)
    nseq = seq // tm
    br = pl.BlockSpec((tm, half), lambda i: (i, 0))
    halo_blocks = tm // PAD
    prev = pl.BlockSpec((PAD, half), lambda i: (jnp.maximum(i * halo_blocks - 1, 0), 0))
    nxt = pl.BlockSpec((PAD, half), lambda i: (jnp.minimum((i + 1) * halo_blocks, t // PAD - 1), 0))
    full = lambda arr: pl.BlockSpec(arr.shape, lambda i: (0,) * arr.ndim)
    sc = b_scale.reshape(1, half)
    return pl.pallas_call(
        functools.partial(_out_even_kernel, nseq=nseq, seq=seq),
        grid=(t // tm,),
        in_specs=[
            pl.BlockSpec((tm, d), lambda i: (i, 0)),
            br, br, br, prev, nxt, br, full(b_map), full(sc), full(w), full(g_post),
            pl.BlockSpec((None, 1, d), lambda i: ((i * tm) // rows_per_mod, 0, 0)),
        ],
        out_specs=pl.BlockSpec((tm, d), lambda i: (i, 0)),
        out_shape=jax.ShapeDtypeStruct((t, d), F32),
        scratch_shapes=[pltpu.VMEM((tm + 2 * PAD, half), F32), pltpu.VMEM((tm + 2 * PAD, LANES), F32)],
        compiler_params=_params(("arbitrary",)),
        name="out_proj_even",
    )(x, a, ga, u, u, u, gb, b_map, sc, w, g_post, gate)


def _out_proj(x, a, ga, b, gb, w, g_post, gate, rows_per_mod, tm=512):
    t, d = x.shape
    half = a.shape[1]
    tm = min(tm, t)
    br = pl.BlockSpec((tm, half), lambda i: (i, 0))
    return pl.pallas_call(
        _out_kernel,
        grid=(t // tm,),
        in_specs=[
            pl.BlockSpec((tm, d), lambda i: (i, 0)),
            br, br, br, br,
            pl.BlockSpec((2 * half, d), lambda i: (0, 0)),
            pl.BlockSpec((1, d), lambda i: (0, 0)),
            pl.BlockSpec((None, 1, d), lambda i: ((i * tm) // rows_per_mod, 0, 0)),
        ],
        out_specs=pl.BlockSpec((tm, d), lambda i: (i, 0)),
        out_shape=jax.ShapeDtypeStruct((t, d), F32),
        compiler_params=_params(("arbitrary",)),
        name="out_proj",
    )(x, a, ga, b, gb, w, g_post, gate)


def _kvup_kernel(c_ref, pe_ref, wk_ref, wv_ref, k_ref, v_ref):
    cb = c_ref[...].astype(BF16)
    heads = k_ref.shape[1] // LANES
    k = jnp.dot(cb, wk_ref[...], preferred_element_type=F32)
    k = k + jnp.concatenate([pe_ref[...]] * heads, axis=1)
    k_ref[...] = k.astype(k_ref.dtype)
    v_ref[...] = jnp.dot(cb, wv_ref[...], preferred_element_type=F32).astype(v_ref.dtype)


def _kvup(ckv, pe128, wk, wv, tm=512):
    t, r = ckv.shape
    tm = min(tm, t)
    nk, nv = wk.shape[1], wv.shape[1]
    row = lambda width: pl.BlockSpec((tm, width), lambda i: (i, 0))
    full = lambda a: pl.BlockSpec(a.shape, lambda i: (0, 0))
    return pl.pallas_call(
        _kvup_kernel,
        grid=(t // tm,),
        in_specs=[row(r), row(LANES), full(wk), full(wv)],
        out_specs=[row(nk), row(nv)],
        out_shape=[jax.ShapeDtypeStruct((t, nk), BF16), jax.ShapeDtypeStruct((t, nv), BF16)],
        compiler_params=_params(("arbitrary",)),
        name="kv_up",
    )(ckv, pe128, wk, wv)


def kernel(x_prompt, x_sample, cache_a_k, cache_a_v, cache_c_k, cache_c_v, cache_d_ckv, cache_d_kpe,
           c, c_ctx, w_mod, b_mod, g_pre, g_post, w_in_e, a_q_norm, a_k_norm, b_map, b_scale, w_out_e,
           w_in_o, c_rpb, d_q_norm, d_w_uq, d_kv_norm, d_w_ukv, w_out_o):
    bp, sp, d = x_prompt.shape
    bs, ss, _ = x_sample.shape
    past = cache_a_k.shape[2]
    depth = w_mod.shape[0]
    tp, ts = bp * sp, bs * ss
    branch = d // 2
    d_heads = 8
    d_nope, d_rope, d_v = 64, 32, 64
    dqk = d_nope + d_rope
    q_lora, kv_lora = d_q_norm.shape[1], d_kv_norm.shape[1]
    kv_w = a_k_norm.shape[1] * 2

    xp = x_prompt.reshape(tp, d)
    xs = x_sample.reshape(ts, d)

    cond = jnp.concatenate([c_ctx[None, :], c, jnp.zeros((8 - 1 - bs, d), F32)], axis=0)
    mods = _mod_call(cond, w_mod, b_mod)

    def mod_rows(layer, part, lo_row, n_rows):
        return mods[layer, lo_row:lo_row + n_rows, part * d:(part + 1) * d].reshape(n_rows, 1, d)

    rope_a = _axial_tables_rows(ss, HEAD_DIM)
    rope_d = _axial_tables_rows(ss, d_rope)
    rope_d_lanes = _axial_tables_lanes(ss, d_rope, d_nope)
    no_rope_p = tuple(jnp.zeros((sp, LANES), F32) for _ in range(3))

    outs = {name: [] for name in ("a_k", "a_v", "c_k", "c_v", "d_ckv", "d_kpe")}
    for layer in range(depth):
        j = layer // 2
        g_in = g_pre[layer].reshape(1, d)
        g_out = g_post[layer].reshape(1, d)
        paths = (("p", xp, bp, sp, 0, 1, tp), ("s", xs, bs, ss, 1, bs, ss))
        if layer % 2 == 0:
            w_in = w_in_e[j].astype(BF16)
            w_out = w_out_e[j].astype(BF16)
            o0 = branch
            new_x = []
            for name, x, nb, seq, row0, nrows, rpm in paths:
                shift, scale, gate = (mod_rows(layer, part, row0, nrows) for part in range(3))
                v_dt = F32 if name == "p" else BF16
                segs = [(0, branch, F32), (o0, kv_w, F32), (o0 + kv_w, kv_w, v_dt),
                        (o0 + 2 * kv_w, branch, BF16), (o0 + 2 * kv_w + branch, branch, F32),
                        (o0 + 2 * kv_w + 2 * branch, branch, BF16)]
                q, k, v, ga, ub, gb = _norm_mm(x, g_in, scale, shift, w_in, segs, rpm, name="in_proj_even")
                k3, v3 = k.reshape(nb, seq, kv_w), v.reshape(nb, seq, kv_w)
                if name == "p":
                    o_a, kn = _gqa_attention(q, [(k3, v3)], a_q_norm[j], a_k_norm[j], None, nb, tq=256,
                                             emit_k=True)
                    outs["a_k"].append(kn)
                    outs["a_v"].append(v)
                else:
                    kv = [(cache_a_k[:, j].reshape(nb, past, kv_w), cache_a_v[:, j].reshape(nb, past, kv_w)),
                          (k3, v3)]
                    o_a = _gqa_attention(q, kv, a_q_norm[j], a_k_norm[j], rope_a, nb, tq=128, emit_k=False)
                new_x.append(_out_even(x, o_a, ga, ub, gb, b_map[j], b_scale[j], w_out, g_out, gate, rpm, seq))
            xp, xs = new_x
        else:
            wi = w_in_o[j]
            off_kpe = 4 * branch + q_lora + kv_lora
            w_in = jnp.concatenate([wi[:, :off_kpe], jnp.zeros((d, d_nope), F32),
                                    wi[:, off_kpe:off_kpe + d_rope],
                                    jnp.zeros((d, LANES - dqk), F32),
                                    wi[:, off_kpe + d_rope:]], axis=1).astype(BF16)
            w_out = w_out_o[j].astype(BF16)
            w_uq = jnp.pad(d_w_uq[j].reshape(q_lora, d_heads, dqk),
                           ((0, 0), (0, 0), (0, LANES - dqk))).reshape(q_lora, d_heads * LANES).astype(BF16)
            ukv = d_w_ukv[j].reshape(kv_lora, d_heads, d_nope + d_v)
            w_uk = jnp.pad(ukv[:, :, :d_nope], ((0, 0), (0, 0), (0, LANES - d_nope))
                           ).reshape(kv_lora, d_heads * LANES).astype(BF16)
            w_uv = ukv[:, :, d_nope:].reshape(kv_lora, d_heads * d_v).astype(BF16)
            qn_g = d_q_norm[j].reshape(1, q_lora)
            kvn_g = d_kv_norm[j].reshape(1, kv_lora)
            table = _nb_bias_table(c_rpb[j])
            new_x = []
            for name, x, nb, seq, row0, nrows, rpm in paths:
                shift, scale, gate = (mod_rows(layer, part, row0, nrows) for part in range(3))
                latent = name == "s"
                qc, kc, vc, gc, gd, qd, ckv_n, kpe, k_d, v_d = _in_odd(
                    x, g_in, scale, shift, w_in, qn_g, kvn_g, w_uq, w_uk, w_uv,
                    rope_d_lanes if latent else no_rope_p, rpm, seq, d_rope // 4 if latent else 0,
                    BF16 if latent else F32, F32 if latent else BF16, branch,
                    LOG2E * HEAD_DIM ** -0.5, LOG2E * dqk ** -0.5)
                kc3, vc3 = kc.reshape(nb, seq, branch), vc.reshape(nb, seq, branch)
                kd3, vd3 = k_d.reshape(nb, seq, -1), v_d.reshape(nb, seq, -1)
                if not latent:
                    outs["c_k"].append(kc)
                    outs["c_v"].append(vc)
                    outs["d_ckv"].append(ckv_n)
                    outs["d_kpe"].append(kpe[:, d_nope:dqk])
                    all_pairs = branch // LANES
                    o_c = _pair_attention(qc, [(kc3, vc3)], nb, HEAD_DIM, tq=256, pps=all_pairs)
                    o_d = _pair_attention(qd, [(kd3, vd3)], nb, 2 * HEAD_DIM, tq=256, pps=all_pairs)
                else:
                    o_c = _nb_attention(qc, kc3, vc3, cache_c_k[:, j].reshape(nb, past, branch),
                                        cache_c_v[:, j].reshape(nb, past, branch), table, nb)
                    pe_ctx = jnp.pad(cache_d_kpe[:, j].reshape(nb * past, d_rope),
                                     ((0, 0), (d_nope, LANES - dqk)))
                    k_ctx, v_ctx = _kvup(cache_d_ckv[:, j].reshape(nb * past, kv_lora), pe_ctx, w_uk, w_uv)
                    kv = [(k_ctx.reshape(nb, past, -1), v_ctx.reshape(nb, past, -1)), (kd3, vd3)]
                    o_d = _pair_attention(qd, kv, nb, 2 * HEAD_DIM, tq=512, tables=rope_d)
                new_x.append(_out_proj(x, o_c, gc, o_d, gd, w_out, g_out, gate, rpm))
            xp, xs = new_x

    kvh = kv_w // HEAD_DIM
    ch = branch // HEAD_DIM

    def stacked(name, *tail):
        return jnp.stack([a.reshape(bp, sp, *tail) for a in outs[name]], axis=1)

    return (xp.reshape(bp, sp, d), xs.reshape(bs, ss, d),
            stacked("a_k", kvh, HEAD_DIM), stacked("a_v", kvh, HEAD_DIM),
            stacked("c_k", ch, HEAD_DIM), stacked("c_v", ch, HEAD_DIM),
            stacked("d_ckv", kv_lora), stacked("d_kpe", d_rope))
```

```python
import functools

import jax
import jax.numpy as jnp
from jax import lax
from jax.experimental import pallas as pl
from jax.experimental.pallas import tpu as pltpu

F32 = jnp.float32
BF16 = jnp.bfloat16

LANES = 128
GRID_W = 64
HEAD_DIM = 64
NA_ROWS = 8
NA_COLS = 16
POOL_WINDOWS = (2, 4, 8, 16)
ROPE_THETA = 10000.0
EPS = 1e-6
NEG = -1e30
LOG2E = 1.4426950408889634
VMEM_LIMIT = 56 * 1024 * 1024


def _params(sem):
    return pltpu.CompilerParams(dimension_semantics=sem, vmem_limit_bytes=VMEM_LIMIT)


def _silu(x):
    h = 0.5 * x
    return h + h * jnp.tanh(h)


def _rms_lanes(x, gain):
    ms = jnp.mean(x * x, axis=-1, keepdims=True)
    return x * lax.rsqrt(ms + EPS) * gain


MOD_PARTS = 3


def _mod_operand(mod, width, tm, rows_per_mod):
    table, row0, part = mod
    index = lambda i: ((row0 + (i * tm) // rows_per_mod) * MOD_PARTS + part, 0, 0)
    return table, pl.BlockSpec((None, 1, width), index)


def _mod_kernel(c_ref, w_ref, b_ref, o_ref):
    s = _silu(c_ref[...]).astype(BF16)
    o_ref[...] = jnp.dot(s, w_ref[...].astype(BF16), preferred_element_type=F32) + b_ref[...]


def _mod_call(cond, w_mod, b_mod):
    depth, d, n = w_mod.shape
    rows = cond.shape[0]
    tn = 1024
    return pl.pallas_call(
        _mod_kernel,
        grid=(depth, n // tn),
        in_specs=[
            pl.BlockSpec((rows, d), lambda l, j: (0, 0)),
            pl.BlockSpec((None, d, tn), lambda l, j: (l, 0, j)),
            pl.BlockSpec((None, 1, tn), lambda l, j: (l, 0, j)),
        ],
        out_specs=pl.BlockSpec((None, rows, tn), lambda l, j: (l, 0, j)),
        out_shape=jax.ShapeDtypeStruct((depth, rows, n), F32),
        compiler_params=_params(("arbitrary", "arbitrary")),
        name="adaln_mod",
    )(cond, w_mod, b_mod.reshape(depth, 1, n))


def _norm_mm_kernel(x_ref, g_ref, sc_ref, sh_ref, w_ref, *o_refs, segs, do_norm):
    x = x_ref[...].astype(F32)
    if do_norm:
        x = _rms_lanes(x, g_ref[...]) * (1.0 + sc_ref[...]) + sh_ref[...]
    z = jnp.dot(x.astype(BF16), w_ref[...], preferred_element_type=F32)
    for o_ref, (start, width) in zip(o_refs, segs):
        o_ref[...] = z[:, start:start + width].astype(o_ref.dtype)


def _norm_mm(x, g, scale, shift, w, segs, rows_per_mod, do_norm=True, tm=512, name="norm_mm"):
    t, k = x.shape
    n = w.shape[1]
    tm = min(tm, t)
    kern = functools.partial(_norm_mm_kernel, segs=tuple((s, wd) for s, wd, _ in segs), do_norm=do_norm)
    scale, scale_spec = _mod_operand(scale, k, tm, rows_per_mod)
    shift, shift_spec = _mod_operand(shift, k, tm, rows_per_mod)
    return pl.pallas_call(
        kern,
        grid=(t // tm,),
        in_specs=[
            pl.BlockSpec((tm, k), lambda i: (i, 0)),
            pl.BlockSpec((1, k), lambda i: (0, 0)),
            scale_spec,
            shift_spec,
            pl.BlockSpec((k, n), lambda i: (0, 0)),
        ],
        out_specs=[pl.BlockSpec((tm, wd), lambda i: (i, 0)) for _, wd, _ in segs],
        out_shape=[jax.ShapeDtypeStruct((t, wd), dt) for _, wd, dt in segs],
        compiler_params=_params(("arbitrary",)),
        name=name,
    )(x, g, scale, shift, w)


def _in_odd_kernel(x_ref, g_ref, sc_ref, sh_ref, w_ref, qg_ref, kvg_ref, wuq_ref, wuk_ref, wuv_ref,
                   cos_ref, sa_ref, sb_ref,
                   qc_ref, kc_ref, vc_ref, gc_ref, gd_ref, qd_ref, ckv_ref, kpe_ref, kd_ref, vd_ref,
                   *, branch, q_lora, kv_lora, rope_shift, qc_mul, qd_mul):
    x = _rms_lanes(x_ref[...], g_ref[...]) * (1.0 + sc_ref[...]) + sh_ref[...]
    z = jnp.dot(x.astype(BF16), w_ref[...], preferred_element_type=F32)
    b = branch
    qc_ref[...] = (z[:, 0:b] * qc_mul).astype(qc_ref.dtype)
    kc_ref[...] = z[:, b:2 * b].astype(kc_ref.dtype)
    vc_ref[...] = z[:, 2 * b:3 * b].astype(vc_ref.dtype)
    gc_ref[...] = z[:, 3 * b:4 * b].astype(gc_ref.dtype)
    o = 4 * b
    cq = _rms_lanes(z[:, o:o + q_lora], qg_ref[...])
    qd = jnp.dot(cq.astype(BF16), wuq_ref[...], preferred_element_type=F32)
    qd_ref[...] = (qd * qd_mul).astype(qd_ref.dtype)
    o += q_lora
    ckv = _rms_lanes(z[:, o:o + kv_lora], kvg_ref[...])
    ckv_ref[...] = ckv
    o += kv_lora
    kpe = z[:, o:o + LANES]
    if rope_shift:
        kpe = (kpe * cos_ref[...] + pltpu.roll(kpe, LANES - rope_shift, 1) * sa_ref[...]
               + pltpu.roll(kpe, rope_shift, 1) * sb_ref[...])
    kpe_ref[...] = kpe
    o += LANES
    gd_ref[...] = z[:, o:o + b].astype(gd_ref.dtype)
    cb = ckv.astype(BF16)
    heads = kd_ref.shape[1] // LANES
    kd = jnp.dot(cb, wuk_ref[...], preferred_element_type=F32) + jnp.concatenate([kpe] * heads, axis=1)
    kd_ref[...] = kd.astype(kd_ref.dtype)
    vd_ref[...] = jnp.dot(cb, wuv_ref[...], preferred_element_type=F32).astype(vd_ref.dtype)


def _in_odd(x, g, scale, shift, w, qg, kvg, wuq, wuk, wuv, tables, rows_per_mod, seq, rope_shift,
            kv_dtype, qd_dtype, branch, qc_mul, qd_mul, tm=512):
    t, k = x.shape
    tm = min(tm, seq)
    nseq = seq // tm
    q_lora, kv_lora = qg.shape[1], kvg.shape[1]
    widths = [(branch, BF16), (branch, kv_dtype), (branch, kv_dtype), (branch, BF16), (branch, BF16),
              (wuq.shape[1], qd_dtype), (kv_lora, F32), (LANES, F32), (wuk.shape[1], BF16), (wuv.shape[1], BF16)]
    scale, scale_spec = _mod_operand(scale, k, tm, rows_per_mod)
    shift, shift_spec = _mod_operand(shift, k, tm, rows_per_mod)
    full = lambda a: pl.BlockSpec(a.shape, lambda i: (0, 0))
    tab_spec = pl.BlockSpec((tm, LANES), lambda i: (i % nseq, 0))
    kern = functools.partial(_in_odd_kernel, branch=branch, q_lora=q_lora, kv_lora=kv_lora,
                             rope_shift=rope_shift, qc_mul=qc_mul, qd_mul=qd_mul)
    return pl.pallas_call(
        kern,
        grid=(t // tm,),
        in_specs=[pl.BlockSpec((tm, k), lambda i: (i, 0)), full(g),
                  scale_spec, shift_spec,
                  full(w), full(qg), full(kvg), full(wuq), full(wuk), full(wuv),
                  tab_spec, tab_spec, tab_spec],
        out_specs=[pl.BlockSpec((tm, wd), lambda i: (i, 0)) for wd, _ in widths],
        out_shape=[jax.ShapeDtypeStruct((t, wd), dt) for wd, dt in widths],
        compiler_params=_params(("arbitrary",)),
        name="in_proj_odd",
    )(x, g, scale, shift, w, qg, kvg, wuq, wuk, wuv, *tables)


def _axial_angles(seq, dims):
    half = dims // 2
    inv = ROPE_THETA ** (-jnp.arange(0, half, 2, dtype=F32) / half)
    t = jnp.arange(seq)
    ang_r = (t // GRID_W).astype(F32)[:, None] * inv
    ang_c = (t % GRID_W).astype(F32)[:, None] * inv
    return ang_r, ang_c


def _axial_tables_lanes(seq, dims, lane0):
    ang_r, ang_c = _axial_angles(seq, dims)
    cos = jnp.concatenate([jnp.cos(ang_r)] * 2 + [jnp.cos(ang_c)] * 2, axis=-1)
    sin = jnp.concatenate([jnp.sin(ang_r)] * 2 + [jnp.sin(ang_c)] * 2, axis=-1)
    first = (jnp.arange(dims) % (dims // 2)) < dims // 4
    sa = jnp.where(first, -sin, 0.0)
    sb = jnp.where(first, 0.0, sin)
    pad_l, pad_r = lane0, LANES - lane0 - dims
    return (jnp.pad(cos, ((0, 0), (pad_l, pad_r)), constant_values=1.0),
            jnp.pad(sa, ((0, 0), (pad_l, pad_r))), jnp.pad(sb, ((0, 0), (pad_l, pad_r))))


def _axial_tables_rows(seq, dims):
    ang_r, ang_c = _axial_angles(seq, dims)
    cos = jnp.concatenate([jnp.cos(ang_r)] * 2 + [jnp.cos(ang_c)] * 2, axis=-1).T
    sin = jnp.concatenate([-jnp.sin(ang_r), jnp.sin(ang_r), -jnp.sin(ang_c), jnp.sin(ang_c)], axis=-1).T
    return cos, sin


def _rms_rows(x, gain):
    ms = jnp.mean(x * x, axis=0, keepdims=True)
    return x * lax.rsqrt(ms + EPS) * gain


def _rope_rows(x, cos, sin):
    q = x.shape[0] // 4
    partner = jnp.concatenate([x[q:2 * q], x[0:q], x[3 * q:4 * q], x[2 * q:3 * q]], axis=0)
    return x * cos + partner * sin


def _attend_t(streams, k_refs, vt_ref, s_ref, p_ref, tk):
    chunks = []
    off = 0
    for k_ref in k_refs:
        n = k_ref.shape[0]
        assert n % tk == 0
        for c in range(n // tk):
            k_fn = lambda i, k_ref=k_ref, r=c * tk: k_ref[r:r + tk, streams[i][1]].astype(BF16)
            vt_fn = lambda i, col=off + c * tk: vt_ref[i, :, col:col + tk]
            chunks.append((k_fn, vt_fn, None))
        off += n
    return _pipeline([q_t for q_t, _ in streams], chunks, s_ref, p_ref)


def _pipeline(q_ts, chunks, s_ref, p_ref):
    nc = len(chunks)
    ns = len(q_ts)
    mq = q_ts[0].shape[1]
    m = [jnp.full((1, mq), -jnp.inf, F32) for _ in range(ns)]
    acc = [jnp.zeros((VT_ROWS, mq), F32) for _ in range(ns)]
    m_chunk = [None] * ns
    alpha = [[None] * ns for _ in range(2)]
    for t in range(nc + 2):
        if t < nc:
            k_fn, _, bias_fn = chunks[t]
            for i, q_t in enumerate(q_ts):
                s = jnp.dot(k_fn(i), q_t, preferred_element_type=F32)
                if bias_fn is not None:
                    s = s + bias_fn(i)
                m_chunk[i] = jnp.max(s, axis=0, keepdims=True)
                s_ref[(t % 2) * ns + i] = s
        if 0 <= t - 2 < nc:
            for i in range(ns):
                pv = jnp.dot(chunks[t - 2][1](i), p_ref[(t % 2) * ns + i], preferred_element_type=F32)
                acc[i] = alpha[t % 2][i] * acc[i] + pv
        if 0 <= t - 1 < nc:
            for i in range(ns):
                m_new = jnp.maximum(m[i], m_prev_chunk[i])
                alpha[(t - 1) % 2][i] = jnp.exp2(m[i] - m_new)
                p = jnp.exp2(s_ref[((t - 1) % 2) * ns + i] - m_new)
                p_ref[((t - 1) % 2) * ns + i] = p.astype(BF16)
                m[i] = m_new
        m_prev_chunk = list(m_chunk)
    return [acc[i][:HEAD_DIM] / acc[i][HEAD_DIM:HEAD_DIM + 1] for i in range(ns)]


VT_ROWS = HEAD_DIM + 16

def _vt_tiles(blk):
    n = blk.shape[0]
    v_t = blk.astype(F32).T
    tail = (lax.broadcasted_iota(jnp.int32, (VT_ROWS - HEAD_DIM, n), 0) == 0).astype(F32)
    return [jnp.concatenate([v_t[h * HEAD_DIM:(h + 1) * HEAD_DIM], tail], axis=0).astype(BF16)
            for h in range(2)]


def _fill_vt(vt_ref, v_refs, chunk=512):
    off = 0
    for v_ref in v_refs:
        n = v_ref.shape[0]
        step = min(chunk, n)
        for c in range(n // step):
            for pp in range(v_ref.shape[1] // LANES):
                tiles = _vt_tiles(v_ref[c * step:(c + 1) * step, pp * LANES:(pp + 1) * LANES])
                for h in range(2):
                    vt_ref[2 * pp + h, :, off + c * step:off + (c + 1) * step] = tiles[h]
        off += n


def _pipe_scratch(n_streams, tk, mq):
    return [pltpu.VMEM((2 * n_streams, tk, mq), F32), pltpu.VMEM((2 * n_streams, tk, mq), BF16)]


def _attn_scratch(n_total, n_streams, tk, mq):
    return [pltpu.VMEM((n_streams, VT_ROWS, n_total), BF16)] + _pipe_scratch(n_streams, tk, mq)


def _gqa_kernel(*refs, n_src, tk, rope, emit_k, q_mul):
    q_ref = refs[0]
    k_refs = [refs[1 + 2 * i] for i in range(n_src)]
    v_refs = [refs[2 + 2 * i] for i in range(n_src)]
    pos = 1 + 2 * n_src
    qg_ref, kg_ref = refs[pos], refs[pos + 1]
    pos += 2
    if rope:
        cosq_ref, sinq_ref, cosk_ref, sink_ref = refs[pos:pos + 4]
        pos += 4
    o_ref = refs[pos]
    pos += 1
    if emit_k:
        kn_ref = refs[pos]
        pos += 1
    kp_ref, vt_ref, s_ref, p_ref = refs[pos:pos + 4]
    tq = q_ref.shape[0]

    @pl.when(pl.program_id(1) == 0)
    def _():
        _fill_vt(vt_ref, v_refs)
        raw = k_refs[-1]
        for c in range(raw.shape[0] // tk):
            rows = slice(c * tk, (c + 1) * tk)
            k_t = raw[rows, :].astype(F32).T
            halves = []
            for hh in range(2):
                x = _rms_rows(k_t[hh * HEAD_DIM:(hh + 1) * HEAD_DIM], kg_ref[...])
                if rope:
                    x = _rope_rows(x, cosk_ref[:, rows], sink_ref[:, rows])
                halves.append(x)
            kn = jnp.concatenate(halves, axis=0).T
            kp_ref[rows, :] = kn.astype(kp_ref.dtype)
            if emit_k:
                kn_ref[rows, :] = kn

    zeros = jnp.zeros((HEAD_DIM, tq), F32)
    parts = []
    for j in range(4):
        q_t = q_ref[:, j * LANES:(j + 1) * LANES].astype(F32).T
        for hh in range(2):
            x = _rms_rows(q_t[hh * HEAD_DIM:(hh + 1) * HEAD_DIM], qg_ref[...])
            if rope:
                x = _rope_rows(x, cosq_ref[...], sinq_ref[...])
            parts.append(x * q_mul)
    streams = []
    for g in range(2):
        cols = [jnp.concatenate([parts[4 * g + n], zeros] if g == 0 else [zeros, parts[4 * g + n]], axis=0)
                for n in range(4)]
        streams.append((jnp.concatenate(cols, axis=1).astype(BF16), slice(None)))
    outs = _attend_t(streams, k_refs[:-1] + [kp_ref], vt_ref, s_ref, p_ref, tk)
    for g in range(2):
        o_t = outs[g]
        for idx, j in enumerate((2 * g, 2 * g + 1)):
            blk_t = jnp.concatenate([o_t[:, (2 * idx) * tq:(2 * idx + 1) * tq],
                                     o_t[:, (2 * idx + 1) * tq:(2 * idx + 2) * tq]], axis=0)
            o_ref[:, j * LANES:(j + 1) * LANES] = blk_t.T.astype(o_ref.dtype)


def _gqa_attention(q, kv, q_gain, k_gain, tables, batch, tq, emit_k, tk=128):
    t, w = q.shape
    s = t // batch
    tq = min(tq, s)
    nq = s // tq
    tk = min([tk] + [k.shape[1] for k, _ in kv])
    in_specs = [pl.BlockSpec((tq, w), lambda b, i: (b * nq + i, 0))]
    args = [q]
    n_total = 0
    for k, v in kv:
        n = k.shape[1]
        n_total += n
        spec = pl.BlockSpec((None, n, LANES), lambda b, i: (b, 0, 0))
        in_specs += [spec, spec]
        args += [k, v]
    n_raw = kv[-1][0].shape[1]
    const = lambda a: pl.BlockSpec(a.shape, lambda b, i: (0, 0))
    qg = jnp.broadcast_to(q_gain[:, None], (HEAD_DIM, tq))
    kg = jnp.broadcast_to(k_gain[:, None], (HEAD_DIM, tk))
    in_specs += [const(qg), const(kg)]
    args += [qg, kg]
    if tables is not None:
        cos, sin = tables
        blk = pl.BlockSpec((HEAD_DIM, tq), lambda b, i: (0, i))
        in_specs += [blk, blk, const(cos), const(sin)]
        args += [cos, sin, cos, sin]
    out_specs = [pl.BlockSpec((tq, w), lambda b, i: (b * nq + i, 0))]
    out_shape = [jax.ShapeDtypeStruct((t, w), BF16)]
    if emit_k:
        out_specs.append(pl.BlockSpec((n_raw, LANES), lambda b, i: (b, 0)))
        out_shape.append(jax.ShapeDtypeStruct((batch * n_raw, LANES), F32))
    res = pl.pallas_call(
        functools.partial(_gqa_kernel, n_src=len(kv), tk=tk, rope=tables is not None, emit_k=emit_k,
                          q_mul=LOG2E * HEAD_DIM ** -0.5),
        grid=(batch, nq),
        in_specs=in_specs,
        out_specs=out_specs,
        out_shape=out_shape,
        scratch_shapes=[pltpu.VMEM((n_raw, LANES), BF16)] + _attn_scratch(n_total, 2, tk, 4 * tq),
        compiler_params=_params(("arbitrary", "arbitrary")),
        name="gqa_attention",
    )(*args)
    return res if emit_k else res[0]


def _pair_kernel(*refs, n_src, tk, dk, rope):
    q_ref = refs[0]
    k_refs = [refs[1 + 2 * i] for i in range(n_src)]
    v_refs = [refs[2 + 2 * i] for i in range(n_src)]
    pos = 1 + 2 * n_src
    if rope:
        cos_ref, sin_ref = refs[pos:pos + 2]
        pos += 2
    o_ref, vt_ref, s_ref, p_ref = refs[pos:pos + 4]
    tq = q_ref.shape[0]

    @pl.when(pl.program_id(2) == 0)
    def _():
        _fill_vt(vt_ref, v_refs)

    top = lax.broadcasted_iota(jnp.int32, (LANES, tq), 0) < HEAD_DIM
    pps = o_ref.shape[1] // LANES
    streams = []
    for pp in range(pps):
        for h in range(2):
            if dk == HEAD_DIM:
                lanes = slice(pp * LANES, (pp + 1) * LANES)
                both = q_ref[:, lanes].astype(F32).T
                q_t = (jnp.where(top, both, 0.0) if h == 0 else jnp.where(top, 0.0, both)).astype(BF16)
                streams.append((q_t, lanes))
            else:
                lanes = slice((2 * pp + h) * LANES, (2 * pp + h + 1) * LANES)
                q_t = q_ref[:, lanes].astype(F32).T
                if rope:
                    n_pe = cos_ref.shape[0]
                    pe = _rope_rows(q_t[HEAD_DIM:HEAD_DIM + n_pe], cos_ref[...], sin_ref[...])
                    q_t = jnp.concatenate([q_t[:HEAD_DIM], pe, q_t[HEAD_DIM + n_pe:]], axis=0)
                streams.append((q_t.astype(BF16), lanes))
    outs = _attend_t(streams, k_refs, vt_ref, s_ref, p_ref, tk)
    for pp in range(pps):
        o_ref[:, pp * LANES:(pp + 1) * LANES] = (
            jnp.concatenate(outs[2 * pp:2 * pp + 2], axis=0).T.astype(o_ref.dtype))


def _pair_attention(q, kv, batch, dk, tq, tables=None, tk=128, pps=1):
    t, w = q.shape
    qw = (LANES if dk == HEAD_DIM else 2 * LANES) * pps
    vw = LANES * pps
    groups = w // qw
    s = t // batch
    tq = min(tq, s)
    nq = s // tq
    tk = min([tk] + [k.shape[1] for k, _ in kv])
    in_specs = [pl.BlockSpec((tq, qw), lambda b, p, i: (b * nq + i, p))]
    args = [q]
    n_total = 0
    for k, v in kv:
        n = k.shape[1]
        n_total += n
        in_specs += [pl.BlockSpec((None, n, qw), lambda b, p, i: (b, 0, p)),
                     pl.BlockSpec((None, n, vw), lambda b, p, i: (b, 0, p))]
        args += [k, v]
    if tables is not None:
        blk = pl.BlockSpec((tables[0].shape[0], tq), lambda b, p, i: (0, i))
        in_specs += [blk, blk]
        args += list(tables)
    return pl.pallas_call(
        functools.partial(_pair_kernel, n_src=len(kv), tk=tk, dk=dk, rope=tables is not None),
        grid=(batch, groups, nq),
        in_specs=in_specs,
        out_specs=pl.BlockSpec((tq, vw), lambda b, p, i: (b * nq + i, p)),
        out_shape=jax.ShapeDtypeStruct((t, groups * vw), BF16),
        scratch_shapes=_attn_scratch(n_total, 2 * pps, tk, tq),
        compiler_params=_params(("arbitrary", "arbitrary", "arbitrary")),
        name="pair_attention",
    )(*args)


NB_QROWS = 4
NB_KROWS = 12
NB_TQ = NB_QROWS * GRID_W
NB_CROWS = 4
NB_TK = NB_CROWS * GRID_W
NB_INVALID = 2 * NA_ROWS - 1


def _nb_kernel(q_ref, kc_ref, vc_ref, k_ref, v_ref, tab_ref, o_ref, vtc_ref, vt_ref, bias_ref, s_ref, p_ref):
    i = pl.program_id(2)
    rows = k_ref.shape[0] // GRID_W
    half = NA_ROWS // 2
    a = i * NB_QROWS
    base_blk = jnp.clip(i - half // NB_QROWS, 0, (rows - NB_KROWS) // NB_QROWS)
    base = base_blk * NB_QROWS
    base_chunk = base_blk * (NB_QROWS // NB_CROWS)
    tq = q_ref.shape[0]

    @pl.when(i == 0)
    def _():
        for ref, src in ((vtc_ref, vc_ref), (vt_ref, v_ref)):
            for c in range(ref.shape[1]):
                for pp in range(src.shape[1] // LANES):
                    tiles = _vt_tiles(src[c * NB_TK:(c + 1) * NB_TK, pp * LANES:(pp + 1) * LANES])
                    for h in range(2):
                        ref[2 * pp + h, c] = tiles[h]

    def unclipped(first_row):
        return jnp.logical_and(first_row - half >= 0, first_row + NB_QROWS - 1 - half <= rows - NA_ROWS)

    @pl.when(jnp.logical_not(jnp.logical_and(unclipped(a), unclipped(a - NB_QROWS))))
    def _():
        lo_tile = lax.broadcasted_iota(jnp.int32, (GRID_W, LANES), 1) < HEAD_DIM
        for h in range(bias_ref.shape[0]):
            for j in range(NB_KROWS):
                kr = base + j
                for ip in range(NB_QROWS // 2):
                    halves = []
                    for qi in (2 * ip, 2 * ip + 1):
                        qr = a + qi
                        r0 = jnp.clip(qr - half, 0, rows - NA_ROWS)
                        valid = jnp.logical_and(kr >= r0, kr < r0 + NA_ROWS)
                        halves.append(tab_ref[h, jnp.where(valid, kr - qr + NA_ROWS - 1, NB_INVALID)])
                    bias_ref[h, j * GRID_W:(j + 1) * GRID_W, ip * LANES:(ip + 1) * LANES] = (
                        jnp.where(lo_tile, halves[0], halves[1]))

    top = lax.broadcasted_iota(jnp.int32, (LANES, tq), 0) < HEAD_DIM
    pps = q_ref.shape[1] // LANES
    q_ts = []
    for pp in range(pps):
        both = q_ref[:, pp * LANES:(pp + 1) * LANES].astype(F32).T
        q_ts += [jnp.where(top, both, 0.0).astype(BF16), jnp.where(top, 0.0, both).astype(BF16)]
    pair_lanes = lambda i_: slice((i_ // 2) * LANES, (i_ // 2 + 1) * LANES)
    chunks = []
    for c in range(vtc_ref.shape[1]):
        chunks.append((lambda i_, c=c: kc_ref[c * NB_TK:(c + 1) * NB_TK, pair_lanes(i_)].astype(BF16),
                       lambda i_, c=c: vtc_ref[i_, c], None))
    for c in range(NB_KROWS // NB_CROWS):
        start = pl.multiple_of((base + c * NB_CROWS) * GRID_W, NB_TK)
        chunks.append((lambda i_, start=start: k_ref[pl.ds(start, NB_TK), pair_lanes(i_)].astype(BF16),
                       lambda i_, c=c: vt_ref[i_, base_chunk + c],
                       lambda i_, c=c: bias_ref[i_, c * NB_TK:(c + 1) * NB_TK, :]))
    outs = _pipeline(q_ts, chunks, s_ref, p_ref)
    for pp in range(pps):
        o_ref[:, pp * LANES:(pp + 1) * LANES] = (
            jnp.concatenate(outs[2 * pp:2 * pp + 2], axis=0).T.astype(o_ref.dtype))


def _nb_attention(q, k, v, kctx, vctx, table, batch, pps=4):
    t, w = q.shape
    s = t // batch
    pw = LANES * pps
    groups = w // pw
    heads = 2 * pps
    tq = NB_TQ
    nq = s // tq
    l = kctx.shape[1]
    assert l % NB_TK == 0 and (s // GRID_W) % NB_QROWS == 0
    return pl.pallas_call(
        _nb_kernel,
        grid=(batch, groups, nq),
        in_specs=[
            pl.BlockSpec((tq, pw), lambda b, p, i: (b * nq + i, p)),
            pl.BlockSpec((None, l, pw), lambda b, p, i: (b, 0, p)),
            pl.BlockSpec((None, l, pw), lambda b, p, i: (b, 0, p)),
            pl.BlockSpec((None, s, pw), lambda b, p, i: (b, 0, p)),
            pl.BlockSpec((None, s, pw), lambda b, p, i: (b, 0, p)),
            pl.BlockSpec((heads, NB_INVALID + 1, GRID_W, LANES), lambda b, p, i: (p, 0, 0, 0)),
        ],
        out_specs=pl.BlockSpec((tq, pw), lambda b, p, i: (b * nq + i, p)),
        out_shape=jax.ShapeDtypeStruct((t, w), BF16),
        scratch_shapes=[pltpu.VMEM((heads, l // NB_TK, VT_ROWS, NB_TK), BF16),
                        pltpu.VMEM((heads, s // NB_TK, VT_ROWS, NB_TK), BF16),
                        pltpu.VMEM((heads, NB_KROWS * GRID_W, tq), F32),
                        *_pipe_scratch(heads, NB_TK, tq)],
        compiler_params=_params(("arbitrary", "arbitrary", "arbitrary")),
        name="nb_attention",
    )(q, kctx, vctx, k, v, table)


def _nb_bias_table(rpb):
    h, n_dr, n_dc = rpb.shape
    qc = jnp.arange(GRID_W)[None, :]
    kc = jnp.arange(GRID_W)[:, None]
    c0 = jnp.clip(qc - NA_COLS // 2, 0, GRID_W - NA_COLS)
    inwin = jnp.logical_and(kc >= c0, kc < c0 + NA_COLS)
    period = 2 * GRID_W - 1
    rev = rpb.astype(F32)[..., ::-1]
    seq = jnp.concatenate([rev[..., NA_COLS - 1:], jnp.zeros((h, n_dr, period - n_dc), F32),
                           rev[..., :NA_COLS - 1]], axis=-1)
    rel = jnp.tile(seq, (1, 1, GRID_W))[..., :GRID_W * (period - 1)]
    rel = rel.reshape(h, n_dr, GRID_W, period - 1)[..., :GRID_W]
    t = jnp.where(inwin[None, None], rel * LOG2E, NEG)
    t = jnp.concatenate([t, jnp.full((h, 1, GRID_W, GRID_W), NEG, F32)], axis=1)
    return jnp.concatenate([t, t], axis=-1)


PAD = 8


def _window_sum(pad_ref, tmp_ref, lanes, win, tm):
    def rows(lo, n):
        return pad_ref[PAD + lo:PAD + lo + n, lanes]

    if win <= 4:
        acc = rows(-(win // 2), tm)
        for d in range(-(win // 2) + 1, win // 2):
            acc = acc + rows(d, tm)
        return acc
    ext = win // 2 - 2
    n = tm + 2 * ext
    cur = rows(-ext - 2, n)
    for d in (-1, 0, 1):
        cur = cur + rows(-ext + d, n)
    width = 4
    while width < win:
        tmp_ref[0:n, :] = cur
        h = width // 2
        prev_ext, ext = ext, ext - h
        n = tm + 2 * ext
        lo = prev_ext - ext
        cur = tmp_ref[lo - h:lo - h + n, :] + tmp_ref[lo + h:lo + h + n, :]
        width *= 2
    return cur


def _project_out(x_ref, ta, tb, w_ref, g_ref, gate_ref, o_ref):
    half = ta.shape[1]
    y = (jnp.dot(ta, w_ref[0:half, :], preferred_element_type=F32)
         + jnp.dot(tb, w_ref[half:2 * half, :], preferred_element_type=F32))
    o_ref[...] = x_ref[...] + gate_ref[...] * _rms_lanes(y, g_ref[...])


def _out_kernel(x_ref, a_ref, ga_ref, b_ref, gb_ref, w_ref, g_ref, gate_ref, o_ref):
    ta = (a_ref[...].astype(F32) * _silu(ga_ref[...].astype(F32))).astype(BF16)
    tb = (b_ref[...].astype(F32) * _silu(gb_ref[...].astype(F32))).astype(BF16)
    _project_out(x_ref, ta, tb, w_ref, g_ref, gate_ref, o_ref)


def _out_even_kernel(x_ref, a_ref, ga_ref, u_ref, up_ref, un_ref, gb_ref, map_ref, sc_ref, w_ref, g_ref,
                     gate_ref, o_ref, pad_ref, tmp_ref, *, nseq, seq):
    tm = a_ref.shape[0]
    pos = pl.program_id(0) % nseq
    u = u_ref[...]
    pad_ref[PAD:PAD + tm, :] = u
    pad_ref[0:PAD, :] = jnp.where(pos == 0, 0.0, up_ref[...])
    pad_ref[PAD + tm:PAD + tm + PAD, :] = jnp.where(pos == nseq - 1, 0.0, un_ref[...])
    t = pos * tm + lax.broadcasted_iota(jnp.int32, (tm, LANES), 0)
    ys = []
    for g, win in enumerate(POOL_WINDOWS):
        lanes = slice(g * LANES, (g + 1) * LANES)
        acc = _window_sum(pad_ref, tmp_ref, lanes, win, tm)
        cnt = jnp.minimum(t + win // 2, seq) - jnp.maximum(t - win // 2, 0)
        diff = acc / cnt.astype(F32) - u[:, lanes]
        ys.append(jnp.dot(diff.astype(BF16), map_ref[g].astype(BF16), preferred_element_type=F32))
    o_b = jnp.concatenate(ys, axis=1) * sc_ref[...]
    ta = (a_ref[...].astype(F32) * _silu(ga_ref[...].astype(F32))).astype(BF16)
    tb = (o_b * _silu(gb_ref[...].astype(F32))).astype(BF16)
    _project_out(x_ref, ta, tb, w_ref, g_ref, gate_ref, o_ref)


def _out_even(x, a, ga, u, gb, b_map, b_scale, w, g_post, gate, rows_per_mod, seq, tm=512):
    t, d = x.shape
    half = a.shape[1]
    tm = min(tm, seq)
    nseq = seq // tm
    br = pl.BlockSpec((tm, half), lambda i: (i, 0))
    halo_blocks = tm // PAD
    prev = pl.BlockSpec((PAD, half), lambda i: (jnp.maximum(i * halo_blocks - 1, 0), 0))
    nxt = pl.BlockSpec((PAD, half), lambda i: (jnp.minimum((i + 1) * halo_blocks, t // PAD - 1), 0))
    full = lambda arr: pl.BlockSpec(arr.shape, lambda i: (0,) * arr.ndim)
    sc = b_scale.reshape(1, half)
    gate, gate_spec = _mod_operand(gate, d, tm, rows_per_mod)
    return pl.pallas_call(
        functools.partial(_out_even_kernel, nseq=nseq, seq=seq),
        grid=(t // tm,),
        in_specs=[
            pl.BlockSpec((tm, d), lambda i: (i, 0)),
            br, br, br, prev, nxt, br, full(b_map), full(sc), full(w), full(g_post), gate_spec,
        ],
        out_specs=pl.BlockSpec((tm, d), lambda i: (i, 0)),
        out_shape=jax.ShapeDtypeStruct((t, d), F32),
        scratch_shapes=[pltpu.VMEM((tm + 2 * PAD, half), F32), pltpu.VMEM((tm + 2 * PAD, LANES), F32)],
        compiler_params=_params(("arbitrary",)),
        name="out_proj_even",
    )(x, a, ga, u, u, u, gb, b_map, sc, w, g_post, gate)


def _out_proj(x, a, ga, b, gb, w, g_post, gate, rows_per_mod, tm=512):
    t, d = x.shape
    half = a.shape[1]
    tm = min(tm, t)
    br = pl.BlockSpec((tm, half), lambda i: (i, 0))
    gate, gate_spec = _mod_operand(gate, d, tm, rows_per_mod)
    return pl.pallas_call(
        _out_kernel,
        grid=(t // tm,),
        in_specs=[
            pl.BlockSpec((tm, d), lambda i: (i, 0)),
            br, br, br, br,
            pl.BlockSpec((2 * half, d), lambda i: (0, 0)),
            pl.BlockSpec((1, d), lambda i: (0, 0)),
            gate_spec,
        ],
        out_specs=pl.BlockSpec((tm, d), lambda i: (i, 0)),
        out_shape=jax.ShapeDtypeStruct((t, d), F32),
        compiler_params=_params(("arbitrary",)),
        name="out_proj",
    )(x, a, ga, b, gb, w, g_post, gate)


def _kvup_kernel(c_ref, pe_ref, wk_ref, wv_ref, k_ref, v_ref):
    cb = c_ref[...].astype(BF16)
    heads = k_ref.shape[1] // LANES
    k = jnp.dot(cb, wk_ref[...], preferred_element_type=F32)
    k = k + jnp.concatenate([pe_ref[...]] * heads, axis=1)
    k_ref[...] = k.astype(k_ref.dtype)
    v_ref[...] = jnp.dot(cb, wv_ref[...], preferred_element_type=F32).astype(v_ref.dtype)


def _kvup(ckv, pe128, wk, wv, tm=512):
    t, r = ckv.shape
    tm = min(tm, t)
    nk, nv = wk.shape[1], wv.shape[1]
    row = lambda width: pl.BlockSpec((tm, width), lambda i: (i, 0))
    full = lambda a: pl.BlockSpec(a.shape, lambda i: (0, 0))
    return pl.pallas_call(
        _kvup_kernel,
        grid=(t // tm,),
        in_specs=[row(r), row(LANES), full(wk), full(wv)],
        out_specs=[row(nk), row(nv)],
        out_shape=[jax.ShapeDtypeStruct((t, nk), BF16), jax.ShapeDtypeStruct((t, nv), BF16)],
        compiler_params=_params(("arbitrary",)),
        name="kv_up",
    )(ckv, pe128, wk, wv)


def kernel(x_prompt, x_sample, cache_a_k, cache_a_v, cache_c_k, cache_c_v, cache_d_ckv, cache_d_kpe,
           c, c_ctx, w_mod, b_mod, g_pre, g_post, w_in_e, a_q_norm, a_k_norm, b_map, b_scale, w_out_e,
           w_in_o, c_rpb, d_q_norm, d_w_uq, d_kv_norm, d_w_ukv, w_out_o):
    bp, sp, d = x_prompt.shape
    bs, ss, _ = x_sample.shape
    past = cache_a_k.shape[2]
    depth = w_mod.shape[0]
    tp, ts = bp * sp, bs * ss
    branch = d // 2
    d_heads = 8
    d_nope, d_rope, d_v = 64, 32, 64
    dqk = d_nope + d_rope
    q_lora, kv_lora = d_q_norm.shape[1], d_kv_norm.shape[1]
    kv_w = a_k_norm.shape[1] * 2

    xp = x_prompt.reshape(tp, d)
    xs = x_sample.reshape(ts, d)

    cond = jnp.concatenate([c_ctx[None, :], c, jnp.zeros((8 - 1 - bs, d), F32)], axis=0)
    mods = _mod_call(cond, w_mod, b_mod)

    mod_table = mods.reshape(depth * cond.shape[0] * MOD_PARTS, 1, d)

    def mod_rows(layer, part, lo_row):
        return mod_table, layer * cond.shape[0] + lo_row, part

    rope_a = _axial_tables_rows(ss, HEAD_DIM)
    rope_d = _axial_tables_rows(ss, d_rope)
    rope_d_lanes = _axial_tables_lanes(ss, d_rope, d_nope)
    no_rope_p = tuple(jnp.zeros((sp, LANES), F32) for _ in range(3))

    outs = {name: [] for name in ("a_k", "a_v", "c_k", "c_v", "d_ckv", "d_kpe")}
    for layer in range(depth):
        j = layer // 2
        g_in = g_pre[layer].reshape(1, d)
        g_out = g_post[layer].reshape(1, d)
        paths = (("p", xp, bp, sp, 0, 1, tp), ("s", xs, bs, ss, 1, bs, ss))
        if layer % 2 == 0:
            w_in = w_in_e[j].astype(BF16)
            w_out = w_out_e[j].astype(BF16)
            o0 = branch
            new_x = []
            for name, x, nb, seq, row0, nrows, rpm in paths:
                shift, scale, gate = (mod_rows(layer, part, row0) for part in range(MOD_PARTS))
                v_dt = F32 if name == "p" else BF16
                segs = [(0, branch, F32), (o0, kv_w, F32), (o0 + kv_w, kv_w, v_dt),
                        (o0 + 2 * kv_w, branch, BF16), (o0 + 2 * kv_w + branch, branch, F32),
                        (o0 + 2 * kv_w + 2 * branch, branch, BF16)]
                q, k, v, ga, ub, gb = _norm_mm(x, g_in, scale, shift, w_in, segs, rpm, name="in_proj_even")
                k3, v3 = k.reshape(nb, seq, kv_w), v.reshape(nb, seq, kv_w)
                if name == "p":
                    o_a, kn = _gqa_attention(q, [(k3, v3)], a_q_norm[j], a_k_norm[j], None, nb, tq=256,
                                             emit_k=True)
                    outs["a_k"].append(kn)
                    outs["a_v"].append(v)
                else:
                    kv = [(cache_a_k[:, j].reshape(nb, past, kv_w), cache_a_v[:, j].reshape(nb, past, kv_w)),
                          (k3, v3)]
                    o_a = _gqa_attention(q, kv, a_q_norm[j], a_k_norm[j], rope_a, nb, tq=256, emit_k=False)
                new_x.append(_out_even(x, o_a, ga, ub, gb, b_map[j], b_scale[j], w_out, g_out, gate, rpm, seq))
            xp, xs = new_x
        else:
            wi = w_in_o[j]
            off_kpe = 4 * branch + q_lora + kv_lora
            w_in = jnp.concatenate([wi[:, :off_kpe], jnp.zeros((d, d_nope), F32),
                                    wi[:, off_kpe:off_kpe + d_rope],
                                    jnp.zeros((d, LANES - dqk), F32),
                                    wi[:, off_kpe + d_rope:]], axis=1).astype(BF16)
            w_out = w_out_o[j].astype(BF16)
            w_uq = jnp.pad(d_w_uq[j].reshape(q_lora, d_heads, dqk),
                           ((0, 0), (0, 0), (0, LANES - dqk))).reshape(q_lora, d_heads * LANES).astype(BF16)
            ukv = d_w_ukv[j].reshape(kv_lora, d_heads, d_nope + d_v)
            w_uk = jnp.pad(ukv[:, :, :d_nope], ((0, 0), (0, 0), (0, LANES - d_nope))
                           ).reshape(kv_lora, d_heads * LANES).astype(BF16)
            w_uv = ukv[:, :, d_nope:].reshape(kv_lora, d_heads * d_v).astype(BF16)
            qn_g = d_q_norm[j].reshape(1, q_lora)
            kvn_g = d_kv_norm[j].reshape(1, kv_lora)
            table = _nb_bias_table(c_rpb[j])
            new_x = []
            for name, x, nb, seq, row0, nrows, rpm in paths:
                shift, scale, gate = (mod_rows(layer, part, row0) for part in range(MOD_PARTS))
                latent = name == "s"
                qc, kc, vc, gc, gd, qd, ckv_n, kpe, k_d, v_d = _in_odd(
                    x, g_in, scale, shift, w_in, qn_g, kvn_g, w_uq, w_uk, w_uv,
                    rope_d_lanes if latent else no_rope_p, rpm, seq, d_rope // 4 if latent else 0,
                    BF16 if latent else F32, F32 if latent else BF16, branch,
                    LOG2E * HEAD_DIM ** -0.5, LOG2E * dqk ** -0.5)
                kc3, vc3 = kc.reshape(nb, seq, branch), vc.reshape(nb, seq, branch)
                kd3, vd3 = k_d.reshape(nb, seq, -1), v_d.reshape(nb, seq, -1)
                if not latent:
                    outs["c_k"].append(kc)
                    outs["c_v"].append(vc)
                    outs["d_ckv"].append(ckv_n)
                    outs["d_kpe"].append(kpe[:, d_nope:dqk])
                    all_pairs = branch // LANES
                    o_c = _pair_attention(qc, [(kc3, vc3)], nb, HEAD_DIM, tq=256, pps=all_pairs)
                    o_d = _pair_attention(qd, [(kd3, vd3)], nb, 2 * HEAD_DIM, tq=256, pps=all_pairs)
                else:
                    o_c = _nb_attention(qc, kc3, vc3, cache_c_k[:, j].reshape(nb, past, branch),
                                        cache_c_v[:, j].reshape(nb, past, branch), table, nb)
                    pe_ctx = jnp.pad(cache_d_kpe[:, j].reshape(nb * past, d_rope),
                                     ((0, 0), (d_nope, LANES - dqk)))
                    k_ctx, v_ctx = _kvup(cache_d_ckv[:, j].reshape(nb * past, kv_lora), pe_ctx, w_uk, w_uv)
                    kv = [(k_ctx.reshape(nb, past, -1), v_ctx.reshape(nb, past, -1)), (kd3, vd3)]
                    o_d = _pair_attention(qd, kv, nb, 2 * HEAD_DIM, tq=512, tables=rope_d, pps=2)
                new_x.append(_out_proj(x, o_c, gc, o_d, gd, w_out, g_out, gate, rpm))
            xp, xs = new_x

    kvh = kv_w // HEAD_DIM
    ch = branch // HEAD_DIM

    def stacked(name, *tail):
        return jnp.stack([a.reshape(bp, sp, *tail) for a in outs[name]], axis=1)

    return (xp.reshape(bp, sp, d), xs.reshape(bs, ss, d),
            stacked("a_k", kvh, HEAD_DIM), stacked("a_v", kvh, HEAD_DIM),
            stacked("c_k", ch, HEAD_DIM), stacked("c_v", ch, HEAD_DIM),
            stacked("d_ckv", kv_lora), stacked("d_kpe", d_rope))
```

```python
import functools

import jax
import jax.numpy as jnp
from jax import lax
from jax.experimental import pallas as pl
from jax.experimental.pallas import tpu as pltpu

F32 = jnp.float32
BF16 = jnp.bfloat16

LANES = 128
GRID_W = 64
HEAD_DIM = 64
NA_ROWS = 8
NA_COLS = 16
POOL_WINDOWS = (2, 4, 8, 16)
ROPE_THETA = 10000.0
EPS = 1e-6
NEG = -1e30
LOG2E = 1.4426950408889634
VMEM_LIMIT = 56 * 1024 * 1024


def _params(sem):
    return pltpu.CompilerParams(dimension_semantics=sem, vmem_limit_bytes=VMEM_LIMIT)


def _silu(x):
    h = 0.5 * x
    return h + h * jnp.tanh(h)


def _rms_lanes(x, gain):
    ms = jnp.mean(x * x, axis=-1, keepdims=True)
    return x * lax.rsqrt(ms + EPS) * gain


MOD_PARTS = 3


def _mod_operand(mod, width, tm, rows_per_mod):
    table, row0, part = mod
    index = lambda i: ((row0 + (i * tm) // rows_per_mod) * MOD_PARTS + part, 0, 0)
    return table, pl.BlockSpec((None, 1, width), index)


def _mod_kernel(c_ref, w_ref, b_ref, o_ref):
    s = _silu(c_ref[...]).astype(BF16)
    o_ref[...] = jnp.dot(s, w_ref[...].astype(BF16), preferred_element_type=F32) + b_ref[...]


def _mod_call(cond, w_mod, b_mod):
    depth, d, n = w_mod.shape
    rows = cond.shape[0]
    tn = 1024
    return pl.pallas_call(
        _mod_kernel,
        grid=(depth, n // tn),
        in_specs=[
            pl.BlockSpec((rows, d), lambda l, j: (0, 0)),
            pl.BlockSpec((None, d, tn), lambda l, j: (l, 0, j)),
            pl.BlockSpec((None, 1, tn), lambda l, j: (l, 0, j)),
        ],
        out_specs=pl.BlockSpec((None, rows, tn), lambda l, j: (l, 0, j)),
        out_shape=jax.ShapeDtypeStruct((depth, rows, n), F32),
        compiler_params=_params(("arbitrary", "arbitrary")),
        name="adaln_mod",
    )(cond, w_mod, b_mod.reshape(depth, 1, n))


def _norm_mm_kernel(x_ref, g_ref, sc_ref, sh_ref, w_ref, *o_refs, segs, do_norm):
    x = x_ref[...].astype(F32)
    if do_norm:
        x = _rms_lanes(x, g_ref[...]) * (1.0 + sc_ref[...]) + sh_ref[...]
    z = jnp.dot(x.astype(BF16), w_ref[...], preferred_element_type=F32)
    for o_ref, (start, width) in zip(o_refs, segs):
        o_ref[...] = z[:, start:start + width].astype(o_ref.dtype)


def _norm_mm(x, g, scale, shift, w, segs, rows_per_mod, do_norm=True, tm=512, name="norm_mm"):
    t, k = x.shape
    n = w.shape[1]
    tm = min(tm, t)
    kern = functools.partial(_norm_mm_kernel, segs=tuple((s, wd) for s, wd, _ in segs), do_norm=do_norm)
    scale, scale_spec = _mod_operand(scale, k, tm, rows_per_mod)
    shift, shift_spec = _mod_operand(shift, k, tm, rows_per_mod)
    return pl.pallas_call(
        kern,
        grid=(t // tm,),
        in_specs=[
            pl.BlockSpec((tm, k), lambda i: (i, 0)),
            pl.BlockSpec((1, k), lambda i: (0, 0)),
            scale_spec,
            shift_spec,
            pl.BlockSpec((k, n), lambda i: (0, 0)),
        ],
        out_specs=[pl.BlockSpec((tm, wd), lambda i: (i, 0)) for _, wd, _ in segs],
        out_shape=[jax.ShapeDtypeStruct((t, wd), dt) for _, wd, dt in segs],
        compiler_params=_params(("arbitrary",)),
        name=name,
    )(x, g, scale, shift, w)


def _in_odd_kernel(x_ref, g_ref, sc_ref, sh_ref, w_ref, qg_ref, kvg_ref, wuq_ref, wuk_ref, wuv_ref,
                   cos_ref, sa_ref, sb_ref,
                   qc_ref, kc_ref, vc_ref, gc_ref, gd_ref, qd_ref, ckv_ref, kpe_ref, kd_ref, vd_ref,
                   *cache_refs, branch, q_lora, kv_lora, rope_shift, qc_mul, qd_mul):
    x = _rms_lanes(x_ref[...], g_ref[...]) * (1.0 + sc_ref[...]) + sh_ref[...]
    z = jnp.dot(x.astype(BF16), w_ref[...], preferred_element_type=F32)
    b = branch
    qc_ref[...] = (z[:, 0:b] * qc_mul).astype(qc_ref.dtype)
    kc_ref[...] = z[:, b:2 * b].astype(kc_ref.dtype)
    vc_ref[...] = z[:, 2 * b:3 * b].astype(vc_ref.dtype)
    gc_ref[...] = z[:, 3 * b:4 * b].astype(gc_ref.dtype)
    if cache_refs:
        _store_heads(cache_refs[0], slice(None), z[:, b:2 * b])
        _store_heads(cache_refs[1], slice(None), z[:, 2 * b:3 * b])
    o = 4 * b
    cq = _rms_lanes(z[:, o:o + q_lora], qg_ref[...])
    qd = jnp.dot(cq.astype(BF16), wuq_ref[...], preferred_element_type=F32)
    qd_ref[...] = (qd * qd_mul).astype(qd_ref.dtype)
    o += q_lora
    ckv = _rms_lanes(z[:, o:o + kv_lora], kvg_ref[...])
    ckv_ref[...] = ckv
    o += kv_lora
    kpe = z[:, o:o + LANES]
    if rope_shift:
        kpe = (kpe * cos_ref[...] + pltpu.roll(kpe, LANES - rope_shift, 1) * sa_ref[...]
               + pltpu.roll(kpe, rope_shift, 1) * sb_ref[...])
    kpe_ref[...] = kpe
    o += LANES
    gd_ref[...] = z[:, o:o + b].astype(gd_ref.dtype)
    cb = ckv.astype(BF16)
    heads = kd_ref.shape[1] // LANES
    kd = jnp.dot(cb, wuk_ref[...], preferred_element_type=F32) + jnp.concatenate([kpe] * heads, axis=1)
    kd_ref[...] = kd.astype(kd_ref.dtype)
    vd_ref[...] = jnp.dot(cb, wuv_ref[...], preferred_element_type=F32).astype(vd_ref.dtype)


def _in_odd(x, g, scale, shift, w, qg, kvg, wuq, wuk, wuv, tables, rows_per_mod, seq, rope_shift,
            emit_cache, qd_dtype, branch, qc_mul, qd_mul, tm=512):
    t, k = x.shape
    tm = min(tm, seq)
    nseq = seq // tm
    q_lora, kv_lora = qg.shape[1], kvg.shape[1]
    widths = [(branch, BF16), (branch, BF16), (branch, BF16), (branch, BF16), (branch, BF16),
              (wuq.shape[1], qd_dtype), (kv_lora, F32), (LANES, F32), (wuk.shape[1], BF16), (wuv.shape[1], BF16)]
    out_specs = [pl.BlockSpec((tm, wd), lambda i: (i, 0)) for wd, _ in widths]
    out_shape = [jax.ShapeDtypeStruct((t, wd), dt) for wd, dt in widths]
    if emit_cache:
        heads = branch // HEAD_DIM
        out_specs += [pl.BlockSpec((tm, heads, HEAD_DIM), lambda i: (i, 0, 0))] * 2
        out_shape += [jax.ShapeDtypeStruct((t, heads, HEAD_DIM), F32)] * 2
    scale, scale_spec = _mod_operand(scale, k, tm, rows_per_mod)
    shift, shift_spec = _mod_operand(shift, k, tm, rows_per_mod)
    full = lambda a: pl.BlockSpec(a.shape, lambda i: (0, 0))
    tab_spec = pl.BlockSpec((tm, LANES), lambda i: (i % nseq, 0))
    kern = functools.partial(_in_odd_kernel, branch=branch, q_lora=q_lora, kv_lora=kv_lora,
                             rope_shift=rope_shift, qc_mul=qc_mul, qd_mul=qd_mul)
    return pl.pallas_call(
        kern,
        grid=(t // tm,),
        in_specs=[pl.BlockSpec((tm, k), lambda i: (i, 0)), full(g),
                  scale_spec, shift_spec,
                  full(w), full(qg), full(kvg), full(wuq), full(wuk), full(wuv),
                  tab_spec, tab_spec, tab_spec],
        out_specs=out_specs,
        out_shape=out_shape,
        compiler_params=_params(("arbitrary",)),
        name="in_proj_odd",
    )(x, g, scale, shift, w, qg, kvg, wuq, wuk, wuv, *tables)


def _axial_angles(seq, dims):
    half = dims // 2
    inv = ROPE_THETA ** (-jnp.arange(0, half, 2, dtype=F32) / half)
    t = jnp.arange(seq)
    ang_r = (t // GRID_W).astype(F32)[:, None] * inv
    ang_c = (t % GRID_W).astype(F32)[:, None] * inv
    return ang_r, ang_c


def _axial_tables_lanes(seq, dims, lane0):
    ang_r, ang_c = _axial_angles(seq, dims)
    cos = jnp.concatenate([jnp.cos(ang_r)] * 2 + [jnp.cos(ang_c)] * 2, axis=-1)
    sin = jnp.concatenate([jnp.sin(ang_r)] * 2 + [jnp.sin(ang_c)] * 2, axis=-1)
    first = (jnp.arange(dims) % (dims // 2)) < dims // 4
    sa = jnp.where(first, -sin, 0.0)
    sb = jnp.where(first, 0.0, sin)
    pad_l, pad_r = lane0, LANES - lane0 - dims
    return (jnp.pad(cos, ((0, 0), (pad_l, pad_r)), constant_values=1.0),
            jnp.pad(sa, ((0, 0), (pad_l, pad_r))), jnp.pad(sb, ((0, 0), (pad_l, pad_r))))


def _axial_tables_rows(seq, dims):
    ang_r, ang_c = _axial_angles(seq, dims)
    cos = jnp.concatenate([jnp.cos(ang_r)] * 2 + [jnp.cos(ang_c)] * 2, axis=-1).T
    sin = jnp.concatenate([-jnp.sin(ang_r), jnp.sin(ang_r), -jnp.sin(ang_c), jnp.sin(ang_c)], axis=-1).T
    return cos, sin


def _rms_rows(x, gain):
    ms = jnp.mean(x * x, axis=0, keepdims=True)
    return x * lax.rsqrt(ms + EPS) * gain


def _rope_rows(x, cos, sin):
    q = x.shape[0] // 4
    partner = jnp.concatenate([x[q:2 * q], x[0:q], x[3 * q:4 * q], x[2 * q:3 * q]], axis=0)
    return x * cos + partner * sin


def _attend_t(streams, k_refs, vt_ref, s_ref, p_ref, tk):
    chunks = []
    off = 0
    for k_ref in k_refs:
        n = k_ref.shape[0]
        assert n % tk == 0
        for c in range(n // tk):
            k_fn = lambda i, k_ref=k_ref, r=c * tk: k_ref[r:r + tk, streams[i][1]].astype(BF16)
            vt_fn = lambda i, col=off + c * tk: vt_ref[i, :, col:col + tk]
            chunks.append((k_fn, vt_fn, None))
        off += n
    return _pipeline([q_t for q_t, _ in streams], chunks, s_ref, p_ref)


def _pipeline(q_ts, chunks, s_ref, p_ref):
    nc = len(chunks)
    ns = len(q_ts)
    mq = q_ts[0].shape[1]
    m = [jnp.full((1, mq), -jnp.inf, F32) for _ in range(ns)]
    acc = [jnp.zeros((VT_ROWS, mq), F32) for _ in range(ns)]
    m_chunk = [None] * ns
    alpha = [[None] * ns for _ in range(2)]
    for t in range(nc + 2):
        if t < nc:
            k_fn, _, bias_fn = chunks[t]
            for i, q_t in enumerate(q_ts):
                s = jnp.dot(k_fn(i), q_t, preferred_element_type=F32)
                if bias_fn is not None:
                    s = s + bias_fn(i)
                m_chunk[i] = jnp.max(s, axis=0, keepdims=True)
                s_ref[(t % 2) * ns + i] = s
        if 0 <= t - 2 < nc:
            for i in range(ns):
                pv = jnp.dot(chunks[t - 2][1](i), p_ref[(t % 2) * ns + i], preferred_element_type=F32)
                acc[i] = alpha[t % 2][i] * acc[i] + pv
        if 0 <= t - 1 < nc:
            for i in range(ns):
                m_new = jnp.maximum(m[i], m_prev_chunk[i])
                alpha[(t - 1) % 2][i] = jnp.exp2(m[i] - m_new)
                p = jnp.exp2(s_ref[((t - 1) % 2) * ns + i] - m_new)
                p_ref[((t - 1) % 2) * ns + i] = p.astype(BF16)
                m[i] = m_new
        m_prev_chunk = list(m_chunk)
    return [acc[i][:HEAD_DIM] / acc[i][HEAD_DIM:HEAD_DIM + 1] for i in range(ns)]


VT_ROWS = HEAD_DIM + 16

def _vt_tiles(blk):
    n = blk.shape[0]
    v_t = blk.astype(F32).T
    tail = (lax.broadcasted_iota(jnp.int32, (VT_ROWS - HEAD_DIM, n), 0) == 0).astype(F32)
    return [jnp.concatenate([v_t[h * HEAD_DIM:(h + 1) * HEAD_DIM], tail], axis=0).astype(BF16)
            for h in range(2)]


def _fill_vt(vt_ref, v_refs, chunk=512):
    off = 0
    for v_ref in v_refs:
        n = v_ref.shape[0]
        step = min(chunk, n)
        for c in range(n // step):
            for pp in range(v_ref.shape[1] // LANES):
                tiles = _vt_tiles(v_ref[c * step:(c + 1) * step, pp * LANES:(pp + 1) * LANES])
                for h in range(2):
                    vt_ref[2 * pp + h, :, off + c * step:off + (c + 1) * step] = tiles[h]
        off += n


def _pipe_scratch(n_streams, tk, mq):
    return [pltpu.VMEM((2 * n_streams, tk, mq), F32), pltpu.VMEM((2 * n_streams, tk, mq), BF16)]


def _attn_scratch(n_total, n_streams, tk, mq):
    return [pltpu.VMEM((n_streams, VT_ROWS, n_total), BF16)] + _pipe_scratch(n_streams, tk, mq)


def _store_heads(ref, rows, x):
    for h in range(ref.shape[1]):
        ref[rows, h, :] = x[:, h * HEAD_DIM:(h + 1) * HEAD_DIM].astype(ref.dtype)


def _gqa_kernel(*refs, n_src, tk, rope, emit_k, q_mul):
    q_ref = refs[0]
    k_refs = [refs[1 + 2 * i] for i in range(n_src)]
    v_refs = [refs[2 + 2 * i] for i in range(n_src)]
    pos = 1 + 2 * n_src
    qg_ref, kg_ref = refs[pos], refs[pos + 1]
    pos += 2
    if rope:
        cosq_ref, sinq_ref, cosk_ref, sink_ref = refs[pos:pos + 4]
        pos += 4
    o_ref = refs[pos]
    pos += 1
    if emit_k:
        kn_ref, vn_ref = refs[pos:pos + 2]
        pos += 2
    kp_ref, vt_ref, s_ref, p_ref = refs[pos:pos + 4]
    tq = q_ref.shape[0]

    @pl.when(pl.program_id(1) == 0)
    def _():
        _fill_vt(vt_ref, v_refs)
        raw = k_refs[-1]
        for c in range(raw.shape[0] // tk):
            rows = slice(c * tk, (c + 1) * tk)
            k_t = raw[rows, :].astype(F32).T
            halves = []
            for hh in range(2):
                x = _rms_rows(k_t[hh * HEAD_DIM:(hh + 1) * HEAD_DIM], kg_ref[...])
                if rope:
                    x = _rope_rows(x, cosk_ref[:, rows], sink_ref[:, rows])
                halves.append(x)
            kn = jnp.concatenate(halves, axis=0).T
            kp_ref[rows, :] = kn.astype(kp_ref.dtype)
            if emit_k:
                _store_heads(kn_ref, rows, kn)
                _store_heads(vn_ref, rows, v_refs[-1][rows, :])

    zeros = jnp.zeros((HEAD_DIM, tq), F32)
    parts = []
    for j in range(4):
        q_t = q_ref[:, j * LANES:(j + 1) * LANES].astype(F32).T
        for hh in range(2):
            x = _rms_rows(q_t[hh * HEAD_DIM:(hh + 1) * HEAD_DIM], qg_ref[...])
            if rope:
                x = _rope_rows(x, cosq_ref[...], sinq_ref[...])
            parts.append(x * q_mul)
    streams = []
    for g in range(2):
        cols = [jnp.concatenate([parts[4 * g + n], zeros] if g == 0 else [zeros, parts[4 * g + n]], axis=0)
                for n in range(4)]
        streams.append((jnp.concatenate(cols, axis=1).astype(BF16), slice(None)))
    outs = _attend_t(streams, k_refs[:-1] + [kp_ref], vt_ref, s_ref, p_ref, tk)
    for g in range(2):
        o_t = outs[g]
        for idx, j in enumerate((2 * g, 2 * g + 1)):
            blk_t = jnp.concatenate([o_t[:, (2 * idx) * tq:(2 * idx + 1) * tq],
                                     o_t[:, (2 * idx + 1) * tq:(2 * idx + 2) * tq]], axis=0)
            o_ref[:, j * LANES:(j + 1) * LANES] = blk_t.T.astype(o_ref.dtype)


def _gqa_attention(q, kv, q_gain, k_gain, tables, batch, tq, emit_k, tk=128):
    t, w = q.shape
    s = t // batch
    tq = min(tq, s)
    nq = s // tq
    tk = min([tk] + [k.shape[1] for k, _ in kv])
    in_specs = [pl.BlockSpec((tq, w), lambda b, i: (b * nq + i, 0))]
    args = [q]
    n_total = 0
    for k, v in kv:
        n = k.shape[1]
        n_total += n
        spec = pl.BlockSpec((None, n, LANES), lambda b, i: (b, 0, 0))
        in_specs += [spec, spec]
        args += [k, v]
    n_raw = kv[-1][0].shape[1]
    const = lambda a: pl.BlockSpec(a.shape, lambda b, i: (0, 0))
    qg = jnp.broadcast_to(q_gain[:, None], (HEAD_DIM, tq))
    kg = jnp.broadcast_to(k_gain[:, None], (HEAD_DIM, tk))
    in_specs += [const(qg), const(kg)]
    args += [qg, kg]
    if tables is not None:
        cos, sin = tables
        blk = pl.BlockSpec((HEAD_DIM, tq), lambda b, i: (0, i))
        in_specs += [blk, blk, const(cos), const(sin)]
        args += [cos, sin, cos, sin]
    out_specs = [pl.BlockSpec((tq, w), lambda b, i: (b * nq + i, 0))]
    out_shape = [jax.ShapeDtypeStruct((t, w), BF16)]
    if emit_k:
        out_specs += [pl.BlockSpec((n_raw, 2, HEAD_DIM), lambda b, i: (b, 0, 0))] * 2
        out_shape += [jax.ShapeDtypeStruct((batch * n_raw, 2, HEAD_DIM), F32)] * 2
    res = pl.pallas_call(
        functools.partial(_gqa_kernel, n_src=len(kv), tk=tk, rope=tables is not None, emit_k=emit_k,
                          q_mul=LOG2E * HEAD_DIM ** -0.5),
        grid=(batch, nq),
        in_specs=in_specs,
        out_specs=out_specs,
        out_shape=out_shape,
        scratch_shapes=[pltpu.VMEM((n_raw, LANES), BF16)] + _attn_scratch(n_total, 2, tk, 4 * tq),
        compiler_params=_params(("arbitrary", "arbitrary")),
        name="gqa_attention",
    )(*args)
    return res if emit_k else res[0]


def _pair_kernel(*refs, n_src, tk, dk, rope):
    q_ref = refs[0]
    k_refs = [refs[1 + 2 * i] for i in range(n_src)]
    v_refs = [refs[2 + 2 * i] for i in range(n_src)]
    pos = 1 + 2 * n_src
    if rope:
        cos_ref, sin_ref = refs[pos:pos + 2]
        pos += 2
    o_ref, vt_ref, s_ref, p_ref = refs[pos:pos + 4]
    tq = q_ref.shape[0]

    @pl.when(pl.program_id(2) == 0)
    def _():
        _fill_vt(vt_ref, v_refs)

    top = lax.broadcasted_iota(jnp.int32, (LANES, tq), 0) < HEAD_DIM
    pps = o_ref.shape[1] // LANES
    streams = []
    for pp in range(pps):
        for h in range(2):
            if dk == HEAD_DIM:
                lanes = slice(pp * LANES, (pp + 1) * LANES)
                both = q_ref[:, lanes].astype(F32).T
                q_t = (jnp.where(top, both, 0.0) if h == 0 else jnp.where(top, 0.0, both)).astype(BF16)
                streams.append((q_t, lanes))
            else:
                lanes = slice((2 * pp + h) * LANES, (2 * pp + h + 1) * LANES)
                q_t = q_ref[:, lanes].astype(F32).T
                if rope:
                    n_pe = cos_ref.shape[0]
                    pe = _rope_rows(q_t[HEAD_DIM:HEAD_DIM + n_pe], cos_ref[...], sin_ref[...])
                    q_t = jnp.concatenate([q_t[:HEAD_DIM], pe, q_t[HEAD_DIM + n_pe:]], axis=0)
                streams.append((q_t.astype(BF16), lanes))
    outs = _attend_t(streams, k_refs, vt_ref, s_ref, p_ref, tk)
    for pp in range(pps):
        o_ref[:, pp * LANES:(pp + 1) * LANES] = (
            jnp.concatenate(outs[2 * pp:2 * pp + 2], axis=0).T.astype(o_ref.dtype))


def _pair_attention(q, kv, batch, dk, tq, tables=None, tk=128, pps=1):
    t, w = q.shape
    qw = (LANES if dk == HEAD_DIM else 2 * LANES) * pps
    vw = LANES * pps
    groups = w // qw
    s = t // batch
    tq = min(tq, s)
    nq = s // tq
    tk = min([tk] + [k.shape[1] for k, _ in kv])
    in_specs = [pl.BlockSpec((tq, qw), lambda b, p, i: (b * nq + i, p))]
    args = [q]
    n_total = 0
    for k, v in kv:
        n = k.shape[1]
        n_total += n
        in_specs += [pl.BlockSpec((None, n, qw), lambda b, p, i: (b, 0, p)),
                     pl.BlockSpec((None, n, vw), lambda b, p, i: (b, 0, p))]
        args += [k, v]
    if tables is not None:
        blk = pl.BlockSpec((tables[0].shape[0], tq), lambda b, p, i: (0, i))
        in_specs += [blk, blk]
        args += list(tables)
    return pl.pallas_call(
        functools.partial(_pair_kernel, n_src=len(kv), tk=tk, dk=dk, rope=tables is not None),
        grid=(batch, groups, nq),
        in_specs=in_specs,
        out_specs=pl.BlockSpec((tq, vw), lambda b, p, i: (b * nq + i, p)),
        out_shape=jax.ShapeDtypeStruct((t, groups * vw), BF16),
        scratch_shapes=_attn_scratch(n_total, 2 * pps, tk, tq),
        compiler_params=_params(("arbitrary", "arbitrary", "arbitrary")),
        name="pair_attention",
    )(*args)


NB_QROWS = 4
NB_KROWS = 12
NB_TQ = NB_QROWS * GRID_W
NB_CROWS = 4
NB_TK = NB_CROWS * GRID_W
NB_INVALID = 2 * NA_ROWS - 1


def _nb_kernel(q_ref, kc_ref, vc_ref, k_ref, v_ref, tab_ref, o_ref, vtc_ref, vt_ref, bias_ref, s_ref, p_ref):
    i = pl.program_id(2)
    rows = k_ref.shape[0] // GRID_W
    half = NA_ROWS // 2
    a = i * NB_QROWS
    base_blk = jnp.clip(i - half // NB_QROWS, 0, (rows - NB_KROWS) // NB_QROWS)
    base = base_blk * NB_QROWS
    base_chunk = base_blk * (NB_QROWS // NB_CROWS)
    tq = q_ref.shape[0]

    @pl.when(i == 0)
    def _():
        for ref, src in ((vtc_ref, vc_ref), (vt_ref, v_ref)):
            for c in range(ref.shape[1]):
                for pp in range(src.shape[1] // LANES):
                    tiles = _vt_tiles(src[c * NB_TK:(c + 1) * NB_TK, pp * LANES:(pp + 1) * LANES])
                    for h in range(2):
                        ref[2 * pp + h, c] = tiles[h]

    def unclipped(first_row):
        return jnp.logical_and(first_row - half >= 0, first_row + NB_QROWS - 1 - half <= rows - NA_ROWS)

    @pl.when(jnp.logical_not(jnp.logical_and(unclipped(a), unclipped(a - NB_QROWS))))
    def _():
        lo_tile = lax.broadcasted_iota(jnp.int32, (GRID_W, LANES), 1) < HEAD_DIM
        for h in range(bias_ref.shape[0]):
            for j in range(NB_KROWS):
                kr = base + j
                for ip in range(NB_QROWS // 2):
                    halves = []
                    for qi in (2 * ip, 2 * ip + 1):
                        qr = a + qi
                        r0 = jnp.clip(qr - half, 0, rows - NA_ROWS)
                        valid = jnp.logical_and(kr >= r0, kr < r0 + NA_ROWS)
                        halves.append(tab_ref[h, jnp.where(valid, kr - qr + NA_ROWS - 1, NB_INVALID)])
                    bias_ref[h, j * GRID_W:(j + 1) * GRID_W, ip * LANES:(ip + 1) * LANES] = (
                        jnp.where(lo_tile, halves[0], halves[1]))

    top = lax.broadcasted_iota(jnp.int32, (LANES, tq), 0) < HEAD_DIM
    pps = q_ref.shape[1] // LANES
    q_ts = []
    for pp in range(pps):
        both = q_ref[:, pp * LANES:(pp + 1) * LANES].astype(F32).T
        q_ts += [jnp.where(top, both, 0.0).astype(BF16), jnp.where(top, 0.0, both).astype(BF16)]
    pair_lanes = lambda i_: slice((i_ // 2) * LANES, (i_ // 2 + 1) * LANES)
    chunks = []
    for c in range(vtc_ref.shape[1]):
        chunks.append((lambda i_, c=c: kc_ref[c * NB_TK:(c + 1) * NB_TK, pair_lanes(i_)].astype(BF16),
                       lambda i_, c=c: vtc_ref[i_, c], None))
    for c in range(NB_KROWS // NB_CROWS):
        start = pl.multiple_of((base + c * NB_CROWS) * GRID_W, NB_TK)
        chunks.append((lambda i_, start=start: k_ref[pl.ds(start, NB_TK), pair_lanes(i_)].astype(BF16),
                       lambda i_, c=c: vt_ref[i_, base_chunk + c],
                       lambda i_, c=c: bias_ref[i_, c * NB_TK:(c + 1) * NB_TK, :]))
    outs = _pipeline(q_ts, chunks, s_ref, p_ref)
    for pp in range(pps):
        o_ref[:, pp * LANES:(pp + 1) * LANES] = (
            jnp.concatenate(outs[2 * pp:2 * pp + 2], axis=0).T.astype(o_ref.dtype))


def _nb_attention(q, k, v, kctx, vctx, table, batch, pps=4):
    t, w = q.shape
    s = t // batch
    pw = LANES * pps
    groups = w // pw
    heads = 2 * pps
    tq = NB_TQ
    nq = s // tq
    l = kctx.shape[1]
    assert l % NB_TK == 0 and (s // GRID_W) % NB_QROWS == 0
    return pl.pallas_call(
        _nb_kernel,
        grid=(batch, groups, nq),
        in_specs=[
            pl.BlockSpec((tq, pw), lambda b, p, i: (b * nq + i, p)),
            pl.BlockSpec((None, l, pw), lambda b, p, i: (b, 0, p)),
            pl.BlockSpec((None, l, pw), lambda b, p, i: (b, 0, p)),
            pl.BlockSpec((None, s, pw), lambda b, p, i: (b, 0, p)),
            pl.BlockSpec((None, s, pw), lambda b, p, i: (b, 0, p)),
            pl.BlockSpec((heads, NB_INVALID + 1, GRID_W, LANES), lambda b, p, i: (p, 0, 0, 0)),
        ],
        out_specs=pl.BlockSpec((tq, pw), lambda b, p, i: (b * nq + i, p)),
        out_shape=jax.ShapeDtypeStruct((t, w), BF16),
        scratch_shapes=[pltpu.VMEM((heads, l // NB_TK, VT_ROWS, NB_TK), BF16),
                        pltpu.VMEM((heads, s // NB_TK, VT_ROWS, NB_TK), BF16),
                        pltpu.VMEM((heads, NB_KROWS * GRID_W, tq), F32),
                        *_pipe_scratch(heads, NB_TK, tq)],
        compiler_params=_params(("arbitrary", "arbitrary", "arbitrary")),
        name="nb_attention",
    )(q, kctx, vctx, k, v, table)


def _nb_bias_table(rpb):
    h, n_dr, n_dc = rpb.shape
    qc = jnp.arange(GRID_W)[None, :]
    kc = jnp.arange(GRID_W)[:, None]
    c0 = jnp.clip(qc - NA_COLS // 2, 0, GRID_W - NA_COLS)
    inwin = jnp.logical_and(kc >= c0, kc < c0 + NA_COLS)
    period = 2 * GRID_W - 1
    rev = rpb.astype(F32)[..., ::-1]
    seq = jnp.concatenate([rev[..., NA_COLS - 1:], jnp.zeros((h, n_dr, period - n_dc), F32),
                           rev[..., :NA_COLS - 1]], axis=-1)
    rel = jnp.tile(seq, (1, 1, GRID_W))[..., :GRID_W * (period - 1)]
    rel = rel.reshape(h, n_dr, GRID_W, period - 1)[..., :GRID_W]
    t = jnp.where(inwin[None, None], rel * LOG2E, NEG)
    t = jnp.concatenate([t, jnp.full((h, 1, GRID_W, GRID_W), NEG, F32)], axis=1)
    return jnp.concatenate([t, t], axis=-1)


PAD = 8


def _window_sum(pad_ref, tmp_ref, lanes, win, tm):
    def rows(lo, n):
        return pad_ref[PAD + lo:PAD + lo + n, lanes]

    if win <= 4:
        acc = rows(-(win // 2), tm)
        for d in range(-(win // 2) + 1, win // 2):
            acc = acc + rows(d, tm)
        return acc
    ext = win // 2 - 2
    n = tm + 2 * ext
    cur = rows(-ext - 2, n)
    for d in (-1, 0, 1):
        cur = cur + rows(-ext + d, n)
    width = 4
    while width < win:
        tmp_ref[0:n, :] = cur
        h = width // 2
        prev_ext, ext = ext, ext - h
        n = tm + 2 * ext
        lo = prev_ext - ext
        cur = tmp_ref[lo - h:lo - h + n, :] + tmp_ref[lo + h:lo + h + n, :]
        width *= 2
    return cur


def _project_out(x_ref, ta, tb, w_ref, g_ref, gate_ref, o_ref):
    half = ta.shape[1]
    y = (jnp.dot(ta, w_ref[0:half, :], preferred_element_type=F32)
         + jnp.dot(tb, w_ref[half:2 * half, :], preferred_element_type=F32))
    o_ref[...] = x_ref[...] + gate_ref[...] * _rms_lanes(y, g_ref[...])


def _out_kernel(x_ref, a_ref, ga_ref, b_ref, gb_ref, w_ref, g_ref, gate_ref, o_ref):
    ta = (a_ref[...].astype(F32) * _silu(ga_ref[...].astype(F32))).astype(BF16)
    tb = (b_ref[...].astype(F32) * _silu(gb_ref[...].astype(F32))).astype(BF16)
    _project_out(x_ref, ta, tb, w_ref, g_ref, gate_ref, o_ref)


def _out_even_kernel(x_ref, a_ref, ga_ref, u_ref, up_ref, un_ref, gb_ref, map_ref, sc_ref, w_ref, g_ref,
                     gate_ref, o_ref, pad_ref, tmp_ref, *, nseq, seq):
    tm = a_ref.shape[0]
    pos = pl.program_id(0) % nseq
    u = u_ref[...]
    pad_ref[PAD:PAD + tm, :] = u
    pad_ref[0:PAD, :] = jnp.where(pos == 0, 0.0, up_ref[...])
    pad_ref[PAD + tm:PAD + tm + PAD, :] = jnp.where(pos == nseq - 1, 0.0, un_ref[...])
    t = pos * tm + lax.broadcasted_iota(jnp.int32, (tm, LANES), 0)
    ys = []
    for g, win in enumerate(POOL_WINDOWS):
        lanes = slice(g * LANES, (g + 1) * LANES)
        acc = _window_sum(pad_ref, tmp_ref, lanes, win, tm)
        cnt = jnp.minimum(t + win // 2, seq) - jnp.maximum(t - win // 2, 0)
        diff = acc / cnt.astype(F32) - u[:, lanes]
        ys.append(jnp.dot(diff.astype(BF16), map_ref[g].astype(BF16), preferred_element_type=F32))
    o_b = jnp.concatenate(ys, axis=1) * sc_ref[...]
    ta = (a_ref[...].astype(F32) * _silu(ga_ref[...].astype(F32))).astype(BF16)
    tb = (o_b * _silu(gb_ref[...].astype(F32))).astype(BF16)
    _project_out(x_ref, ta, tb, w_ref, g_ref, gate_ref, o_ref)


def _out_even(x, a, ga, u, gb, b_map, b_scale, w, g_post, gate, rows_per_mod, seq, tm=512):
    t, d = x.shape
    half = a.shape[1]
    tm = min(tm, seq)
    nseq = seq // tm
    br = pl.BlockSpec((tm, half), lambda i: (i, 0))
    halo_blocks = tm // PAD
    prev = pl.BlockSpec((PAD, half), lambda i: (jnp.maximum(i * halo_blocks - 1, 0), 0))
    nxt = pl.BlockSpec((PAD, half), lambda i: (jnp.minimum((i + 1) * halo_blocks, t // PAD - 1), 0))
    full = lambda arr: pl.BlockSpec(arr.shape, lambda i: (0,) * arr.ndim)
    sc = b_scale.reshape(1, half)
    gate, gate_spec = _mod_operand(gate, d, tm, rows_per_mod)
    return pl.pallas_call(
        functools.partial(_out_even_kernel, nseq=nseq, seq=seq),
        grid=(t // tm,),
        in_specs=[
            pl.BlockSpec((tm, d), lambda i: (i, 0)),
            br, br, br, prev, nxt, br, full(b_map), full(sc), full(w), full(g_post), gate_spec,
        ],
        out_specs=pl.BlockSpec((tm, d), lambda i: (i, 0)),
        out_shape=jax.ShapeDtypeStruct((t, d), F32),
        scratch_shapes=[pltpu.VMEM((tm + 2 * PAD, half), F32), pltpu.VMEM((tm + 2 * PAD, LANES), F32)],
        compiler_params=_params(("arbitrary",)),
        name="out_proj_even",
    )(x, a, ga, u, u, u, gb, b_map, sc, w, g_post, gate)


def _out_proj(x, a, ga, b, gb, w, g_post, gate, rows_per_mod, tm=512):
    t, d = x.shape
    half = a.shape[1]
    tm = min(tm, t)
    br = pl.BlockSpec((tm, half), lambda i: (i, 0))
    gate, gate_spec = _mod_operand(gate, d, tm, rows_per_mod)
    return pl.pallas_call(
        _out_kernel,
        grid=(t // tm,),
        in_specs=[
            pl.BlockSpec((tm, d), lambda i: (i, 0)),
            br, br, br, br,
            pl.BlockSpec((2 * half, d), lambda i: (0, 0)),
            pl.BlockSpec((1, d), lambda i: (0, 0)),
            gate_spec,
        ],
        out_specs=pl.BlockSpec((tm, d), lambda i: (i, 0)),
        out_shape=jax.ShapeDtypeStruct((t, d), F32),
        compiler_params=_params(("arbitrary",)),
        name="out_proj",
    )(x, a, ga, b, gb, w, g_post, gate)


def _kvup_kernel(c_ref, pe_ref, wk_ref, wv_ref, k_ref, v_ref):
    cb = c_ref[...].astype(BF16)
    heads = k_ref.shape[1] // LANES
    k = jnp.dot(cb, wk_ref[...], preferred_element_type=F32)
    k = k + jnp.concatenate([pe_ref[...]] * heads, axis=1)
    k_ref[...] = k.astype(k_ref.dtype)
    v_ref[...] = jnp.dot(cb, wv_ref[...], preferred_element_type=F32).astype(v_ref.dtype)


def _kvup(ckv, pe128, wk, wv, tm=512):
    t, r = ckv.shape
    tm = min(tm, t)
    nk, nv = wk.shape[1], wv.shape[1]
    row = lambda width: pl.BlockSpec((tm, width), lambda i: (i, 0))
    full = lambda a: pl.BlockSpec(a.shape, lambda i: (0, 0))
    return pl.pallas_call(
        _kvup_kernel,
        grid=(t // tm,),
        in_specs=[row(r), row(LANES), full(wk), full(wv)],
        out_specs=[row(nk), row(nv)],
        out_shape=[jax.ShapeDtypeStruct((t, nk), BF16), jax.ShapeDtypeStruct((t, nv), BF16)],
        compiler_params=_params(("arbitrary",)),
        name="kv_up",
    )(ckv, pe128, wk, wv)


def kernel(x_prompt, x_sample, cache_a_k, cache_a_v, cache_c_k, cache_c_v, cache_d_ckv, cache_d_kpe,
           c, c_ctx, w_mod, b_mod, g_pre, g_post, w_in_e, a_q_norm, a_k_norm, b_map, b_scale, w_out_e,
           w_in_o, c_rpb, d_q_norm, d_w_uq, d_kv_norm, d_w_ukv, w_out_o):
    bp, sp, d = x_prompt.shape
    bs, ss, _ = x_sample.shape
    past = cache_a_k.shape[2]
    depth = w_mod.shape[0]
    tp, ts = bp * sp, bs * ss
    branch = d // 2
    d_heads = 8
    d_nope, d_rope, d_v = 64, 32, 64
    dqk = d_nope + d_rope
    q_lora, kv_lora = d_q_norm.shape[1], d_kv_norm.shape[1]
    kv_w = a_k_norm.shape[1] * 2

    xp = x_prompt.reshape(tp, d)
    xs = x_sample.reshape(ts, d)

    cond = jnp.concatenate([c_ctx[None, :], c, jnp.zeros((8 - 1 - bs, d), F32)], axis=0)
    mods = _mod_call(cond, w_mod, b_mod)

    mod_table = mods.reshape(depth * cond.shape[0] * MOD_PARTS, 1, d)

    def mod_rows(layer, part, lo_row):
        return mod_table, layer * cond.shape[0] + lo_row, part

    rope_a = _axial_tables_rows(ss, HEAD_DIM)
    rope_d = _axial_tables_rows(ss, d_rope)
    rope_d_lanes = _axial_tables_lanes(ss, d_rope, d_nope)
    no_rope_p = tuple(jnp.zeros((sp, LANES), F32) for _ in range(3))

    outs = {name: [] for name in ("a_k", "a_v", "c_k", "c_v", "d_ckv", "d_kpe")}
    for layer in range(depth):
        j = layer // 2
        g_in = g_pre[layer].reshape(1, d)
        g_out = g_post[layer].reshape(1, d)
        paths = (("p", xp, bp, sp, 0, 1, tp), ("s", xs, bs, ss, 1, bs, ss))
        if layer % 2 == 0:
            w_in = w_in_e[j].astype(BF16)
            w_out = w_out_e[j].astype(BF16)
            o0 = branch
            new_x = []
            for name, x, nb, seq, row0, nrows, rpm in paths:
                shift, scale, gate = (mod_rows(layer, part, row0) for part in range(MOD_PARTS))
                v_dt = F32 if name == "p" else BF16
                segs = [(0, branch, F32), (o0, kv_w, F32), (o0 + kv_w, kv_w, v_dt),
                        (o0 + 2 * kv_w, branch, BF16), (o0 + 2 * kv_w + branch, branch, F32),
                        (o0 + 2 * kv_w + 2 * branch, branch, BF16)]
                q, k, v, ga, ub, gb = _norm_mm(x, g_in, scale, shift, w_in, segs, rpm, name="in_proj_even")
                k3, v3 = k.reshape(nb, seq, kv_w), v.reshape(nb, seq, kv_w)
                if name == "p":
                    o_a, kn, vn = _gqa_attention(q, [(k3, v3)], a_q_norm[j], a_k_norm[j], None, nb, tq=256,
                                                 emit_k=True)
                    outs["a_k"].append(kn)
                    outs["a_v"].append(vn)
                else:
                    kv = [(cache_a_k[:, j].reshape(nb, past, kv_w), cache_a_v[:, j].reshape(nb, past, kv_w)),
                          (k3, v3)]
                    o_a = _gqa_attention(q, kv, a_q_norm[j], a_k_norm[j], rope_a, nb, tq=256, emit_k=False)
                new_x.append(_out_even(x, o_a, ga, ub, gb, b_map[j], b_scale[j], w_out, g_out, gate, rpm, seq))
            xp, xs = new_x
        else:
            wi = w_in_o[j]
            off_kpe = 4 * branch + q_lora + kv_lora
            w_in = jnp.concatenate([wi[:, :off_kpe], jnp.zeros((d, d_nope), F32),
                                    wi[:, off_kpe:off_kpe + d_rope],
                                    jnp.zeros((d, LANES - dqk), F32),
                                    wi[:, off_kpe + d_rope:]], axis=1).astype(BF16)
            w_out = w_out_o[j].astype(BF16)
            w_uq = jnp.pad(d_w_uq[j].reshape(q_lora, d_heads, dqk),
                           ((0, 0), (0, 0), (0, LANES - dqk))).reshape(q_lora, d_heads * LANES).astype(BF16)
            ukv = d_w_ukv[j].reshape(kv_lora, d_heads, d_nope + d_v)
            w_uk = jnp.pad(ukv[:, :, :d_nope], ((0, 0), (0, 0), (0, LANES - d_nope))
                           ).reshape(kv_lora, d_heads * LANES).astype(BF16)
            w_uv = ukv[:, :, d_nope:].reshape(kv_lora, d_heads * d_v).astype(BF16)
            qn_g = d_q_norm[j].reshape(1, q_lora)
            kvn_g = d_kv_norm[j].reshape(1, kv_lora)
            table = _nb_bias_table(c_rpb[j])
            new_x = []
            for name, x, nb, seq, row0, nrows, rpm in paths:
                shift, scale, gate = (mod_rows(layer, part, row0) for part in range(MOD_PARTS))
                latent = name == "s"
                qc, kc, vc, gc, gd, qd, ckv_n, kpe, k_d, v_d, *cache_kv = _in_odd(
                    x, g_in, scale, shift, w_in, qn_g, kvn_g, w_uq, w_uk, w_uv,
                    rope_d_lanes if latent else no_rope_p, rpm, seq, d_rope // 4 if latent else 0,
                    not latent, F32 if latent else BF16, branch,
                    LOG2E * HEAD_DIM ** -0.5, LOG2E * dqk ** -0.5)
                kc3, vc3 = kc.reshape(nb, seq, branch), vc.reshape(nb, seq, branch)
                kd3, vd3 = k_d.reshape(nb, seq, -1), v_d.reshape(nb, seq, -1)
                if not latent:
                    outs["c_k"].append(cache_kv[0])
                    outs["c_v"].append(cache_kv[1])
                    outs["d_ckv"].append(ckv_n)
                    outs["d_kpe"].append(kpe[:, d_nope:dqk])
                    all_pairs = branch // LANES
                    o_c = _pair_attention(qc, [(kc3, vc3)], nb, HEAD_DIM, tq=256, pps=all_pairs)
                    o_d = _pair_attention(qd, [(kd3, vd3)], nb, 2 * HEAD_DIM, tq=256, pps=all_pairs)
                else:
                    o_c = _nb_attention(qc, kc3, vc3, cache_c_k[:, j].reshape(nb, past, branch),
                                        cache_c_v[:, j].reshape(nb, past, branch), table, nb)
                    pe_ctx = jnp.pad(cache_d_kpe[:, j].reshape(nb * past, d_rope),
                                     ((0, 0), (d_nope, LANES - dqk)))
                    k_ctx, v_ctx = _kvup(cache_d_ckv[:, j].reshape(nb * past, kv_lora), pe_ctx, w_uk, w_uv)
                    kv = [(k_ctx.reshape(nb, past, -1), v_ctx.reshape(nb, past, -1)), (kd3, vd3)]
                    o_d = _pair_attention(qd, kv, nb, 2 * HEAD_DIM, tq=512, tables=rope_d, pps=2)
                new_x.append(_out_proj(x, o_c, gc, o_d, gd, w_out, g_out, gate, rpm))
            xp, xs = new_x

    kvh = kv_w // HEAD_DIM
    ch = branch // HEAD_DIM

    def stacked(name, *tail):
        return jnp.stack([a.reshape(bp, sp, *tail) for a in outs[name]], axis=1)

    return (xp.reshape(bp, sp, d), xs.reshape(bs, ss, d),
            stacked("a_k", kvh, HEAD_DIM), stacked("a_v", kvh, HEAD_DIM),
            stacked("c_k", ch, HEAD_DIM), stacked("c_v", ch, HEAD_DIM),
            stacked("d_ckv", kv_lora), stacked("d_kpe", d_rope))
```

```python
import functools

import jax
import jax.numpy as jnp
from jax import lax
from jax.experimental import pallas as pl
from jax.experimental.pallas import tpu as pltpu

F32 = jnp.float32
BF16 = jnp.bfloat16

LANES = 128
GRID_W = 64
HEAD_DIM = 64
NA_ROWS = 8
NA_COLS = 16
POOL_WINDOWS = (2, 4, 8, 16)
ROPE_THETA = 10000.0
EPS = 1e-6
NEG = -1e30
LOG2E = 1.4426950408889634
VMEM_LIMIT = 56 * 1024 * 1024


def _params(sem):
    return pltpu.CompilerParams(dimension_semantics=sem, vmem_limit_bytes=VMEM_LIMIT)


def _silu(x):
    h = 0.5 * x
    return h + h * jnp.tanh(h)


def _rms_lanes(x, gain):
    ms = jnp.mean(x * x, axis=-1, keepdims=True)
    return x * lax.rsqrt(ms + EPS) * gain


MOD_PARTS = 3


def _mod_operand(mod, width, tm, rows_per_mod):
    table, row0, part = mod
    index = lambda i: ((row0 + (i * tm) // rows_per_mod) * MOD_PARTS + part, 0, 0)
    return table, pl.BlockSpec((None, 1, width), index)


def _mod_kernel(c_ref, w_ref, b_ref, o_ref):
    s = _silu(c_ref[...]).astype(BF16)
    o_ref[...] = jnp.dot(s, w_ref[...].astype(BF16), preferred_element_type=F32) + b_ref[...]


def _mod_call(cond, w_mod, b_mod):
    depth, d, n = w_mod.shape
    rows = cond.shape[0]
    tn = 1024
    return pl.pallas_call(
        _mod_kernel,
        grid=(depth, n // tn),
        in_specs=[
            pl.BlockSpec((rows, d), lambda l, j: (0, 0)),
            pl.BlockSpec((None, d, tn), lambda l, j: (l, 0, j)),
            pl.BlockSpec((None, 1, tn), lambda l, j: (l, 0, j)),
        ],
        out_specs=pl.BlockSpec((None, rows, tn), lambda l, j: (l, 0, j)),
        out_shape=jax.ShapeDtypeStruct((depth, rows, n), F32),
        compiler_params=_params(("arbitrary", "arbitrary")),
        name="adaln_mod",
    )(cond, w_mod, b_mod.reshape(depth, 1, n))


def _norm_mm_kernel(x_ref, g_ref, sc_ref, sh_ref, w_ref, *o_refs, segs, do_norm):
    x = x_ref[...].astype(F32)
    if do_norm:
        x = _rms_lanes(x, g_ref[...]) * (1.0 + sc_ref[...]) + sh_ref[...]
    z = jnp.dot(x.astype(BF16), w_ref[...], preferred_element_type=F32)
    for o_ref, (start, width) in zip(o_refs, segs):
        o_ref[...] = z[:, start:start + width].astype(o_ref.dtype)


def _norm_mm(x, g, scale, shift, w, segs, rows_per_mod, do_norm=True, tm=512, name="norm_mm"):
    t, k = x.shape
    n = w.shape[1]
    tm = min(tm, t)
    kern = functools.partial(_norm_mm_kernel, segs=tuple((s, wd) for s, wd, _ in segs), do_norm=do_norm)
    scale, scale_spec = _mod_operand(scale, k, tm, rows_per_mod)
    shift, shift_spec = _mod_operand(shift, k, tm, rows_per_mod)
    return pl.pallas_call(
        kern,
        grid=(t // tm,),
        in_specs=[
            pl.BlockSpec((tm, k), lambda i: (i, 0)),
            pl.BlockSpec((1, k), lambda i: (0, 0)),
            scale_spec,
            shift_spec,
            pl.BlockSpec((k, n), lambda i: (0, 0)),
        ],
        out_specs=[pl.BlockSpec((tm, wd), lambda i: (i, 0)) for _, wd, _ in segs],
        out_shape=[jax.ShapeDtypeStruct((t, wd), dt) for _, wd, dt in segs],
        compiler_params=_params(("arbitrary",)),
        name=name,
    )(x, g, scale, shift, w)


def _in_odd_kernel(x_ref, g_ref, sc_ref, sh_ref, w_ref, qg_ref, kvg_ref, wuq_ref, wuk_ref, wuv_ref,
                   cos_ref, sa_ref, sb_ref,
                   qc_ref, kc_ref, vc_ref, gc_ref, gd_ref, qd_ref, ckv_ref, kpe_ref, kd_ref, vd_ref,
                   *cache_refs, branch, q_lora, kv_lora, rope_shift, qc_mul, qd_mul):
    x = _rms_lanes(x_ref[...], g_ref[...]) * (1.0 + sc_ref[...]) + sh_ref[...]
    z = jnp.dot(x.astype(BF16), w_ref[...], preferred_element_type=F32)
    b = branch
    qc_ref[...] = (z[:, 0:b] * qc_mul).astype(qc_ref.dtype)
    kc_ref[...] = z[:, b:2 * b].astype(kc_ref.dtype)
    vc_ref[...] = z[:, 2 * b:3 * b].astype(vc_ref.dtype)
    gc_ref[...] = z[:, 3 * b:4 * b].astype(gc_ref.dtype)
    if cache_refs:
        _store_heads(cache_refs[0], slice(None), z[:, b:2 * b])
        _store_heads(cache_refs[1], slice(None), z[:, 2 * b:3 * b])
    o = 4 * b
    cq = _rms_lanes(z[:, o:o + q_lora], qg_ref[...])
    qd = jnp.dot(cq.astype(BF16), wuq_ref[...], preferred_element_type=F32)
    qd_ref[...] = (qd * qd_mul).astype(qd_ref.dtype)
    o += q_lora
    ckv = _rms_lanes(z[:, o:o + kv_lora], kvg_ref[...])
    ckv_ref[...] = ckv
    o += kv_lora
    kpe = z[:, o:o + LANES]
    if rope_shift:
        kpe = (kpe * cos_ref[...] + pltpu.roll(kpe, LANES - rope_shift, 1) * sa_ref[...]
               + pltpu.roll(kpe, rope_shift, 1) * sb_ref[...])
    kpe_ref[...] = kpe
    o += LANES
    gd_ref[...] = z[:, o:o + b].astype(gd_ref.dtype)
    cb = ckv.astype(BF16)
    heads = kd_ref.shape[1] // LANES
    kd = jnp.dot(cb, wuk_ref[...], preferred_element_type=F32) + jnp.concatenate([kpe] * heads, axis=1)
    kd_ref[...] = kd.astype(kd_ref.dtype)
    vd_ref[...] = jnp.dot(cb, wuv_ref[...], preferred_element_type=F32).astype(vd_ref.dtype)


def _in_odd(x, g, scale, shift, w, qg, kvg, wuq, wuk, wuv, tables, rows_per_mod, seq, rope_shift,
            emit_cache, qd_dtype, branch, qc_mul, qd_mul, tm=512):
    t, k = x.shape
    tm = min(tm, seq)
    nseq = seq // tm
    q_lora, kv_lora = qg.shape[1], kvg.shape[1]
    widths = [(branch, BF16), (branch, BF16), (branch, BF16), (branch, BF16), (branch, BF16),
              (wuq.shape[1], qd_dtype), (kv_lora, F32), (LANES, F32), (wuk.shape[1], BF16), (wuv.shape[1], BF16)]
    out_specs = [pl.BlockSpec((tm, wd), lambda i: (i, 0)) for wd, _ in widths]
    out_shape = [jax.ShapeDtypeStruct((t, wd), dt) for wd, dt in widths]
    if emit_cache:
        heads = branch // HEAD_DIM
        out_specs += [pl.BlockSpec((tm, heads, HEAD_DIM), lambda i: (i, 0, 0))] * 2
        out_shape += [jax.ShapeDtypeStruct((t, heads, HEAD_DIM), F32)] * 2
    scale, scale_spec = _mod_operand(scale, k, tm, rows_per_mod)
    shift, shift_spec = _mod_operand(shift, k, tm, rows_per_mod)
    full = lambda a: pl.BlockSpec(a.shape, lambda i: (0, 0))
    tab_spec = pl.BlockSpec((tm, LANES), lambda i: (i % nseq, 0))
    kern = functools.partial(_in_odd_kernel, branch=branch, q_lora=q_lora, kv_lora=kv_lora,
                             rope_shift=rope_shift, qc_mul=qc_mul, qd_mul=qd_mul)
    return pl.pallas_call(
        kern,
        grid=(t // tm,),
        in_specs=[pl.BlockSpec((tm, k), lambda i: (i, 0)), full(g),
                  scale_spec, shift_spec,
                  full(w), full(qg), full(kvg), full(wuq), full(wuk), full(wuv),
                  tab_spec, tab_spec, tab_spec],
        out_specs=out_specs,
        out_shape=out_shape,
        compiler_params=_params(("arbitrary",)),
        name="in_proj_odd",
    )(x, g, scale, shift, w, qg, kvg, wuq, wuk, wuv, *tables)


def _axial_angles(seq, dims):
    half = dims // 2
    inv = ROPE_THETA ** (-jnp.arange(0, half, 2, dtype=F32) / half)
    t = jnp.arange(seq)
    ang_r = (t // GRID_W).astype(F32)[:, None] * inv
    ang_c = (t % GRID_W).astype(F32)[:, None] * inv
    return ang_r, ang_c


def _axial_tables_lanes(seq, dims, lane0):
    ang_r, ang_c = _axial_angles(seq, dims)
    cos = jnp.concatenate([jnp.cos(ang_r)] * 2 + [jnp.cos(ang_c)] * 2, axis=-1)
    sin = jnp.concatenate([jnp.sin(ang_r)] * 2 + [jnp.sin(ang_c)] * 2, axis=-1)
    first = (jnp.arange(dims) % (dims // 2)) < dims // 4
    sa = jnp.where(first, -sin, 0.0)
    sb = jnp.where(first, 0.0, sin)
    pad_l, pad_r = lane0, LANES - lane0 - dims
    return (jnp.pad(cos, ((0, 0), (pad_l, pad_r)), constant_values=1.0),
            jnp.pad(sa, ((0, 0), (pad_l, pad_r))), jnp.pad(sb, ((0, 0), (pad_l, pad_r))))


def _axial_tables_rows(seq, dims):
    ang_r, ang_c = _axial_angles(seq, dims)
    cos = jnp.concatenate([jnp.cos(ang_r)] * 2 + [jnp.cos(ang_c)] * 2, axis=-1).T
    sin = jnp.concatenate([-jnp.sin(ang_r), jnp.sin(ang_r), -jnp.sin(ang_c), jnp.sin(ang_c)], axis=-1).T
    return cos, sin


def _rms_rows(x, gain):
    ms = jnp.mean(x * x, axis=0, keepdims=True)
    return x * lax.rsqrt(ms + EPS) * gain


def _rope_rows(x, cos, sin):
    q = x.shape[0] // 4
    partner = jnp.concatenate([x[q:2 * q], x[0:q], x[3 * q:4 * q], x[2 * q:3 * q]], axis=0)
    return x * cos + partner * sin


def _attend_t(streams, k_refs, vt_ref, s_ref, p_ref, tk):
    chunks = []
    off = 0
    for k_ref in k_refs:
        n = k_ref.shape[0]
        assert n % tk == 0
        for c in range(n // tk):
            k_fn = lambda i, k_ref=k_ref, r=c * tk: k_ref[r:r + tk, streams[i][1]].astype(BF16)
            vt_fn = lambda i, col=off + c * tk: vt_ref[i, :, col:col + tk]
            chunks.append((k_fn, vt_fn, None))
        off += n
    return _pipeline([q_t for q_t, _ in streams], chunks, s_ref, p_ref)


def _pipeline(q_ts, chunks, s_ref, p_ref):
    nc = len(chunks)
    ns = len(q_ts)
    mq = q_ts[0].shape[1]
    m = [jnp.full((1, mq), -jnp.inf, F32) for _ in range(ns)]
    acc = [jnp.zeros((VT_ROWS, mq), F32) for _ in range(ns)]
    m_chunk = [None] * ns
    alpha = [[None] * ns for _ in range(2)]
    for t in range(nc + 2):
        if t < nc:
            k_fn, _, bias_fn = chunks[t]
            for i, q_t in enumerate(q_ts):
                s = jnp.dot(k_fn(i), q_t, preferred_element_type=F32)
                if bias_fn is not None:
                    s = s + bias_fn(i)
                m_chunk[i] = jnp.max(s, axis=0, keepdims=True)
                s_ref[(t % 2) * ns + i] = s
        if 0 <= t - 2 < nc:
            for i in range(ns):
                pv = jnp.dot(chunks[t - 2][1](i), p_ref[(t % 2) * ns + i], preferred_element_type=F32)
                acc[i] = alpha[t % 2][i] * acc[i] + pv
        if 0 <= t - 1 < nc:
            for i in range(ns):
                m_new = jnp.maximum(m[i], m_prev_chunk[i])
                alpha[(t - 1) % 2][i] = jnp.exp2(m[i] - m_new)
                p = jnp.exp2(s_ref[((t - 1) % 2) * ns + i] - m_new)
                p_ref[((t - 1) % 2) * ns + i] = p.astype(BF16)
                m[i] = m_new
        m_prev_chunk = list(m_chunk)
    return [acc[i][:HEAD_DIM] / acc[i][HEAD_DIM:HEAD_DIM + 1] for i in range(ns)]


VT_ROWS = HEAD_DIM + 16

def _vt_tiles(blk):
    n = blk.shape[0]
    v_t = blk.astype(F32).T
    tail = (lax.broadcasted_iota(jnp.int32, (VT_ROWS - HEAD_DIM, n), 0) == 0).astype(F32)
    return [jnp.concatenate([v_t[h * HEAD_DIM:(h + 1) * HEAD_DIM], tail], axis=0).astype(BF16)
            for h in range(2)]


def _fill_vt(vt_ref, v_refs, chunk=512):
    off = 0
    for v_ref in v_refs:
        n = v_ref.shape[0]
        step = min(chunk, n)
        for c in range(n // step):
            for pp in range(v_ref.shape[1] // LANES):
                tiles = _vt_tiles(v_ref[c * step:(c + 1) * step, pp * LANES:(pp + 1) * LANES])
                for h in range(2):
                    vt_ref[2 * pp + h, :, off + c * step:off + (c + 1) * step] = tiles[h]
        off += n


def _pipe_scratch(n_streams, tk, mq):
    return [pltpu.VMEM((2 * n_streams, tk, mq), F32), pltpu.VMEM((2 * n_streams, tk, mq), BF16)]


def _attn_scratch(n_total, n_streams, tk, mq):
    return [pltpu.VMEM((n_streams, VT_ROWS, n_total), BF16)] + _pipe_scratch(n_streams, tk, mq)


def _store_heads(ref, rows, x):
    ref[rows] = x.astype(ref.dtype).reshape(x.shape[0], ref.shape[1], HEAD_DIM)


def _gqa_kernel(*refs, n_src, tk, rope, emit_k, cache_src, q_mul):
    q_ref = refs[0]
    k_refs = [refs[1 + 2 * i] for i in range(n_src)]
    v_refs = [refs[2 + 2 * i] for i in range(n_src)]
    pos = 1 + 2 * n_src
    qg_ref, kg_ref = refs[pos], refs[pos + 1]
    pos += 2
    if rope:
        cosq_ref, sinq_ref, cosk_ref, sink_ref = refs[pos:pos + 4]
        pos += 4
    o_ref = refs[pos]
    pos += 1
    if emit_k:
        kn_ref, vn_ref = refs[pos:pos + 2]
        pos += 2
    if cache_src:
        kcs_ref = refs[pos]
        pos += 1
    kp_ref, vt_ref, s_ref, p_ref = refs[pos:pos + 4]
    tq = q_ref.shape[0]

    @pl.when(pl.program_id(1) == 0)
    def _():
        v_srcs = list(v_refs)
        if cache_src:
            kcs_ref[...] = k_refs[0][...].reshape(kcs_ref.shape).astype(kcs_ref.dtype)
            v_srcs[0] = v_refs[0][...].reshape(kcs_ref.shape)
        _fill_vt(vt_ref, v_srcs)
        raw = k_refs[-1]
        for c in range(raw.shape[0] // tk):
            rows = slice(c * tk, (c + 1) * tk)
            k_t = raw[rows, :].astype(F32).T
            halves = []
            for hh in range(2):
                x = _rms_rows(k_t[hh * HEAD_DIM:(hh + 1) * HEAD_DIM], kg_ref[...])
                if rope:
                    x = _rope_rows(x, cosk_ref[:, rows], sink_ref[:, rows])
                halves.append(x)
            kn = jnp.concatenate(halves, axis=0).T
            kp_ref[rows, :] = kn.astype(kp_ref.dtype)
            if emit_k:
                _store_heads(kn_ref, rows, kn)
                _store_heads(vn_ref, rows, v_refs[-1][rows, :])

    zeros = jnp.zeros((HEAD_DIM, tq), F32)
    parts = []
    for j in range(4):
        q_t = q_ref[:, j * LANES:(j + 1) * LANES].astype(F32).T
        for hh in range(2):
            x = _rms_rows(q_t[hh * HEAD_DIM:(hh + 1) * HEAD_DIM], qg_ref[...])
            if rope:
                x = _rope_rows(x, cosq_ref[...], sinq_ref[...])
            parts.append(x * q_mul)
    streams = []
    for g in range(2):
        cols = [jnp.concatenate([parts[4 * g + n], zeros] if g == 0 else [zeros, parts[4 * g + n]], axis=0)
                for n in range(4)]
        streams.append((jnp.concatenate(cols, axis=1).astype(BF16), slice(None)))
    given = [kcs_ref] if cache_src else k_refs[:-1]
    outs = _attend_t(streams, given + [kp_ref], vt_ref, s_ref, p_ref, tk)
    for g in range(2):
        o_t = outs[g]
        for idx, j in enumerate((2 * g, 2 * g + 1)):
            blk_t = jnp.concatenate([o_t[:, (2 * idx) * tq:(2 * idx + 1) * tq],
                                     o_t[:, (2 * idx + 1) * tq:(2 * idx + 2) * tq]], axis=0)
            o_ref[:, j * LANES:(j + 1) * LANES] = blk_t.T.astype(o_ref.dtype)


def _gqa_attention(q, kv, q_gain, k_gain, tables, batch, tq, emit_k, tk=128):
    t, w = q.shape
    s = t // batch
    tq = min(tq, s)
    nq = s // tq
    tk = min([tk] + [k.shape[1] for k, _ in kv])
    in_specs = [pl.BlockSpec((tq, w), lambda b, i: (b * nq + i, 0))]
    args = [q]
    n_total = 0
    cache_rows = 0
    for k, v in kv:
        n = k.shape[1]
        n_total += n
        if k.ndim == 4:
            cache_rows = n
            spec = pl.BlockSpec((None,) + k.shape[1:], lambda b, i: (b, 0, 0, 0))
        else:
            spec = pl.BlockSpec((None, n, LANES), lambda b, i: (b, 0, 0))
        in_specs += [spec, spec]
        args += [k, v]
    assert cache_rows == 0 or (len(kv) == 2 and kv[0][0].ndim == 4)
    n_raw = kv[-1][0].shape[1]
    const = lambda a: pl.BlockSpec(a.shape, lambda b, i: (0, 0))
    qg = jnp.broadcast_to(q_gain[:, None], (HEAD_DIM, tq))
    kg = jnp.broadcast_to(k_gain[:, None], (HEAD_DIM, tk))
    in_specs += [const(qg), const(kg)]
    args += [qg, kg]
    if tables is not None:
        cos, sin = tables
        blk = pl.BlockSpec((HEAD_DIM, tq), lambda b, i: (0, i))
        in_specs += [blk, blk, const(cos), const(sin)]
        args += [cos, sin, cos, sin]
    out_specs = [pl.BlockSpec((tq, w), lambda b, i: (b * nq + i, 0))]
    out_shape = [jax.ShapeDtypeStruct((t, w), BF16)]
    if emit_k:
        out_specs += [pl.BlockSpec((n_raw, 2, HEAD_DIM), lambda b, i: (b, 0, 0))] * 2
        out_shape += [jax.ShapeDtypeStruct((batch * n_raw, 2, HEAD_DIM), F32)] * 2
    res = pl.pallas_call(
        functools.partial(_gqa_kernel, n_src=len(kv), tk=tk, rope=tables is not None, emit_k=emit_k,
                          cache_src=cache_rows > 0, q_mul=LOG2E * HEAD_DIM ** -0.5),
        grid=(batch, nq),
        in_specs=in_specs,
        out_specs=out_specs,
        out_shape=out_shape,
        scratch_shapes=([pltpu.VMEM((cache_rows, LANES), BF16)] if cache_rows else [])
        + [pltpu.VMEM((n_raw, LANES), BF16)] + _attn_scratch(n_total, 2, tk, 4 * tq),
        compiler_params=_params(("arbitrary", "arbitrary")),
        name="gqa_attention",
    )(*args)
    return res if emit_k else res[0]


def _pair_kernel(*refs, n_src, tk, dk, rope):
    q_ref = refs[0]
    k_refs = [refs[1 + 2 * i] for i in range(n_src)]
    v_refs = [refs[2 + 2 * i] for i in range(n_src)]
    pos = 1 + 2 * n_src
    if rope:
        cos_ref, sin_ref = refs[pos:pos + 2]
        pos += 2
    o_ref, vt_ref, s_ref, p_ref = refs[pos:pos + 4]
    tq = q_ref.shape[0]

    @pl.when(pl.program_id(2) == 0)
    def _():
        _fill_vt(vt_ref, v_refs)

    top = lax.broadcasted_iota(jnp.int32, (LANES, tq), 0) < HEAD_DIM
    pps = o_ref.shape[1] // LANES
    streams = []
    for pp in range(pps):
        for h in range(2):
            if dk == HEAD_DIM:
                lanes = slice(pp * LANES, (pp + 1) * LANES)
                both = q_ref[:, lanes].astype(F32).T
                q_t = (jnp.where(top, both, 0.0) if h == 0 else jnp.where(top, 0.0, both)).astype(BF16)
                streams.append((q_t, lanes))
            else:
                lanes = slice((2 * pp + h) * LANES, (2 * pp + h + 1) * LANES)
                q_t = q_ref[:, lanes].astype(F32).T
                if rope:
                    n_pe = cos_ref.shape[0]
                    pe = _rope_rows(q_t[HEAD_DIM:HEAD_DIM + n_pe], cos_ref[...], sin_ref[...])
                    q_t = jnp.concatenate([q_t[:HEAD_DIM], pe, q_t[HEAD_DIM + n_pe:]], axis=0)
                streams.append((q_t.astype(BF16), lanes))
    outs = _attend_t(streams, k_refs, vt_ref, s_ref, p_ref, tk)
    for pp in range(pps):
        o_ref[:, pp * LANES:(pp + 1) * LANES] = (
            jnp.concatenate(outs[2 * pp:2 * pp + 2], axis=0).T.astype(o_ref.dtype))


def _pair_attention(q, kv, batch, dk, tq, tables=None, tk=128, pps=1):
    t, w = q.shape
    qw = (LANES if dk == HEAD_DIM else 2 * LANES) * pps
    vw = LANES * pps
    groups = w // qw
    s = t // batch
    tq = min(tq, s)
    nq = s // tq
    tk = min([tk] + [k.shape[1] for k, _ in kv])
    in_specs = [pl.BlockSpec((tq, qw), lambda b, p, i: (b * nq + i, p))]
    args = [q]
    n_total = 0
    for k, v in kv:
        n = k.shape[1]
        n_total += n
        in_specs += [pl.BlockSpec((None, n, qw), lambda b, p, i: (b, 0, p)),
                     pl.BlockSpec((None, n, vw), lambda b, p, i: (b, 0, p))]
        args += [k, v]
    if tables is not None:
        blk = pl.BlockSpec((tables[0].shape[0], tq), lambda b, p, i: (0, i))
        in_specs += [blk, blk]
        args += list(tables)
    return pl.pallas_call(
        functools.partial(_pair_kernel, n_src=len(kv), tk=tk, dk=dk, rope=tables is not None),
        grid=(batch, groups, nq),
        in_specs=in_specs,
        out_specs=pl.BlockSpec((tq, vw), lambda b, p, i: (b * nq + i, p)),
        out_shape=jax.ShapeDtypeStruct((t, groups * vw), BF16),
        scratch_shapes=_attn_scratch(n_total, 2 * pps, tk, tq),
        compiler_params=_params(("arbitrary", "arbitrary", "arbitrary")),
        name="pair_attention",
    )(*args)


NB_QROWS = 4
NB_KROWS = 12
NB_TQ = NB_QROWS * GRID_W
NB_CROWS = 4
NB_TK = NB_CROWS * GRID_W
NB_INVALID = 2 * NA_ROWS - 1


def _nb_kernel(q_ref, kc_ref, vc_ref, k_ref, v_ref, tab_ref, o_ref, kcs_ref, vtc_ref, vt_ref, bias_ref,
               s_ref, p_ref):
    i = pl.program_id(2)
    rows = k_ref.shape[0] // GRID_W
    half = NA_ROWS // 2
    a = i * NB_QROWS
    base_blk = jnp.clip(i - half // NB_QROWS, 0, (rows - NB_KROWS) // NB_QROWS)
    base = base_blk * NB_QROWS
    base_chunk = base_blk * (NB_QROWS // NB_CROWS)
    tq = q_ref.shape[0]

    @pl.when(i == 0)
    def _():
        l, width = kcs_ref.shape
        kcs_ref[...] = kc_ref[...].reshape(l, width).astype(kcs_ref.dtype)
        vc = vc_ref[...].reshape(l, width)
        for ref, src in ((vtc_ref, vc), (vt_ref, v_ref)):
            for c in range(ref.shape[1]):
                for pp in range(width // LANES):
                    tiles = _vt_tiles(src[c * NB_TK:(c + 1) * NB_TK, pp * LANES:(pp + 1) * LANES])
                    for h in range(2):
                        ref[2 * pp + h, c] = tiles[h]

    def unclipped(first_row):
        return jnp.logical_and(first_row - half >= 0, first_row + NB_QROWS - 1 - half <= rows - NA_ROWS)

    @pl.when(jnp.logical_not(jnp.logical_and(unclipped(a), unclipped(a - NB_QROWS))))
    def _():
        lo_tile = lax.broadcasted_iota(jnp.int32, (GRID_W, LANES), 1) < HEAD_DIM
        for h in range(bias_ref.shape[0]):
            for j in range(NB_KROWS):
                kr = base + j
                for ip in range(NB_QROWS // 2):
                    halves = []
                    for qi in (2 * ip, 2 * ip + 1):
                        qr = a + qi
                        r0 = jnp.clip(qr - half, 0, rows - NA_ROWS)
                        valid = jnp.logical_and(kr >= r0, kr < r0 + NA_ROWS)
                        halves.append(tab_ref[h, jnp.where(valid, kr - qr + NA_ROWS - 1, NB_INVALID)])
                    bias_ref[h, j * GRID_W:(j + 1) * GRID_W, ip * LANES:(ip + 1) * LANES] = (
                        jnp.where(lo_tile, halves[0], halves[1]))

    top = lax.broadcasted_iota(jnp.int32, (LANES, tq), 0) < HEAD_DIM
    pps = q_ref.shape[1] // LANES
    q_ts = []
    for pp in range(pps):
        both = q_ref[:, pp * LANES:(pp + 1) * LANES].astype(F32).T
        q_ts += [jnp.where(top, both, 0.0).astype(BF16), jnp.where(top, 0.0, both).astype(BF16)]
    pair_lanes = lambda i_: slice((i_ // 2) * LANES, (i_ // 2 + 1) * LANES)
    chunks = []
    for c in range(vtc_ref.shape[1]):
        chunks.append((lambda i_, c=c: kcs_ref[c * NB_TK:(c + 1) * NB_TK, pair_lanes(i_)],
                       lambda i_, c=c: vtc_ref[i_, c], None))
    for c in range(NB_KROWS // NB_CROWS):
        start = pl.multiple_of((base + c * NB_CROWS) * GRID_W, NB_TK)
        chunks.append((lambda i_, start=start: k_ref[pl.ds(start, NB_TK), pair_lanes(i_)].astype(BF16),
                       lambda i_, c=c: vt_ref[i_, base_chunk + c],
                       lambda i_, c=c: bias_ref[i_, c * NB_TK:(c + 1) * NB_TK, :]))
    outs = _pipeline(q_ts, chunks, s_ref, p_ref)
    for pp in range(pps):
        o_ref[:, pp * LANES:(pp + 1) * LANES] = (
            jnp.concatenate(outs[2 * pp:2 * pp + 2], axis=0).T.astype(o_ref.dtype))


def _nb_attention(q, k, v, kctx, vctx, table, batch, pps=4):
    t, w = q.shape
    s = t // batch
    pw = LANES * pps
    groups = w // pw
    heads = 2 * pps
    tq = NB_TQ
    nq = s // tq
    l = kctx.shape[1]
    assert l % NB_TK == 0 and (s // GRID_W) % NB_QROWS == 0 and kctx.shape[2] == heads
    return pl.pallas_call(
        _nb_kernel,
        grid=(batch, groups, nq),
        in_specs=[
            pl.BlockSpec((tq, pw), lambda b, p, i: (b * nq + i, p)),
            pl.BlockSpec((None, l, heads, HEAD_DIM), lambda b, p, i: (b, 0, 0, 0)),
            pl.BlockSpec((None, l, heads, HEAD_DIM), lambda b, p, i: (b, 0, 0, 0)),
            pl.BlockSpec((None, s, pw), lambda b, p, i: (b, 0, p)),
            pl.BlockSpec((None, s, pw), lambda b, p, i: (b, 0, p)),
            pl.BlockSpec((heads, NB_INVALID + 1, GRID_W, LANES), lambda b, p, i: (p, 0, 0, 0)),
        ],
        out_specs=pl.BlockSpec((tq, pw), lambda b, p, i: (b * nq + i, p)),
        out_shape=jax.ShapeDtypeStruct((t, w), BF16),
        scratch_shapes=[pltpu.VMEM((l, pw), BF16),
                        pltpu.VMEM((heads, l // NB_TK, VT_ROWS, NB_TK), BF16),
                        pltpu.VMEM((heads, s // NB_TK, VT_ROWS, NB_TK), BF16),
                        pltpu.VMEM((heads, NB_KROWS * GRID_W, tq), F32),
                        *_pipe_scratch(heads, NB_TK, tq)],
        compiler_params=_params(("arbitrary", "arbitrary", "arbitrary")),
        name="nb_attention",
    )(q, kctx, vctx, k, v, table)


def _nb_bias_table(rpb):
    h, n_dr, n_dc = rpb.shape
    qc = jnp.arange(GRID_W)[None, :]
    kc = jnp.arange(GRID_W)[:, None]
    c0 = jnp.clip(qc - NA_COLS // 2, 0, GRID_W - NA_COLS)
    inwin = jnp.logical_and(kc >= c0, kc < c0 + NA_COLS)
    period = 2 * GRID_W - 1
    rev = rpb.astype(F32)[..., ::-1]
    seq = jnp.concatenate([rev[..., NA_COLS - 1:], jnp.zeros((h, n_dr, period - n_dc), F32),
                           rev[..., :NA_COLS - 1]], axis=-1)
    rel = jnp.tile(seq, (1, 1, GRID_W))[..., :GRID_W * (period - 1)]
    rel = rel.reshape(h, n_dr, GRID_W, period - 1)[..., :GRID_W]
    t = jnp.where(inwin[None, None], rel * LOG2E, NEG)
    t = jnp.concatenate([t, jnp.full((h, 1, GRID_W, GRID_W), NEG, F32)], axis=1)
    return jnp.concatenate([t, t], axis=-1)


PAD = 8


def _window_sum(pad_ref, tmp_ref, lanes, win, tm):
    def rows(lo, n):
        return pad_ref[PAD + lo:PAD + lo + n, lanes]

    if win <= 4:
        acc = rows(-(win // 2), tm)
        for d in range(-(win // 2) + 1, win // 2):
            acc = acc + rows(d, tm)
        return acc
    ext = win // 2 - 2
    n = tm + 2 * ext
    cur = rows(-ext - 2, n)
    for d in (-1, 0, 1):
        cur = cur + rows(-ext + d, n)
    width = 4
    while width < win:
        tmp_ref[0:n, :] = cur
        h = width // 2
        prev_ext, ext = ext, ext - h
        n = tm + 2 * ext
        lo = prev_ext - ext
        cur = tmp_ref[lo - h:lo - h + n, :] + tmp_ref[lo + h:lo + h + n, :]
        width *= 2
    return cur


def _project_out(x_ref, ta, tb, w_ref, g_ref, gate_ref, o_ref):
    half = ta.shape[1]
    y = (jnp.dot(ta, w_ref[0:half, :], preferred_element_type=F32)
         + jnp.dot(tb, w_ref[half:2 * half, :], preferred_element_type=F32))
    o_ref[...] = x_ref[...] + gate_ref[...] * _rms_lanes(y, g_ref[...])


def _out_kernel(x_ref, a_ref, ga_ref, b_ref, gb_ref, w_ref, g_ref, gate_ref, o_ref):
    ta = (a_ref[...].astype(F32) * _silu(ga_ref[...].astype(F32))).astype(BF16)
    tb = (b_ref[...].astype(F32) * _silu(gb_ref[...].astype(F32))).astype(BF16)
    _project_out(x_ref, ta, tb, w_ref, g_ref, gate_ref, o_ref)


def _out_even_kernel(x_ref, a_ref, ga_ref, u_ref, up_ref, un_ref, gb_ref, map_ref, sc_ref, w_ref, g_ref,
                     gate_ref, o_ref, pad_ref, tmp_ref, *, nseq, seq):
    tm = a_ref.shape[0]
    pos = pl.program_id(0) % nseq
    u = u_ref[...]
    pad_ref[PAD:PAD + tm, :] = u
    pad_ref[0:PAD, :] = jnp.where(pos == 0, 0.0, up_ref[...])
    pad_ref[PAD + tm:PAD + tm + PAD, :] = jnp.where(pos == nseq - 1, 0.0, un_ref[...])
    t = pos * tm + lax.broadcasted_iota(jnp.int32, (tm, LANES), 0)
    ys = []
    for g, win in enumerate(POOL_WINDOWS):
        lanes = slice(g * LANES, (g + 1) * LANES)
        acc = _window_sum(pad_ref, tmp_ref, lanes, win, tm)
        cnt = jnp.minimum(t + win // 2, seq) - jnp.maximum(t - win // 2, 0)
        diff = acc / cnt.astype(F32) - u[:, lanes]
        ys.append(jnp.dot(diff.astype(BF16), map_ref[g].astype(BF16), preferred_element_type=F32))
    o_b = jnp.concatenate(ys, axis=1) * sc_ref[...]
    ta = (a_ref[...].astype(F32) * _silu(ga_ref[...].astype(F32))).astype(BF16)
    tb = (o_b * _silu(gb_ref[...].astype(F32))).astype(BF16)
    _project_out(x_ref, ta, tb, w_ref, g_ref, gate_ref, o_ref)


def _out_even(x, a, ga, u, gb, b_map, b_scale, w, g_post, gate, rows_per_mod, seq, tm=512):
    t, d = x.shape
    half = a.shape[1]
    tm = min(tm, seq)
    nseq = seq // tm
    br = pl.BlockSpec((tm, half), lambda i: (i, 0))
    halo_blocks = tm // PAD
    prev = pl.BlockSpec((PAD, half), lambda i: (jnp.maximum(i * halo_blocks - 1, 0), 0))
    nxt = pl.BlockSpec((PAD, half), lambda i: (jnp.minimum((i + 1) * halo_blocks, t // PAD - 1), 0))
    full = lambda arr: pl.BlockSpec(arr.shape, lambda i: (0,) * arr.ndim)
    sc = b_scale.reshape(1, half)
    gate, gate_spec = _mod_operand(gate, d, tm, rows_per_mod)
    return pl.pallas_call(
        functools.partial(_out_even_kernel, nseq=nseq, seq=seq),
        grid=(t // tm,),
        in_specs=[
            pl.BlockSpec((tm, d), lambda i: (i, 0)),
            br, br, br, prev, nxt, br, full(b_map), full(sc), full(w), full(g_post), gate_spec,
        ],
        out_specs=pl.BlockSpec((tm, d), lambda i: (i, 0)),
        out_shape=jax.ShapeDtypeStruct((t, d), F32),
        scratch_shapes=[pltpu.VMEM((tm + 2 * PAD, half), F32), pltpu.VMEM((tm + 2 * PAD, LANES), F32)],
        compiler_params=_params(("arbitrary",)),
        name="out_proj_even",
    )(x, a, ga, u, u, u, gb, b_map, sc, w, g_post, gate)


def _out_proj(x, a, ga, b, gb, w, g_post, gate, rows_per_mod, tm=512):
    t, d = x.shape
    half = a.shape[1]
    tm = min(tm, t)
    br = pl.BlockSpec((tm, half), lambda i: (i, 0))
    gate, gate_spec = _mod_operand(gate, d, tm, rows_per_mod)
    return pl.pallas_call(
        _out_kernel,
        grid=(t // tm,),
        in_specs=[
            pl.BlockSpec((tm, d), lambda i: (i, 0)),
            br, br, br, br,
            pl.BlockSpec((2 * half, d), lambda i: (0, 0)),
            pl.BlockSpec((1, d), lambda i: (0, 0)),
            gate_spec,
        ],
        out_specs=pl.BlockSpec((tm, d), lambda i: (i, 0)),
        out_shape=jax.ShapeDtypeStruct((t, d), F32),
        compiler_params=_params(("arbitrary",)),
        name="out_proj",
    )(x, a, ga, b, gb, w, g_post, gate)


def _kvup_kernel(c_ref, pe_ref, wk_ref, wv_ref, k_ref, v_ref):
    cb = c_ref[...].astype(BF16)
    heads = k_ref.shape[1] // LANES
    k = jnp.dot(cb, wk_ref[...], preferred_element_type=F32)
    k = k + jnp.concatenate([pe_ref[...]] * heads, axis=1)
    k_ref[...] = k.astype(k_ref.dtype)
    v_ref[...] = jnp.dot(cb, wv_ref[...], preferred_element_type=F32).astype(v_ref.dtype)


def _kvup(ckv, pe128, wk, wv, tm=512):
    t, r = ckv.shape
    tm = min(tm, t)
    nk, nv = wk.shape[1], wv.shape[1]
    row = lambda width: pl.BlockSpec((tm, width), lambda i: (i, 0))
    full = lambda a: pl.BlockSpec(a.shape, lambda i: (0, 0))
    return pl.pallas_call(
        _kvup_kernel,
        grid=(t // tm,),
        in_specs=[row(r), row(LANES), full(wk), full(wv)],
        out_specs=[row(nk), row(nv)],
        out_shape=[jax.ShapeDtypeStruct((t, nk), BF16), jax.ShapeDtypeStruct((t, nv), BF16)],
        compiler_params=_params(("arbitrary",)),
        name="kv_up",
    )(ckv, pe128, wk, wv)


def kernel(x_prompt, x_sample, cache_a_k, cache_a_v, cache_c_k, cache_c_v, cache_d_ckv, cache_d_kpe,
           c, c_ctx, w_mod, b_mod, g_pre, g_post, w_in_e, a_q_norm, a_k_norm, b_map, b_scale, w_out_e,
           w_in_o, c_rpb, d_q_norm, d_w_uq, d_kv_norm, d_w_ukv, w_out_o):
    bp, sp, d = x_prompt.shape
    bs, ss, _ = x_sample.shape
    past = cache_a_k.shape[2]
    depth = w_mod.shape[0]
    tp, ts = bp * sp, bs * ss
    branch = d // 2
    d_heads = 8
    d_nope, d_rope, d_v = 64, 32, 64
    dqk = d_nope + d_rope
    q_lora, kv_lora = d_q_norm.shape[1], d_kv_norm.shape[1]
    kv_w = a_k_norm.shape[1] * 2

    xp = x_prompt.reshape(tp, d)
    xs = x_sample.reshape(ts, d)

    cond = jnp.concatenate([c_ctx[None, :], c, jnp.zeros((8 - 1 - bs, d), F32)], axis=0)
    mods = _mod_call(cond, w_mod, b_mod)

    mod_table = mods.reshape(depth * cond.shape[0] * MOD_PARTS, 1, d)

    def mod_rows(layer, part, lo_row):
        return mod_table, layer * cond.shape[0] + lo_row, part

    rope_a = _axial_tables_rows(ss, HEAD_DIM)
    rope_d = _axial_tables_rows(ss, d_rope)
    rope_d_lanes = _axial_tables_lanes(ss, d_rope, d_nope)
    no_rope_p = tuple(jnp.zeros((sp, LANES), F32) for _ in range(3))

    outs = {name: [] for name in ("a_k", "a_v", "c_k", "c_v", "d_ckv", "d_kpe")}
    for layer in range(depth):
        j = layer // 2
        g_in = g_pre[layer].reshape(1, d)
        g_out = g_post[layer].reshape(1, d)
        paths = (("p", xp, bp, sp, 0, 1, tp), ("s", xs, bs, ss, 1, bs, ss))
        if layer % 2 == 0:
            w_in = w_in_e[j].astype(BF16)
            w_out = w_out_e[j].astype(BF16)
            o0 = branch
            new_x = []
            for name, x, nb, seq, row0, nrows, rpm in paths:
                shift, scale, gate = (mod_rows(layer, part, row0) for part in range(MOD_PARTS))
                v_dt = F32 if name == "p" else BF16
                segs = [(0, branch, F32), (o0, kv_w, F32), (o0 + kv_w, kv_w, v_dt),
                        (o0 + 2 * kv_w, branch, BF16), (o0 + 2 * kv_w + branch, branch, F32),
                        (o0 + 2 * kv_w + 2 * branch, branch, BF16)]
                q, k, v, ga, ub, gb = _norm_mm(x, g_in, scale, shift, w_in, segs, rpm, name="in_proj_even")
                k3, v3 = k.reshape(nb, seq, kv_w), v.reshape(nb, seq, kv_w)
                if name == "p":
                    o_a, kn, vn = _gqa_attention(q, [(k3, v3)], a_q_norm[j], a_k_norm[j], None, nb, tq=256,
                                                 emit_k=True)
                    outs["a_k"].append(kn)
                    outs["a_v"].append(vn)
                else:
                    kv = [(cache_a_k[:, j], cache_a_v[:, j]), (k3, v3)]
                    o_a = _gqa_attention(q, kv, a_q_norm[j], a_k_norm[j], rope_a, nb, tq=256, emit_k=False)
                new_x.append(_out_even(x, o_a, ga, ub, gb, b_map[j], b_scale[j], w_out, g_out, gate, rpm, seq))
            xp, xs = new_x
        else:
            wi = w_in_o[j]
            off_kpe = 4 * branch + q_lora + kv_lora
            w_in = jnp.concatenate([wi[:, :off_kpe], jnp.zeros((d, d_nope), F32),
                                    wi[:, off_kpe:off_kpe + d_rope],
                                    jnp.zeros((d, LANES - dqk), F32),
                                    wi[:, off_kpe + d_rope:]], axis=1).astype(BF16)
            w_out = w_out_o[j].astype(BF16)
            w_uq = jnp.pad(d_w_uq[j].reshape(q_lora, d_heads, dqk),
                           ((0, 0), (0, 0), (0, LANES - dqk))).reshape(q_lora, d_heads * LANES).astype(BF16)
            ukv = d_w_ukv[j].reshape(kv_lora, d_heads, d_nope + d_v)
            w_uk = jnp.pad(ukv[:, :, :d_nope], ((0, 0), (0, 0), (0, LANES - d_nope))
                           ).reshape(kv_lora, d_heads * LANES).astype(BF16)
            w_uv = ukv[:, :, d_nope:].reshape(kv_lora, d_heads * d_v).astype(BF16)
            qn_g = d_q_norm[j].reshape(1, q_lora)
            kvn_g = d_kv_norm[j].reshape(1, kv_lora)
            table = _nb_bias_table(c_rpb[j])
            new_x = []
            for name, x, nb, seq, row0, nrows, rpm in paths:
                shift, scale, gate = (mod_rows(layer, part, row0) for part in range(MOD_PARTS))
                latent = name == "s"
                qc, kc, vc, gc, gd, qd, ckv_n, kpe, k_d, v_d, *cache_kv = _in_odd(
                    x, g_in, scale, shift, w_in, qn_g, kvn_g, w_uq, w_uk, w_uv,
                    rope_d_lanes if latent else no_rope_p, rpm, seq, d_rope // 4 if latent else 0,
                    not latent, F32 if latent else BF16, branch,
                    LOG2E * HEAD_DIM ** -0.5, LOG2E * dqk ** -0.5)
                kc3, vc3 = kc.reshape(nb, seq, branch), vc.reshape(nb, seq, branch)
                kd3, vd3 = k_d.reshape(nb, seq, -1), v_d.reshape(nb, seq, -1)
                if not latent:
                    outs["c_k"].append(cache_kv[0])
                    outs["c_v"].append(cache_kv[1])
                    outs["d_ckv"].append(ckv_n)
                    outs["d_kpe"].append(kpe[:, d_nope:dqk])
                    all_pairs = branch // LANES
                    o_c = _pair_attention(qc, [(kc3, vc3)], nb, HEAD_DIM, tq=256, pps=all_pairs)
                    o_d = _pair_attention(qd, [(kd3, vd3)], nb, 2 * HEAD_DIM, tq=256, pps=all_pairs)
                else:
                    o_c = _nb_attention(qc, kc3, vc3, cache_c_k[:, j], cache_c_v[:, j], table, nb)
                    pe_ctx = jnp.pad(cache_d_kpe[:, j].reshape(nb * past, d_rope),
                                     ((0, 0), (d_nope, LANES - dqk)))
                    k_ctx, v_ctx = _kvup(cache_d_ckv[:, j].reshape(nb * past, kv_lora), pe_ctx, w_uk, w_uv)
                    kv = [(k_ctx.reshape(nb, past, -1), v_ctx.reshape(nb, past, -1)), (kd3, vd3)]
                    o_d = _pair_attention(qd, kv, nb, 2 * HEAD_DIM, tq=512, tables=rope_d, pps=2)
                new_x.append(_out_proj(x, o_c, gc, o_d, gd, w_out, g_out, gate, rpm))
            xp, xs = new_x

    kvh = kv_w // HEAD_DIM
    ch = branch // HEAD_DIM

    def stacked(name, *tail):
        return jnp.stack([a.reshape(bp, sp, *tail) for a in outs[name]], axis=1)

    return (xp.reshape(bp, sp, d), xs.reshape(bs, ss, d),
            stacked("a_k", kvh, HEAD_DIM), stacked("a_v", kvh, HEAD_DIM),
            stacked("c_k", ch, HEAD_DIM), stacked("c_v", ch, HEAD_DIM),
            stacked("d_ckv", kv_lora), stacked("d_kpe", d_rope))
```

```python
import functools

import jax
import jax.numpy as jnp
from jax import lax
from jax.experimental import pallas as pl
from jax.experimental.pallas import tpu as pltpu

F32 = jnp.float32
BF16 = jnp.bfloat16

LANES = 128
GRID_W = 64
HEAD_DIM = 64
NA_ROWS = 8
NA_COLS = 16
POOL_WINDOWS = (2, 4, 8, 16)
ROPE_THETA = 10000.0
EPS = 1e-6
NEG = -1e30
LOG2E = 1.4426950408889634
VMEM_LIMIT = 56 * 1024 * 1024


def _params(sem):
    return pltpu.CompilerParams(dimension_semantics=sem, vmem_limit_bytes=VMEM_LIMIT)


def _silu(x):
    h = 0.5 * x
    return h + h * jnp.tanh(h)


def _rms_lanes(x, gain):
    ms = jnp.mean(x * x, axis=-1, keepdims=True)
    return x * lax.rsqrt(ms + EPS) * gain


MOD_PARTS = 3


def _mod_operand(mod, width, tm, rows_per_mod):
    table, row0, part = mod
    index = lambda i: ((row0 + (i * tm) // rows_per_mod) * MOD_PARTS + part, 0, 0)
    return table, pl.BlockSpec((None, 1, width), index)


def _mod_kernel(c_ref, w_ref, b_ref, o_ref):
    s = _silu(c_ref[...]).astype(BF16)
    o_ref[...] = jnp.dot(s, w_ref[...].astype(BF16), preferred_element_type=F32) + b_ref[...]


def _mod_call(cond, w_mod, b_mod):
    depth, d, n = w_mod.shape
    rows = cond.shape[0]
    tn = 1024
    return pl.pallas_call(
        _mod_kernel,
        grid=(depth, n // tn),
        in_specs=[
            pl.BlockSpec((rows, d), lambda l, j: (0, 0)),
            pl.BlockSpec((None, d, tn), lambda l, j: (l, 0, j)),
            pl.BlockSpec((None, 1, tn), lambda l, j: (l, 0, j)),
        ],
        out_specs=pl.BlockSpec((None, rows, tn), lambda l, j: (l, 0, j)),
        out_shape=jax.ShapeDtypeStruct((depth, rows, n), F32),
        compiler_params=_params(("arbitrary", "arbitrary")),
        name="adaln_mod",
    )(cond, w_mod, b_mod.reshape(depth, 1, n))


def _norm_mm_kernel(x_ref, g_ref, sc_ref, sh_ref, w_ref, *o_refs, segs, do_norm):
    x = x_ref[...].astype(F32)
    if do_norm:
        x = _rms_lanes(x, g_ref[...]) * (1.0 + sc_ref[...]) + sh_ref[...]
    z = jnp.dot(x.astype(BF16), w_ref[...], preferred_element_type=F32)
    for o_ref, (start, width) in zip(o_refs, segs):
        o_ref[...] = z[:, start:start + width].astype(o_ref.dtype)


def _norm_mm(x, g, scale, shift, w, segs, rows_per_mod, do_norm=True, tm=512, name="norm_mm"):
    t, k = x.shape
    n = w.shape[1]
    tm = min(tm, t)
    kern = functools.partial(_norm_mm_kernel, segs=tuple((s, wd) for s, wd, _ in segs), do_norm=do_norm)
    scale, scale_spec = _mod_operand(scale, k, tm, rows_per_mod)
    shift, shift_spec = _mod_operand(shift, k, tm, rows_per_mod)
    return pl.pallas_call(
        kern,
        grid=(t // tm,),
        in_specs=[
            pl.BlockSpec((tm, k), lambda i: (i, 0)),
            pl.BlockSpec((1, k), lambda i: (0, 0)),
            scale_spec,
            shift_spec,
            pl.BlockSpec((k, n), lambda i: (0, 0)),
        ],
        out_specs=[pl.BlockSpec((tm, wd), lambda i: (i, 0)) for _, wd, _ in segs],
        out_shape=[jax.ShapeDtypeStruct((t, wd), dt) for _, wd, dt in segs],
        compiler_params=_params(("arbitrary",)),
        name=name,
    )(x, g, scale, shift, w)


def _in_odd_kernel(x_ref, g_ref, sc_ref, sh_ref, w_ref, qg_ref, kvg_ref, wuq_ref, wuk_ref, wuv_ref, *refs,
                   branch, q_lora, kv_lora, rope_shift, qc_mul, qd_mul):
    if rope_shift:
        cos_ref, sa_ref, sb_ref = refs[:3]
        refs = refs[3:]
    qc_ref, kc_ref, vc_ref, gc_ref, gd_ref, qd_ref, ckv_ref, kpe_ref, kd_ref, vd_ref = refs[:10]
    cache_refs = refs[10:]
    x = _rms_lanes(x_ref[...], g_ref[...]) * (1.0 + sc_ref[...]) + sh_ref[...]
    z = jnp.dot(x.astype(BF16), w_ref[...], preferred_element_type=F32)
    b = branch
    qc_ref[...] = (z[:, 0:b] * qc_mul).astype(qc_ref.dtype)
    kc_ref[...] = z[:, b:2 * b].astype(kc_ref.dtype)
    vc_ref[...] = z[:, 2 * b:3 * b].astype(vc_ref.dtype)
    gc_ref[...] = z[:, 3 * b:4 * b].astype(gc_ref.dtype)
    if cache_refs:
        _store_heads(cache_refs[0], slice(None), z[:, b:2 * b])
        _store_heads(cache_refs[1], slice(None), z[:, 2 * b:3 * b])
    o = 4 * b
    cq = _rms_lanes(z[:, o:o + q_lora], qg_ref[...])
    qd = jnp.dot(cq.astype(BF16), wuq_ref[...], preferred_element_type=F32)
    qd_ref[...] = (qd * qd_mul).astype(qd_ref.dtype)
    o += q_lora
    ckv = _rms_lanes(z[:, o:o + kv_lora], kvg_ref[...])
    ckv_ref[...] = ckv
    o += kv_lora
    kpe = z[:, o:o + LANES]
    if rope_shift:
        kpe = (kpe * cos_ref[...] + pltpu.roll(kpe, LANES - rope_shift, 1) * sa_ref[...]
               + pltpu.roll(kpe, rope_shift, 1) * sb_ref[...])
    kpe_ref[...] = kpe
    o += LANES
    gd_ref[...] = z[:, o:o + b].astype(gd_ref.dtype)
    cb = ckv.astype(BF16)
    heads = kd_ref.shape[1] // LANES
    kd = jnp.dot(cb, wuk_ref[...], preferred_element_type=F32) + jnp.concatenate([kpe] * heads, axis=1)
    kd_ref[...] = kd.astype(kd_ref.dtype)
    vd_ref[...] = jnp.dot(cb, wuv_ref[...], preferred_element_type=F32).astype(vd_ref.dtype)


def _in_odd(x, g, scale, shift, w, qg, kvg, wuq, wuk, wuv, tables, rows_per_mod, seq, rope_shift,
            emit_cache, qd_dtype, branch, qc_mul, qd_mul, tm=512):
    t, k = x.shape
    tm = min(tm, seq if rope_shift else t)
    nseq = max(seq // tm, 1)
    q_lora, kv_lora = qg.shape[1], kvg.shape[1]
    widths = [(branch, BF16), (branch, BF16), (branch, BF16), (branch, BF16), (branch, BF16),
              (wuq.shape[1], qd_dtype), (kv_lora, F32), (LANES, F32), (wuk.shape[1], BF16), (wuv.shape[1], BF16)]
    out_specs = [pl.BlockSpec((tm, wd), lambda i: (i, 0)) for wd, _ in widths]
    out_shape = [jax.ShapeDtypeStruct((t, wd), dt) for wd, dt in widths]
    if emit_cache:
        heads = branch // HEAD_DIM
        out_specs += [pl.BlockSpec((tm, heads, HEAD_DIM), lambda i: (i, 0, 0))] * 2
        out_shape += [jax.ShapeDtypeStruct((t, heads, HEAD_DIM), F32)] * 2
    scale, scale_spec = _mod_operand(scale, k, tm, rows_per_mod)
    shift, shift_spec = _mod_operand(shift, k, tm, rows_per_mod)
    full = lambda a: pl.BlockSpec(a.shape, lambda i: (0, 0))
    tab_spec = pl.BlockSpec((tm, LANES), lambda i: (i % nseq, 0))
    kern = functools.partial(_in_odd_kernel, branch=branch, q_lora=q_lora, kv_lora=kv_lora,
                             rope_shift=rope_shift, qc_mul=qc_mul, qd_mul=qd_mul)
    return pl.pallas_call(
        kern,
        grid=(t // tm,),
        in_specs=[pl.BlockSpec((tm, k), lambda i: (i, 0)), full(g),
                  scale_spec, shift_spec,
                  full(w), full(qg), full(kvg), full(wuq), full(wuk), full(wuv)]
        + ([tab_spec] * 3 if rope_shift else []),
        out_specs=out_specs,
        out_shape=out_shape,
        compiler_params=_params(("arbitrary",)),
        name="in_proj_odd",
    )(x, g, scale, shift, w, qg, kvg, wuq, wuk, wuv, *(tables if rope_shift else ()))


def _axial_angles(seq, dims):
    half = dims // 2
    inv = ROPE_THETA ** (-jnp.arange(0, half, 2, dtype=F32) / half)
    t = jnp.arange(seq)
    ang_r = (t // GRID_W).astype(F32)[:, None] * inv
    ang_c = (t % GRID_W).astype(F32)[:, None] * inv
    return ang_r, ang_c


def _axial_tables_lanes(seq, dims, lane0):
    ang_r, ang_c = _axial_angles(seq, dims)
    cos = jnp.concatenate([jnp.cos(ang_r)] * 2 + [jnp.cos(ang_c)] * 2, axis=-1)
    sin = jnp.concatenate([jnp.sin(ang_r)] * 2 + [jnp.sin(ang_c)] * 2, axis=-1)
    first = (jnp.arange(dims) % (dims // 2)) < dims // 4
    sa = jnp.where(first, -sin, 0.0)
    sb = jnp.where(first, 0.0, sin)
    pad_l, pad_r = lane0, LANES - lane0 - dims
    return (jnp.pad(cos, ((0, 0), (pad_l, pad_r)), constant_values=1.0),
            jnp.pad(sa, ((0, 0), (pad_l, pad_r))), jnp.pad(sb, ((0, 0), (pad_l, pad_r))))


def _axial_tables_rows(seq, dims):
    ang_r, ang_c = _axial_angles(seq, dims)
    cos = jnp.concatenate([jnp.cos(ang_r)] * 2 + [jnp.cos(ang_c)] * 2, axis=-1).T
    sin = jnp.concatenate([-jnp.sin(ang_r), jnp.sin(ang_r), -jnp.sin(ang_c), jnp.sin(ang_c)], axis=-1).T
    return cos, sin


def _rms_rows(x, gain):
    ms = jnp.mean(x * x, axis=0, keepdims=True)
    return x * lax.rsqrt(ms + EPS) * gain


def _rope_rows(x, cos, sin):
    q = x.shape[0] // 4
    partner = jnp.concatenate([x[q:2 * q], x[0:q], x[3 * q:4 * q], x[2 * q:3 * q]], axis=0)
    return x * cos + partner * sin


def _attend_t(streams, k_refs, vt_ref, s_ref, p_ref, tk):
    chunks = []
    off = 0
    for src, k_ref in enumerate(k_refs):
        n = k_ref.shape[0]
        assert n % tk == 0
        for c in range(n // tk):
            def k_fn(i, src=src, r=c * tk):
                ref = streams[i][2][src] if len(streams[i]) > 2 else k_refs[src]
                return ref[r:r + tk, streams[i][1]].astype(BF16)
            vt_fn = lambda i, col=off + c * tk: vt_ref[i, :, col:col + tk]
            chunks.append((k_fn, vt_fn, None))
        off += n
    return _pipeline([stream[0] for stream in streams], chunks, s_ref, p_ref)


def _pipeline(q_ts, chunks, s_ref, p_ref):
    nc = len(chunks)
    ns = len(q_ts)
    mq = q_ts[0].shape[1]
    m = [jnp.full((1, mq), -jnp.inf, F32) for _ in range(ns)]
    acc = [jnp.zeros((VT_ROWS, mq), F32) for _ in range(ns)]
    m_chunk = [None] * ns
    alpha = [[None] * ns for _ in range(2)]
    for t in range(nc + 2):
        if t < nc:
            k_fn, _, bias_fn = chunks[t]
            for i, q_t in enumerate(q_ts):
                s = jnp.dot(k_fn(i), q_t, preferred_element_type=F32)
                if bias_fn is not None:
                    s = s + bias_fn(i)
                m_chunk[i] = jnp.max(s, axis=0, keepdims=True)
                s_ref[(t % 2) * ns + i] = s
        if 0 <= t - 2 < nc:
            for i in range(ns):
                pv = jnp.dot(chunks[t - 2][1](i), p_ref[(t % 2) * ns + i], preferred_element_type=F32)
                acc[i] = alpha[t % 2][i] * acc[i] + pv
        if 0 <= t - 1 < nc:
            for i in range(ns):
                m_new = jnp.maximum(m[i], m_prev_chunk[i])
                alpha[(t - 1) % 2][i] = jnp.exp2(m[i] - m_new)
                p = jnp.exp2(s_ref[((t - 1) % 2) * ns + i] - m_new)
                p_ref[((t - 1) % 2) * ns + i] = p.astype(BF16)
                m[i] = m_new
        m_prev_chunk = list(m_chunk)
    return [acc[i][:HEAD_DIM] / acc[i][HEAD_DIM:HEAD_DIM + 1] for i in range(ns)]


VT_ROWS = HEAD_DIM + 16

def _vt_tiles(blk):
    n = blk.shape[0]
    v_t = blk.astype(F32).T
    tail = (lax.broadcasted_iota(jnp.int32, (VT_ROWS - HEAD_DIM, n), 0) == 0).astype(F32)
    return [jnp.concatenate([v_t[h * HEAD_DIM:(h + 1) * HEAD_DIM], tail], axis=0).astype(BF16)
            for h in range(2)]


def _fill_vt(vt_ref, v_refs, chunk=512, base=0):
    off = 0
    for v_ref in v_refs:
        n = v_ref.shape[0]
        step = min(chunk, n)
        for c in range(n // step):
            for pp in range(v_ref.shape[1] // LANES):
                tiles = _vt_tiles(v_ref[c * step:(c + 1) * step, pp * LANES:(pp + 1) * LANES])
                for h in range(2):
                    vt_ref[base + 2 * pp + h, :, off + c * step:off + (c + 1) * step] = tiles[h]
        off += n


def _pipe_scratch(n_streams, tk, mq):
    return [pltpu.VMEM((2 * n_streams, tk, mq), F32), pltpu.VMEM((2 * n_streams, tk, mq), BF16)]


def _attn_scratch(n_total, n_streams, tk, mq):
    return [pltpu.VMEM((n_streams, VT_ROWS, n_total), BF16)] + _pipe_scratch(n_streams, tk, mq)


def _store_heads(ref, rows, x):
    ref[rows] = x.astype(ref.dtype).reshape(x.shape[0], ref.shape[1], HEAD_DIM)


def _gqa_kernel(*refs, n_src, tk, rope, emit_k, cache_src, q_mul):
    q_ref = refs[0]
    k_refs = [refs[1 + 2 * i] for i in range(n_src)]
    v_refs = [refs[2 + 2 * i] for i in range(n_src)]
    pos = 1 + 2 * n_src
    qg_ref, kg_ref = refs[pos], refs[pos + 1]
    pos += 2
    if rope:
        cosq_ref, sinq_ref, cosk_ref, sink_ref = refs[pos:pos + 4]
        pos += 4
    o_ref = refs[pos]
    pos += 1
    if emit_k:
        kn_ref, vn_ref = refs[pos:pos + 2]
        pos += 2
    if cache_src:
        kcs_ref = refs[pos]
        pos += 1
    kp_ref, vt_ref, s_ref, p_ref = refs[pos:pos + 4]
    tq = q_ref.shape[0]

    @pl.when(pl.program_id(1) == 0)
    def _():
        v_srcs = list(v_refs)
        if cache_src:
            kcs_ref[...] = k_refs[0][...].reshape(kcs_ref.shape).astype(kcs_ref.dtype)
            v_srcs[0] = v_refs[0][...].reshape(kcs_ref.shape)
        _fill_vt(vt_ref, v_srcs)
        raw = k_refs[-1]
        for c in range(raw.shape[0] // tk):
            rows = slice(c * tk, (c + 1) * tk)
            k_t = raw[rows, :].astype(F32).T
            halves = []
            for hh in range(2):
                x = _rms_rows(k_t[hh * HEAD_DIM:(hh + 1) * HEAD_DIM], kg_ref[...])
                if rope:
                    x = _rope_rows(x, cosk_ref[:, rows], sink_ref[:, rows])
                halves.append(x)
            kn = jnp.concatenate(halves, axis=0).T
            kp_ref[rows, :] = kn.astype(kp_ref.dtype)
            if emit_k:
                _store_heads(kn_ref, rows, kn)
                _store_heads(vn_ref, rows, v_refs[-1][rows, :])

    zeros = jnp.zeros((HEAD_DIM, tq), F32)
    parts = []
    for j in range(4):
        q_t = q_ref[:, j * LANES:(j + 1) * LANES].astype(F32).T
        for hh in range(2):
            x = _rms_rows(q_t[hh * HEAD_DIM:(hh + 1) * HEAD_DIM], qg_ref[...])
            if rope:
                x = _rope_rows(x, cosq_ref[...], sinq_ref[...])
            parts.append(x * q_mul)
    streams = []
    for g in range(2):
        cols = [jnp.concatenate([parts[4 * g + n], zeros] if g == 0 else [zeros, parts[4 * g + n]], axis=0)
                for n in range(4)]
        streams.append((jnp.concatenate(cols, axis=1).astype(BF16), slice(None)))
    given = [kcs_ref] if cache_src else k_refs[:-1]
    outs = _attend_t(streams, given + [kp_ref], vt_ref, s_ref, p_ref, tk)
    for g in range(2):
        o_t = outs[g]
        for idx, j in enumerate((2 * g, 2 * g + 1)):
            blk_t = jnp.concatenate([o_t[:, (2 * idx) * tq:(2 * idx + 1) * tq],
                                     o_t[:, (2 * idx + 1) * tq:(2 * idx + 2) * tq]], axis=0)
            o_ref[:, j * LANES:(j + 1) * LANES] = blk_t.T.astype(o_ref.dtype)


def _gqa_attention(q, kv, q_gain, k_gain, tables, batch, tq, emit_k, tk=128):
    t, w = q.shape
    s = t // batch
    tq = min(tq, s)
    nq = s // tq
    tk = min([tk] + [k.shape[1] for k, _ in kv])
    in_specs = [pl.BlockSpec((tq, w), lambda b, i: (b * nq + i, 0))]
    args = [q]
    n_total = 0
    cache_rows = 0
    for k, v in kv:
        n = k.shape[1]
        n_total += n
        if k.ndim == 4:
            cache_rows = n
            spec = pl.BlockSpec((None,) + k.shape[1:], lambda b, i: (b, 0, 0, 0))
        else:
            spec = pl.BlockSpec((None, n, LANES), lambda b, i: (b, 0, 0))
        in_specs += [spec, spec]
        args += [k, v]
    assert cache_rows == 0 or (len(kv) == 2 and kv[0][0].ndim == 4)
    n_raw = kv[-1][0].shape[1]
    const = lambda a: pl.BlockSpec(a.shape, lambda b, i: (0, 0))
    qg = jnp.broadcast_to(q_gain[:, None], (HEAD_DIM, tq))
    kg = jnp.broadcast_to(k_gain[:, None], (HEAD_DIM, tk))
    in_specs += [const(qg), const(kg)]
    args += [qg, kg]
    if tables is not None:
        cos, sin = tables
        blk = pl.BlockSpec((HEAD_DIM, tq), lambda b, i: (0, i))
        in_specs += [blk, blk, const(cos), const(sin)]
        args += [cos, sin, cos, sin]
    out_specs = [pl.BlockSpec((tq, w), lambda b, i: (b * nq + i, 0))]
    out_shape = [jax.ShapeDtypeStruct((t, w), BF16)]
    if emit_k:
        out_specs += [pl.BlockSpec((n_raw, 2, HEAD_DIM), lambda b, i: (b, 0, 0))] * 2
        out_shape += [jax.ShapeDtypeStruct((batch * n_raw, 2, HEAD_DIM), F32)] * 2
    res = pl.pallas_call(
        functools.partial(_gqa_kernel, n_src=len(kv), tk=tk, rope=tables is not None, emit_k=emit_k,
                          cache_src=cache_rows > 0, q_mul=LOG2E * HEAD_DIM ** -0.5),
        grid=(batch, nq),
        in_specs=in_specs,
        out_specs=out_specs,
        out_shape=out_shape,
        scratch_shapes=([pltpu.VMEM((cache_rows, LANES), BF16)] if cache_rows else [])
        + [pltpu.VMEM((n_raw, LANES), BF16)] + _attn_scratch(n_total, 2, tk, 4 * tq),
        compiler_params=_params(("arbitrary", "arbitrary")),
        name="gqa_attention",
    )(*args)
    return res if emit_k else res[0]


def _pair_kernel(*refs, n_src, tk, dk, rope):
    q_ref = refs[0]
    k_refs = [refs[1 + 2 * i] for i in range(n_src)]
    v_refs = [refs[2 + 2 * i] for i in range(n_src)]
    pos = 1 + 2 * n_src
    if rope:
        cos_ref, sin_ref = refs[pos:pos + 2]
        pos += 2
    o_ref, vt_ref, s_ref, p_ref = refs[pos:pos + 4]
    bps = k_refs[0].shape[0]
    tq = q_ref.shape[0] // bps
    pps = o_ref.shape[1] // LANES

    @pl.when(pl.program_id(2) == 0)
    def _():
        for bb in range(bps):
            _fill_vt(vt_ref, [r.at[bb] for r in v_refs], base=bb * 2 * pps)

    top = lax.broadcasted_iota(jnp.int32, (LANES, tq), 0) < HEAD_DIM
    streams = []
    for bb in range(bps):
        rows = slice(bb * tq, (bb + 1) * tq)
        keys = [r.at[bb] for r in k_refs]
        for pp in range(pps):
            for h in range(2):
                if dk == HEAD_DIM:
                    lanes = slice(pp * LANES, (pp + 1) * LANES)
                    both = q_ref[rows, lanes].astype(F32).T
                    q_t = (jnp.where(top, both, 0.0) if h == 0 else jnp.where(top, 0.0, both)).astype(BF16)
                else:
                    lanes = slice((2 * pp + h) * LANES, (2 * pp + h + 1) * LANES)
                    q_t = q_ref[rows, lanes].astype(F32).T
                    if rope:
                        n_pe = cos_ref.shape[0]
                        pe = _rope_rows(q_t[HEAD_DIM:HEAD_DIM + n_pe], cos_ref[...], sin_ref[...])
                        q_t = jnp.concatenate([q_t[:HEAD_DIM], pe, q_t[HEAD_DIM + n_pe:]], axis=0)
                    q_t = q_t.astype(BF16)
                streams.append((q_t, lanes, keys))
    outs = _attend_t(streams, [r.at[0] for r in k_refs], vt_ref, s_ref, p_ref, tk)
    for bb in range(bps):
        for pp in range(pps):
            first = 2 * (bb * pps + pp)
            o_ref[bb * tq:(bb + 1) * tq, pp * LANES:(pp + 1) * LANES] = (
                jnp.concatenate(outs[first:first + 2], axis=0).T.astype(o_ref.dtype))


def _pair_attention(q, kv, batch, dk, tq, tables=None, tk=128, pps=1, bps=1):
    t, w = q.shape
    qw = (LANES if dk == HEAD_DIM else 2 * LANES) * pps
    vw = LANES * pps
    groups = w // qw
    s = t // batch
    tq = min(tq, s)
    nq = s // tq
    tk = min([tk] + [k.shape[1] for k, _ in kv])
    assert bps == 1 or nq == 1
    in_specs = [pl.BlockSpec((bps * tq, qw), lambda b, p, i: (b * nq + i, p))]
    args = [q]
    n_total = 0
    for k, v in kv:
        n = k.shape[1]
        n_total += n
        in_specs += [pl.BlockSpec((bps, n, qw), lambda b, p, i: (b, 0, p)),
                     pl.BlockSpec((bps, n, vw), lambda b, p, i: (b, 0, p))]
        args += [k, v]
    if tables is not None:
        blk = pl.BlockSpec((tables[0].shape[0], tq), lambda b, p, i: (0, i))
        in_specs += [blk, blk]
        args += list(tables)
    return pl.pallas_call(
        functools.partial(_pair_kernel, n_src=len(kv), tk=tk, dk=dk, rope=tables is not None),
        grid=(batch // bps, groups, nq),
        in_specs=in_specs,
        out_specs=pl.BlockSpec((bps * tq, vw), lambda b, p, i: (b * nq + i, p)),
        out_shape=jax.ShapeDtypeStruct((t, groups * vw), BF16),
        scratch_shapes=_attn_scratch(n_total, 2 * pps * bps, tk, tq),
        compiler_params=_params(("arbitrary", "arbitrary", "arbitrary")),
        name="pair_attention",
    )(*args)


NB_QROWS = 4
NB_KROWS = 12
NB_TQ = NB_QROWS * GRID_W
NB_CROWS = 4
NB_TK = NB_CROWS * GRID_W
NB_INVALID = 2 * NA_ROWS - 1


def _nb_kernel(q_ref, kc_ref, vc_ref, k_ref, v_ref, tab_ref, o_ref, kcs_ref, vtc_ref, vt_ref, bias_ref,
               s_ref, p_ref):
    i = pl.program_id(2)
    rows = k_ref.shape[0] // GRID_W
    half = NA_ROWS // 2
    a = i * NB_QROWS
    base_blk = jnp.clip(i - half // NB_QROWS, 0, (rows - NB_KROWS) // NB_QROWS)
    base = base_blk * NB_QROWS
    base_chunk = base_blk * (NB_QROWS // NB_CROWS)
    tq = q_ref.shape[0]

    @pl.when(i == 0)
    def _():
        l, width = kcs_ref.shape
        kcs_ref[...] = kc_ref[...].reshape(l, width).astype(kcs_ref.dtype)
        vc = vc_ref[...].reshape(l, width)
        for ref, src in ((vtc_ref, vc), (vt_ref, v_ref)):
            for c in range(ref.shape[1]):
                for pp in range(width // LANES):
                    tiles = _vt_tiles(src[c * NB_TK:(c + 1) * NB_TK, pp * LANES:(pp + 1) * LANES])
                    for h in range(2):
                        ref[2 * pp + h, c] = tiles[h]

    def unclipped(first_row):
        return jnp.logical_and(first_row - half >= 0, first_row + NB_QROWS - 1 - half <= rows - NA_ROWS)

    @pl.when(jnp.logical_not(jnp.logical_and(unclipped(a), unclipped(a - NB_QROWS))))
    def _():
        lo_tile = lax.broadcasted_iota(jnp.int32, (GRID_W, LANES), 1) < HEAD_DIM
        for h in range(bias_ref.shape[0]):
            for j in range(NB_KROWS):
                kr = base + j
                for ip in range(NB_QROWS // 2):
                    halves = []
                    for qi in (2 * ip, 2 * ip + 1):
                        qr = a + qi
                        r0 = jnp.clip(qr - half, 0, rows - NA_ROWS)
                        valid = jnp.logical_and(kr >= r0, kr < r0 + NA_ROWS)
                        halves.append(tab_ref[h, jnp.where(valid, kr - qr + NA_ROWS - 1, NB_INVALID)])
                    bias_ref[h, j * GRID_W:(j + 1) * GRID_W, ip * LANES:(ip + 1) * LANES] = (
                        jnp.where(lo_tile, halves[0], halves[1]))

    top = lax.broadcasted_iota(jnp.int32, (LANES, tq), 0) < HEAD_DIM
    pps = q_ref.shape[1] // LANES
    q_ts = []
    for pp in range(pps):
        both = q_ref[:, pp * LANES:(pp + 1) * LANES].astype(F32).T
        q_ts += [jnp.where(top, both, 0.0).astype(BF16), jnp.where(top, 0.0, both).astype(BF16)]
    pair_lanes = lambda i_: slice((i_ // 2) * LANES, (i_ // 2 + 1) * LANES)
    chunks = []
    for c in range(vtc_ref.shape[1]):
        chunks.append((lambda i_, c=c: kcs_ref[c * NB_TK:(c + 1) * NB_TK, pair_lanes(i_)],
                       lambda i_, c=c: vtc_ref[i_, c], None))
    for c in range(NB_KROWS // NB_CROWS):
        start = pl.multiple_of((base + c * NB_CROWS) * GRID_W, NB_TK)
        chunks.append((lambda i_, start=start: k_ref[pl.ds(start, NB_TK), pair_lanes(i_)].astype(BF16),
                       lambda i_, c=c: vt_ref[i_, base_chunk + c],
                       lambda i_, c=c: bias_ref[i_, c * NB_TK:(c + 1) * NB_TK, :]))
    outs = _pipeline(q_ts, chunks, s_ref, p_ref)
    for pp in range(pps):
        o_ref[:, pp * LANES:(pp + 1) * LANES] = (
            jnp.concatenate(outs[2 * pp:2 * pp + 2], axis=0).T.astype(o_ref.dtype))


def _nb_attention(q, k, v, kctx, vctx, table, batch, pps=4):
    t, w = q.shape
    s = t // batch
    pw = LANES * pps
    groups = w // pw
    heads = 2 * pps
    tq = NB_TQ
    nq = s // tq
    l = kctx.shape[1]
    assert l % NB_TK == 0 and (s // GRID_W) % NB_QROWS == 0 and kctx.shape[2] == heads
    return pl.pallas_call(
        _nb_kernel,
        grid=(batch, groups, nq),
        in_specs=[
            pl.BlockSpec((tq, pw), lambda b, p, i: (b * nq + i, p)),
            pl.BlockSpec((None, l, heads, HEAD_DIM), lambda b, p, i: (b, 0, 0, 0)),
            pl.BlockSpec((None, l, heads, HEAD_DIM), lambda b, p, i: (b, 0, 0, 0)),
            pl.BlockSpec((None, s, pw), lambda b, p, i: (b, 0, p)),
            pl.BlockSpec((None, s, pw), lambda b, p, i: (b, 0, p)),
            pl.BlockSpec((heads, NB_INVALID + 1, GRID_W, LANES), lambda b, p, i: (p, 0, 0, 0)),
        ],
        out_specs=pl.BlockSpec((tq, pw), lambda b, p, i: (b * nq + i, p)),
        out_shape=jax.ShapeDtypeStruct((t, w), BF16),
        scratch_shapes=[pltpu.VMEM((l, pw), BF16),
                        pltpu.VMEM((heads, l // NB_TK, VT_ROWS, NB_TK), BF16),
                        pltpu.VMEM((heads, s // NB_TK, VT_ROWS, NB_TK), BF16),
                        pltpu.VMEM((heads, NB_KROWS * GRID_W, tq), F32),
                        *_pipe_scratch(heads, NB_TK, tq)],
        compiler_params=_params(("arbitrary", "arbitrary", "arbitrary")),
        name="nb_attention",
    )(q, kctx, vctx, k, v, table)


def _nb_bias_table(rpb):
    h, n_dr, n_dc = rpb.shape
    qc = jnp.arange(GRID_W)[None, :]
    kc = jnp.arange(GRID_W)[:, None]
    c0 = jnp.clip(qc - NA_COLS // 2, 0, GRID_W - NA_COLS)
    inwin = jnp.logical_and(kc >= c0, kc < c0 + NA_COLS)
    period = 2 * GRID_W - 1
    rev = rpb.astype(F32)[..., ::-1]
    seq = jnp.concatenate([rev[..., NA_COLS - 1:], jnp.zeros((h, n_dr, period - n_dc), F32),
                           rev[..., :NA_COLS - 1]], axis=-1)
    rel = jnp.tile(seq, (1, 1, GRID_W))[..., :GRID_W * (period - 1)]
    rel = rel.reshape(h, n_dr, GRID_W, period - 1)[..., :GRID_W]
    t = jnp.where(inwin[None, None], rel * LOG2E, NEG)
    t = jnp.concatenate([t, jnp.full((h, 1, GRID_W, GRID_W), NEG, F32)], axis=1)
    return jnp.concatenate([t, t], axis=-1)


PAD = 8


def _window_sum(pad_ref, tmp_ref, lanes, win, tm):
    def rows(lo, n):
        return pad_ref[PAD + lo:PAD + lo + n, lanes]

    if win <= 4:
        acc = rows(-(win // 2), tm)
        for d in range(-(win // 2) + 1, win // 2):
            acc = acc + rows(d, tm)
        return acc
    ext = win // 2 - 2
    n = tm + 2 * ext
    cur = rows(-ext - 2, n)
    for d in (-1, 0, 1):
        cur = cur + rows(-ext + d, n)
    width = 4
    while width < win:
        tmp_ref[0:n, :] = cur
        h = width // 2
        prev_ext, ext = ext, ext - h
        n = tm + 2 * ext
        lo = prev_ext - ext
        cur = tmp_ref[lo - h:lo - h + n, :] + tmp_ref[lo + h:lo + h + n, :]
        width *= 2
    return cur


def _project_out(x_ref, ta, tb, w_ref, g_ref, gate_ref, o_ref):
    half = ta.shape[1]
    y = (jnp.dot(ta, w_ref[0:half, :], preferred_element_type=F32)
         + jnp.dot(tb, w_ref[half:2 * half, :], preferred_element_type=F32))
    o_ref[...] = x_ref[...] + gate_ref[...] * _rms_lanes(y, g_ref[...])


def _out_kernel(x_ref, a_ref, ga_ref, b_ref, gb_ref, w_ref, g_ref, gate_ref, o_ref):
    ta = (a_ref[...].astype(F32) * _silu(ga_ref[...].astype(F32))).astype(BF16)
    tb = (b_ref[...].astype(F32) * _silu(gb_ref[...].astype(F32))).astype(BF16)
    _project_out(x_ref, ta, tb, w_ref, g_ref, gate_ref, o_ref)


def _out_even_kernel(x_ref, a_ref, ga_ref, u_ref, up_ref, un_ref, gb_ref, map_ref, sc_ref, w_ref, g_ref,
                     gate_ref, o_ref, pad_ref, tmp_ref, *, nseq, seq):
    tm = a_ref.shape[0]
    pos = pl.program_id(0) % nseq
    u = u_ref[...]
    pad_ref[PAD:PAD + tm, :] = u
    pad_ref[0:PAD, :] = jnp.where(pos == 0, 0.0, up_ref[...])
    pad_ref[PAD + tm:PAD + tm + PAD, :] = jnp.where(pos == nseq - 1, 0.0, un_ref[...])
    t = pos * tm + lax.broadcasted_iota(jnp.int32, (tm, LANES), 0)
    ys = []
    for g, win in enumerate(POOL_WINDOWS):
        lanes = slice(g * LANES, (g + 1) * LANES)
        acc = _window_sum(pad_ref, tmp_ref, lanes, win, tm)
        cnt = jnp.minimum(t + win // 2, seq) - jnp.maximum(t - win // 2, 0)
        diff = acc / cnt.astype(F32) - u[:, lanes]
        ys.append(jnp.dot(diff.astype(BF16), map_ref[g].astype(BF16), preferred_element_type=F32))
    o_b = jnp.concatenate(ys, axis=1) * sc_ref[...]
    ta = (a_ref[...].astype(F32) * _silu(ga_ref[...].astype(F32))).astype(BF16)
    tb = (o_b * _silu(gb_ref[...].astype(F32))).astype(BF16)
    _project_out(x_ref, ta, tb, w_ref, g_ref, gate_ref, o_ref)


def _out_even(x, a, ga, u, gb, b_map, b_scale, w, g_post, gate, rows_per_mod, seq, tm=512):
    t, d = x.shape
    half = a.shape[1]
    tm = min(tm, seq)
    nseq = seq // tm
    br = pl.BlockSpec((tm, half), lambda i: (i, 0))
    halo_blocks = tm // PAD
    prev = pl.BlockSpec((PAD, half), lambda i: (jnp.maximum(i * halo_blocks - 1, 0), 0))
    nxt = pl.BlockSpec((PAD, half), lambda i: (jnp.minimum((i + 1) * halo_blocks, t // PAD - 1), 0))
    full = lambda arr: pl.BlockSpec(arr.shape, lambda i: (0,) * arr.ndim)
    sc = b_scale.reshape(1, half)
    gate, gate_spec = _mod_operand(gate, d, tm, rows_per_mod)
    return pl.pallas_call(
        functools.partial(_out_even_kernel, nseq=nseq, seq=seq),
        grid=(t // tm,),
        in_specs=[
            pl.BlockSpec((tm, d), lambda i: (i, 0)),
            br, br, br, prev, nxt, br, full(b_map), full(sc), full(w), full(g_post), gate_spec,
        ],
        out_specs=pl.BlockSpec((tm, d), lambda i: (i, 0)),
        out_shape=jax.ShapeDtypeStruct((t, d), F32),
        scratch_shapes=[pltpu.VMEM((tm + 2 * PAD, half), F32), pltpu.VMEM((tm + 2 * PAD, LANES), F32)],
        compiler_params=_params(("arbitrary",)),
        name="out_proj_even",
    )(x, a, ga, u, u, u, gb, b_map, sc, w, g_post, gate)


def _out_proj(x, a, ga, b, gb, w, g_post, gate, rows_per_mod, tm=512):
    t, d = x.shape
    half = a.shape[1]
    tm = min(tm, t)
    br = pl.BlockSpec((tm, half), lambda i: (i, 0))
    gate, gate_spec = _mod_operand(gate, d, tm, rows_per_mod)
    return pl.pallas_call(
        _out_kernel,
        grid=(t // tm,),
        in_specs=[
            pl.BlockSpec((tm, d), lambda i: (i, 0)),
            br, br, br, br,
            pl.BlockSpec((2 * half, d), lambda i: (0, 0)),
            pl.BlockSpec((1, d), lambda i: (0, 0)),
            gate_spec,
        ],
        out_specs=pl.BlockSpec((tm, d), lambda i: (i, 0)),
        out_shape=jax.ShapeDtypeStruct((t, d), F32),
        compiler_params=_params(("arbitrary",)),
        name="out_proj",
    )(x, a, ga, b, gb, w, g_post, gate)


def _kvup_kernel(c_ref, pe_ref, wk_ref, wv_ref, k_ref, v_ref):
    cb = c_ref[...].astype(BF16)
    heads = k_ref.shape[1] // LANES
    k = jnp.dot(cb, wk_ref[...], preferred_element_type=F32)
    k = k + jnp.concatenate([pe_ref[...]] * heads, axis=1)
    k_ref[...] = k.astype(k_ref.dtype)
    v_ref[...] = jnp.dot(cb, wv_ref[...], preferred_element_type=F32).astype(v_ref.dtype)


def _kvup(ckv, pe128, wk, wv, tm=512):
    t, r = ckv.shape
    tm = min(tm, t)
    nk, nv = wk.shape[1], wv.shape[1]
    row = lambda width: pl.BlockSpec((tm, width), lambda i: (i, 0))
    full = lambda a: pl.BlockSpec(a.shape, lambda i: (0, 0))
    return pl.pallas_call(
        _kvup_kernel,
        grid=(t // tm,),
        in_specs=[row(r), row(LANES), full(wk), full(wv)],
        out_specs=[row(nk), row(nv)],
        out_shape=[jax.ShapeDtypeStruct((t, nk), BF16), jax.ShapeDtypeStruct((t, nv), BF16)],
        compiler_params=_params(("arbitrary",)),
        name="kv_up",
    )(ckv, pe128, wk, wv)


def kernel(x_prompt, x_sample, cache_a_k, cache_a_v, cache_c_k, cache_c_v, cache_d_ckv, cache_d_kpe,
           c, c_ctx, w_mod, b_mod, g_pre, g_post, w_in_e, a_q_norm, a_k_norm, b_map, b_scale, w_out_e,
           w_in_o, c_rpb, d_q_norm, d_w_uq, d_kv_norm, d_w_ukv, w_out_o):
    bp, sp, d = x_prompt.shape
    bs, ss, _ = x_sample.shape
    past = cache_a_k.shape[2]
    depth = w_mod.shape[0]
    tp, ts = bp * sp, bs * ss
    branch = d // 2
    d_heads = 8
    d_nope, d_rope, d_v = 64, 32, 64
    dqk = d_nope + d_rope
    q_lora, kv_lora = d_q_norm.shape[1], d_kv_norm.shape[1]
    kv_w = a_k_norm.shape[1] * 2

    xp = x_prompt.reshape(tp, d)
    xs = x_sample.reshape(ts, d)

    cond = jnp.concatenate([c_ctx[None, :], c, jnp.zeros((8 - 1 - bs, d), F32)], axis=0)
    mods = _mod_call(cond, w_mod, b_mod)

    mod_table = mods.reshape(depth * cond.shape[0] * MOD_PARTS, 1, d)

    def mod_rows(layer, part, lo_row):
        return mod_table, layer * cond.shape[0] + lo_row, part

    rope_a = _axial_tables_rows(ss, HEAD_DIM)
    rope_d = _axial_tables_rows(ss, d_rope)
    rope_d_lanes = _axial_tables_lanes(ss, d_rope, d_nope)

    outs = {name: [] for name in ("a_k", "a_v", "c_k", "c_v", "d_ckv", "d_kpe")}
    for layer in range(depth):
        j = layer // 2
        g_in = g_pre[layer].reshape(1, d)
        g_out = g_post[layer].reshape(1, d)
        paths = (("p", xp, bp, sp, 0, 1, tp), ("s", xs, bs, ss, 1, bs, ss))
        if layer % 2 == 0:
            w_in = w_in_e[j].astype(BF16)
            w_out = w_out_e[j].astype(BF16)
            o0 = branch
            new_x = []
            for name, x, nb, seq, row0, nrows, rpm in paths:
                shift, scale, gate = (mod_rows(layer, part, row0) for part in range(MOD_PARTS))
                v_dt = F32 if name == "p" else BF16
                segs = [(0, branch, F32), (o0, kv_w, F32), (o0 + kv_w, kv_w, v_dt),
                        (o0 + 2 * kv_w, branch, BF16), (o0 + 2 * kv_w + branch, branch, F32),
                        (o0 + 2 * kv_w + 2 * branch, branch, BF16)]
                q, k, v, ga, ub, gb = _norm_mm(x, g_in, scale, shift, w_in, segs, rpm, name="in_proj_even")
                k3, v3 = k.reshape(nb, seq, kv_w), v.reshape(nb, seq, kv_w)
                if name == "p":
                    o_a, kn, vn = _gqa_attention(q, [(k3, v3)], a_q_norm[j], a_k_norm[j], None, nb, tq=256,
                                                 emit_k=True)
                    outs["a_k"].append(kn)
                    outs["a_v"].append(vn)
                else:
                    kv = [(cache_a_k[:, j], cache_a_v[:, j]), (k3, v3)]
                    o_a = _gqa_attention(q, kv, a_q_norm[j], a_k_norm[j], rope_a, nb, tq=256, emit_k=False)
                new_x.append(_out_even(x, o_a, ga, ub, gb, b_map[j], b_scale[j], w_out, g_out, gate, rpm, seq))
            xp, xs = new_x
        else:
            wi = w_in_o[j].astype(BF16)
            off_kpe = 4 * branch + q_lora + kv_lora
            w_in = jnp.concatenate([wi[:, :off_kpe], jnp.zeros((d, d_nope), BF16),
                                    wi[:, off_kpe:off_kpe + d_rope],
                                    jnp.zeros((d, LANES - dqk), BF16),
                                    wi[:, off_kpe + d_rope:]], axis=1)
            w_out = w_out_o[j].astype(BF16)
            w_uq = jnp.pad(d_w_uq[j].reshape(q_lora, d_heads, dqk),
                           ((0, 0), (0, 0), (0, LANES - dqk))).reshape(q_lora, d_heads * LANES).astype(BF16)
            ukv = d_w_ukv[j].reshape(kv_lora, d_heads, d_nope + d_v)
            w_uk = jnp.pad(ukv[:, :, :d_nope], ((0, 0), (0, 0), (0, LANES - d_nope))
                           ).reshape(kv_lora, d_heads * LANES).astype(BF16)
            w_uv = ukv[:, :, d_nope:].reshape(kv_lora, d_heads * d_v).astype(BF16)
            qn_g = d_q_norm[j].reshape(1, q_lora)
            kvn_g = d_kv_norm[j].reshape(1, kv_lora)
            table = _nb_bias_table(c_rpb[j])
            new_x = []
            for name, x, nb, seq, row0, nrows, rpm in paths:
                shift, scale, gate = (mod_rows(layer, part, row0) for part in range(MOD_PARTS))
                latent = name == "s"
                qc, kc, vc, gc, gd, qd, ckv_n, kpe, k_d, v_d, *cache_kv = _in_odd(
                    x, g_in, scale, shift, w_in, qn_g, kvn_g, w_uq, w_uk, w_uv,
                    rope_d_lanes, rpm, seq, d_rope // 4 if latent else 0,
                    not latent, F32 if latent else BF16, branch,
                    LOG2E * HEAD_DIM ** -0.5, LOG2E * dqk ** -0.5)
                kc3, vc3 = kc.reshape(nb, seq, branch), vc.reshape(nb, seq, branch)
                kd3, vd3 = k_d.reshape(nb, seq, -1), v_d.reshape(nb, seq, -1)
                if not latent:
                    outs["c_k"].append(cache_kv[0])
                    outs["c_v"].append(cache_kv[1])
                    outs["d_ckv"].append(ckv_n)
                    outs["d_kpe"].append(kpe[:, d_nope:dqk])
                    all_pairs = branch // LANES
                    o_c = _pair_attention(qc, [(kc3, vc3)], nb, HEAD_DIM, tq=256, pps=all_pairs, bps=2)
                    o_d = _pair_attention(qd, [(kd3, vd3)], nb, 2 * HEAD_DIM, tq=256, pps=all_pairs, bps=2)
                else:
                    o_c = _nb_attention(qc, kc3, vc3, cache_c_k[:, j], cache_c_v[:, j], table, nb)
                    pe_ctx = jnp.pad(cache_d_kpe[:, j].reshape(nb * past, d_rope),
                                     ((0, 0), (d_nope, LANES - dqk)))
                    k_ctx, v_ctx = _kvup(cache_d_ckv[:, j].reshape(nb * past, kv_lora), pe_ctx, w_uk, w_uv)
                    kv = [(k_ctx.reshape(nb, past, -1), v_ctx.reshape(nb, past, -1)), (kd3, vd3)]
                    o_d = _pair_attention(qd, kv, nb, 2 * HEAD_DIM, tq=512, tables=rope_d, pps=2)
                new_x.append(_out_proj(x, o_c, gc, o_d, gd, w_out, g_out, gate, rpm))
            xp, xs = new_x

    kvh = kv_w // HEAD_DIM
    ch = branch // HEAD_DIM

    def stacked(name, *tail):
        return jnp.stack([a.reshape(bp, sp, *tail) for a in outs[name]], axis=1)

    return (xp.reshape(bp, sp, d), xs.reshape(bs, ss, d),
            stacked("a_k", kvh, HEAD_DIM), stacked("a_v", kvh, HEAD_DIM),
            stacked("c_k", ch, HEAD_DIM), stacked("c_v", ch, HEAD_DIM),
            stacked("d_ckv", kv_lora), stacked("d_kpe", d_rope))
```

```python
import functools

import jax
import jax.numpy as jnp
from jax import lax
from jax.experimental import pallas as pl
from jax.experimental.pallas import tpu as pltpu

F32 = jnp.float32
BF16 = jnp.bfloat16

LANES = 128
GRID_W = 64
HEAD_DIM = 64
NA_ROWS = 8
NA_COLS = 16
POOL_WINDOWS = (2, 4, 8, 16)
ROPE_THETA = 10000.0
EPS = 1e-6
NEG = -1e30
LOG2E = 1.4426950408889634
VMEM_LIMIT = 56 * 1024 * 1024

def _params(sem):
    return pltpu.CompilerParams(dimension_semantics=sem, vmem_limit_bytes=VMEM_LIMIT)


def _silu(x):
    h = 0.5 * x
    return h + h * jnp.tanh(h)


def _rms_lanes(x, gain):
    ms = jnp.mean(x * x, axis=-1, keepdims=True)
    return x * lax.rsqrt(ms + EPS) * gain


MOD_PARTS = 3


def _mod_operand(mod, width, tm, rows_per_mod):
    table, row0, part = mod
    index = lambda i: ((row0 + (i * tm) // rows_per_mod) * MOD_PARTS + part, 0, 0)
    return table, pl.BlockSpec((None, 1, width), index)


def _mod_kernel(c_ref, w_ref, b_ref, o_ref):
    s = _silu(c_ref[...]).astype(BF16)
    o_ref[...] = jnp.dot(s, w_ref[...].astype(BF16), preferred_element_type=F32) + b_ref[...]


def _mod_call(cond, w_mod, b_mod):
    depth, d, n = w_mod.shape
    rows = cond.shape[0]
    tn = 1024
    return pl.pallas_call(
        _mod_kernel,
        grid=(depth, n // tn),
        in_specs=[
            pl.BlockSpec((rows, d), lambda l, j: (0, 0)),
            pl.BlockSpec((None, d, tn), lambda l, j: (l, 0, j)),
            pl.BlockSpec((None, 1, tn), lambda l, j: (l, 0, j)),
        ],
        out_specs=pl.BlockSpec((None, rows, tn), lambda l, j: (l, 0, j)),
        out_shape=jax.ShapeDtypeStruct((depth, rows, n), F32),
        compiler_params=_params(("arbitrary", "arbitrary")),
        name="adaln_mod",
    )(cond, w_mod, b_mod.reshape(depth, 1, n))


def _norm_mm_kernel(x_ref, g_ref, sc_ref, sh_ref, w_ref, *o_refs, segs, do_norm):
    x = x_ref[...].astype(F32)
    if do_norm:
        x = _rms_lanes(x, g_ref[...]) * (1.0 + sc_ref[...]) + sh_ref[...]
    z = jnp.dot(x.astype(BF16), w_ref[...], preferred_element_type=F32)
    for o_ref, (start, width) in zip(o_refs, segs):
        o_ref[...] = z[:, start:start + width].astype(o_ref.dtype)


def _norm_mm(x, g, scale, shift, w, segs, rows_per_mod, do_norm=True, tm=512, name="norm_mm"):
    t, k = x.shape
    n = w.shape[1]
    tm = min(tm, t)
    kern = functools.partial(_norm_mm_kernel, segs=tuple((s, wd) for s, wd, _ in segs), do_norm=do_norm)
    scale, scale_spec = _mod_operand(scale, k, tm, rows_per_mod)
    shift, shift_spec = _mod_operand(shift, k, tm, rows_per_mod)
    return pl.pallas_call(
        kern,
        grid=(t // tm,),
        in_specs=[
            pl.BlockSpec((tm, k), lambda i: (i, 0)),
            pl.BlockSpec((1, k), lambda i: (0, 0)),
            scale_spec,
            shift_spec,
            pl.BlockSpec((k, n), lambda i: (0, 0)),
        ],
        out_specs=[pl.BlockSpec((tm, wd), lambda i: (i, 0)) for _, wd, _ in segs],
        out_shape=[jax.ShapeDtypeStruct((t, wd), dt) for _, wd, dt in segs],
        compiler_params=_params(("arbitrary",)),
        name=name,
    )(x, g, scale, shift, w)


def _in_odd_kernel(x_ref, g_ref, sc_ref, sh_ref, w_ref, qg_ref, kvg_ref, wuq_ref, wuk_ref, wuv_ref, *refs,
                   branch, q_lora, kv_lora, rope_shift, qc_mul, qd_mul):
    if rope_shift:
        cos_ref, sa_ref, sb_ref = refs[:3]
        refs = refs[3:]
    qc_ref, kc_ref, vc_ref, gc_ref, gd_ref, qd_ref, ckv_ref, kpe_ref, kd_ref, vd_ref = refs[:10]
    cache_refs = refs[10:]
    x = _rms_lanes(x_ref[...], g_ref[...]) * (1.0 + sc_ref[...]) + sh_ref[...]
    z = jnp.dot(x.astype(BF16), w_ref[...], preferred_element_type=F32)
    b = branch
    qc_ref[...] = (z[:, 0:b] * qc_mul).astype(qc_ref.dtype)
    kc_ref[...] = z[:, b:2 * b].astype(kc_ref.dtype)
    vc_ref[...] = z[:, 2 * b:3 * b].astype(vc_ref.dtype)
    gc_ref[...] = z[:, 3 * b:4 * b].astype(gc_ref.dtype)
    if cache_refs:
        _store_heads(cache_refs[0], slice(None), z[:, b:2 * b])
        _store_heads(cache_refs[1], slice(None), z[:, 2 * b:3 * b])
    o = 4 * b
    cq = _rms_lanes(z[:, o:o + q_lora], qg_ref[...])
    qd = jnp.dot(cq.astype(BF16), wuq_ref[...], preferred_element_type=F32)
    qd_ref[...] = (qd * qd_mul).astype(qd_ref.dtype)
    o += q_lora
    ckv = _rms_lanes(z[:, o:o + kv_lora], kvg_ref[...])
    ckv_ref[...] = ckv
    o += kv_lora
    kpe = z[:, o:o + LANES]
    if rope_shift:
        kpe = (kpe * cos_ref[...] + pltpu.roll(kpe, LANES - rope_shift, 1) * sa_ref[...]
               + pltpu.roll(kpe, rope_shift, 1) * sb_ref[...])
    kpe_ref[...] = kpe
    o += LANES
    gd_ref[...] = z[:, o:o + b].astype(gd_ref.dtype)
    cb = ckv.astype(BF16)
    heads = kd_ref.shape[1] // LANES
    kd = jnp.dot(cb, wuk_ref[...], preferred_element_type=F32) + jnp.concatenate([kpe] * heads, axis=1)
    kd_ref[...] = kd.astype(kd_ref.dtype)
    vd_ref[...] = jnp.dot(cb, wuv_ref[...], preferred_element_type=F32).astype(vd_ref.dtype)


def _in_odd(x, g, scale, shift, w, qg, kvg, wuq, wuk, wuv, tables, rows_per_mod, seq, rope_shift,
            emit_cache, qd_dtype, branch, qc_mul, qd_mul, tm=512):
    t, k = x.shape
    tm = min(tm, seq if rope_shift else t)
    nseq = max(seq // tm, 1)
    q_lora, kv_lora = qg.shape[1], kvg.shape[1]
    widths = [(branch, BF16), (branch, BF16), (branch, BF16), (branch, BF16), (branch, BF16),
              (wuq.shape[1], qd_dtype), (kv_lora, F32), (LANES, F32), (wuk.shape[1], BF16), (wuv.shape[1], BF16)]
    out_specs = [pl.BlockSpec((tm, wd), lambda i: (i, 0)) for wd, _ in widths]
    out_shape = [jax.ShapeDtypeStruct((t, wd), dt) for wd, dt in widths]
    if emit_cache:
        heads = branch // HEAD_DIM
        out_specs += [pl.BlockSpec((tm, heads, HEAD_DIM), lambda i: (i, 0, 0))] * 2
        out_shape += [jax.ShapeDtypeStruct((t, heads, HEAD_DIM), F32)] * 2
    scale, scale_spec = _mod_operand(scale, k, tm, rows_per_mod)
    shift, shift_spec = _mod_operand(shift, k, tm, rows_per_mod)
    full = lambda a: pl.BlockSpec(a.shape, lambda i: (0, 0))
    tab_spec = pl.BlockSpec((tm, LANES), lambda i: (i % nseq, 0))
    kern = functools.partial(_in_odd_kernel, branch=branch, q_lora=q_lora, kv_lora=kv_lora,
                             rope_shift=rope_shift, qc_mul=qc_mul, qd_mul=qd_mul)
    return pl.pallas_call(
        kern,
        grid=(t // tm,),
        in_specs=[pl.BlockSpec((tm, k), lambda i: (i, 0)), full(g),
                  scale_spec, shift_spec,
                  full(w), full(qg), full(kvg), full(wuq), full(wuk), full(wuv)]
        + ([tab_spec] * 3 if rope_shift else []),
        out_specs=out_specs,
        out_shape=out_shape,
        compiler_params=_params(("arbitrary",)),
        name="in_proj_odd",
    )(x, g, scale, shift, w, qg, kvg, wuq, wuk, wuv, *(tables if rope_shift else ()))


def _axial_angles(seq, dims):
    half = dims // 2
    inv = ROPE_THETA ** (-jnp.arange(0, half, 2, dtype=F32) / half)
    t = jnp.arange(seq)
    ang_r = (t // GRID_W).astype(F32)[:, None] * inv
    ang_c = (t % GRID_W).astype(F32)[:, None] * inv
    return ang_r, ang_c


def _axial_tables_lanes(seq, dims, lane0):
    ang_r, ang_c = _axial_angles(seq, dims)
    cos = jnp.concatenate([jnp.cos(ang_r)] * 2 + [jnp.cos(ang_c)] * 2, axis=-1)
    sin = jnp.concatenate([jnp.sin(ang_r)] * 2 + [jnp.sin(ang_c)] * 2, axis=-1)
    first = (jnp.arange(dims) % (dims // 2)) < dims // 4
    sa = jnp.where(first, -sin, 0.0)
    sb = jnp.where(first, 0.0, sin)
    pad_l, pad_r = lane0, LANES - lane0 - dims
    return (jnp.pad(cos, ((0, 0), (pad_l, pad_r)), constant_values=1.0),
            jnp.pad(sa, ((0, 0), (pad_l, pad_r))), jnp.pad(sb, ((0, 0), (pad_l, pad_r))))


def _axial_tables_rows(seq, dims):
    ang_r, ang_c = _axial_angles(seq, dims)
    cos = jnp.concatenate([jnp.cos(ang_r)] * 2 + [jnp.cos(ang_c)] * 2, axis=-1).T
    sin = jnp.concatenate([-jnp.sin(ang_r), jnp.sin(ang_r), -jnp.sin(ang_c), jnp.sin(ang_c)], axis=-1).T
    return cos, sin


def _rms_rows(x, gain):
    ms = jnp.mean(x * x, axis=0, keepdims=True)
    return x * lax.rsqrt(ms + EPS) * gain


def _rope_rows(x, cos, sin):
    q = x.shape[0] // 4
    partner = jnp.concatenate([x[q:2 * q], x[0:q], x[3 * q:4 * q], x[2 * q:3 * q]], axis=0)
    return x * cos + partner * sin


def _attend_t(streams, k_refs, vt_ref, s_ref, p_ref, tk):
    chunks = []
    off = 0
    for src, k_ref in enumerate(k_refs):
        n = k_ref.shape[0]
        assert n % tk == 0
        for c in range(n // tk):
            def k_fn(i, src=src, r=c * tk):
                ref = streams[i][2][src] if len(streams[i]) > 2 else k_refs[src]
                return ref[r:r + tk, streams[i][1]].astype(BF16)
            vt_fn = lambda i, col=off + c * tk: vt_ref[i, :, col:col + tk]
            chunks.append((k_fn, vt_fn, None))
        off += n
    return _pipeline([stream[0] for stream in streams], chunks, s_ref, p_ref)


def _pipeline(q_ts, chunks, s_ref, p_ref):
    nc = len(chunks)
    ns = len(q_ts)
    mq = q_ts[0].shape[1]
    m = [jnp.full((1, mq), -jnp.inf, F32) for _ in range(ns)]
    acc = [jnp.zeros((VT_ROWS, mq), F32) for _ in range(ns)]
    m_chunk = [None] * ns
    alpha = [[None] * ns for _ in range(2)]
    for t in range(nc + 2):
        if t < nc:
            k_fn, _, bias_fn = chunks[t]
            for i, q_t in enumerate(q_ts):
                s = jnp.dot(k_fn(i), q_t, preferred_element_type=F32)
                if bias_fn is not None:
                    s = s + bias_fn(i)
                m_chunk[i] = jnp.max(s, axis=0, keepdims=True)
                s_ref[(t % 2) * ns + i] = s
        if 0 <= t - 2 < nc:
            for i in range(ns):
                pv = jnp.dot(chunks[t - 2][1](i), p_ref[(t % 2) * ns + i], preferred_element_type=F32)
                acc[i] = alpha[t % 2][i] * acc[i] + pv
        if 0 <= t - 1 < nc:
            for i in range(ns):
                m_new = jnp.maximum(m[i], m_prev_chunk[i])
                alpha[(t - 1) % 2][i] = jnp.exp2(m[i] - m_new)
                p = jnp.exp2(s_ref[((t - 1) % 2) * ns + i] - m_new)
                p_ref[((t - 1) % 2) * ns + i] = p.astype(BF16)
                m[i] = m_new
        m_prev_chunk = list(m_chunk)
    return [acc[i][:HEAD_DIM] / acc[i][HEAD_DIM:HEAD_DIM + 1] for i in range(ns)]


VT_ROWS = HEAD_DIM + 16

def _vt_tiles(blk):
    n = blk.shape[0]
    v_t = blk.astype(F32).T
    tail = (lax.broadcasted_iota(jnp.int32, (VT_ROWS - HEAD_DIM, n), 0) == 0).astype(F32)
    return [jnp.concatenate([v_t[h * HEAD_DIM:(h + 1) * HEAD_DIM], tail], axis=0).astype(BF16)
            for h in range(2)]


def _fill_vt(vt_ref, v_refs, chunk=512, base=0):
    off = 0
    for v_ref in v_refs:
        n = v_ref.shape[0]
        step = min(chunk, n)
        for c in range(n // step):
            for pp in range(v_ref.shape[1] // LANES):
                tiles = _vt_tiles(v_ref[c * step:(c + 1) * step, pp * LANES:(pp + 1) * LANES])
                for h in range(2):
                    vt_ref[base + 2 * pp + h, :, off + c * step:off + (c + 1) * step] = tiles[h]
        off += n


def _pipe_scratch(n_streams, tk, mq):
    return [pltpu.VMEM((2 * n_streams, tk, mq), F32), pltpu.VMEM((2 * n_streams, tk, mq), BF16)]


def _attn_scratch(n_total, n_streams, tk, mq):
    return [pltpu.VMEM((n_streams, VT_ROWS, n_total), BF16)] + _pipe_scratch(n_streams, tk, mq)


def _store_heads(ref, rows, x):
    ref[rows] = x.astype(ref.dtype).reshape(x.shape[0], ref.shape[1], HEAD_DIM)


def _gqa_kernel(*refs, n_src, tk, rope, emit_k, cache_src, q_mul):
    q_ref = refs[0]
    k_refs = [refs[1 + 2 * i] for i in range(n_src)]
    v_refs = [refs[2 + 2 * i] for i in range(n_src)]
    pos = 1 + 2 * n_src
    qg_ref, kg_ref = refs[pos], refs[pos + 1]
    pos += 2
    if rope:
        cosq_ref, sinq_ref, cosk_ref, sink_ref = refs[pos:pos + 4]
        pos += 4
    o_ref = refs[pos]
    pos += 1
    if emit_k:
        kn_ref, vn_ref = refs[pos:pos + 2]
        pos += 2
    if cache_src:
        kcs_ref = refs[pos]
        pos += 1
    kp_ref, vt_ref, s_ref, p_ref = refs[pos:pos + 4]
    tq = q_ref.shape[0]

    @pl.when(pl.program_id(1) == 0)
    def _():
        v_srcs = list(v_refs)
        if cache_src:
            kcs_ref[...] = k_refs[0][...].reshape(kcs_ref.shape).astype(kcs_ref.dtype)
            v_srcs[0] = v_refs[0][...].reshape(kcs_ref.shape)
        _fill_vt(vt_ref, v_srcs)
        raw = k_refs[-1]
        for c in range(raw.shape[0] // tk):
            rows = slice(c * tk, (c + 1) * tk)
            k_t = raw[rows, :].astype(F32).T
            halves = []
            for hh in range(2):
                x = _rms_rows(k_t[hh * HEAD_DIM:(hh + 1) * HEAD_DIM], kg_ref[...])
                if rope:
                    x = _rope_rows(x, cosk_ref[:, rows], sink_ref[:, rows])
                halves.append(x)
            kn = jnp.concatenate(halves, axis=0).T
            kp_ref[rows, :] = kn.astype(kp_ref.dtype)
            if emit_k:
                _store_heads(kn_ref, rows, kn)
                _store_heads(vn_ref, rows, v_refs[-1][rows, :])

    zeros = jnp.zeros((HEAD_DIM, tq), F32)
    parts = []
    for j in range(4):
        q_t = q_ref[:, j * LANES:(j + 1) * LANES].astype(F32).T
        for hh in range(2):
            x = _rms_rows(q_t[hh * HEAD_DIM:(hh + 1) * HEAD_DIM], qg_ref[...])
            if rope:
                x = _rope_rows(x, cosq_ref[...], sinq_ref[...])
            parts.append(x * q_mul)
    streams = []
    for g in range(2):
        cols = [jnp.concatenate([parts[4 * g + n], zeros] if g == 0 else [zeros, parts[4 * g + n]], axis=0)
                for n in range(4)]
        streams.append((jnp.concatenate(cols, axis=1).astype(BF16), slice(None)))
    given = [kcs_ref] if cache_src else k_refs[:-1]
    outs = _attend_t(streams, given + [kp_ref], vt_ref, s_ref, p_ref, tk)
    for g in range(2):
        o_t = outs[g]
        for idx, j in enumerate((2 * g, 2 * g + 1)):
            blk_t = jnp.concatenate([o_t[:, (2 * idx) * tq:(2 * idx + 1) * tq],
                                     o_t[:, (2 * idx + 1) * tq:(2 * idx + 2) * tq]], axis=0)
            o_ref[:, j * LANES:(j + 1) * LANES] = blk_t.T.astype(o_ref.dtype)


def _gqa_attention(q, kv, q_gain, k_gain, tables, batch, tq, emit_k, tk=128):
    t, w = q.shape
    s = t // batch
    tq = min(tq, s)
    nq = s // tq
    tk = min([tk] + [k.shape[1] for k, _ in kv])
    in_specs = [pl.BlockSpec((tq, w), lambda b, i: (b * nq + i, 0))]
    args = [q]
    n_total = 0
    cache_rows = 0
    for k, v in kv:
        n = k.shape[1]
        n_total += n
        if k.ndim == 4:
            cache_rows = n
            spec = pl.BlockSpec((None,) + k.shape[1:], lambda b, i: (b, 0, 0, 0))
        else:
            spec = pl.BlockSpec((None, n, LANES), lambda b, i: (b, 0, 0))
        in_specs += [spec, spec]
        args += [k, v]
    assert cache_rows == 0 or (len(kv) == 2 and kv[0][0].ndim == 4)
    n_raw = kv[-1][0].shape[1]
    const = lambda a: pl.BlockSpec(a.shape, lambda b, i: (0, 0))
    qg = jnp.broadcast_to(q_gain[:, None], (HEAD_DIM, tq))
    kg = jnp.broadcast_to(k_gain[:, None], (HEAD_DIM, tk))
    in_specs += [const(qg), const(kg)]
    args += [qg, kg]
    if tables is not None:
        cos, sin = tables
        blk = pl.BlockSpec((HEAD_DIM, tq), lambda b, i: (0, i))
        in_specs += [blk, blk, const(cos), const(sin)]
        args += [cos, sin, cos, sin]
    out_specs = [pl.BlockSpec((tq, w), lambda b, i: (b * nq + i, 0))]
    out_shape = [jax.ShapeDtypeStruct((t, w), BF16)]
    if emit_k:
        out_specs += [pl.BlockSpec((n_raw, 2, HEAD_DIM), lambda b, i: (b, 0, 0))] * 2
        out_shape += [jax.ShapeDtypeStruct((batch * n_raw, 2, HEAD_DIM), F32)] * 2
    res = pl.pallas_call(
        functools.partial(_gqa_kernel, n_src=len(kv), tk=tk, rope=tables is not None, emit_k=emit_k,
                          cache_src=cache_rows > 0, q_mul=LOG2E * HEAD_DIM ** -0.5),
        grid=(batch, nq),
        in_specs=in_specs,
        out_specs=out_specs,
        out_shape=out_shape,
        scratch_shapes=([pltpu.VMEM((cache_rows, LANES), BF16)] if cache_rows else [])
        + [pltpu.VMEM((n_raw, LANES), BF16)] + _attn_scratch(n_total, 2, tk, 4 * tq),
        compiler_params=_params(("arbitrary", "arbitrary")),
        name="gqa_attention",
    )(*args)
    return res if emit_k else res[0]


def _pair_kernel(*refs, n_src, tk, dk, rope):
    q_ref = refs[0]
    k_refs = [refs[1 + 2 * i] for i in range(n_src)]
    v_refs = [refs[2 + 2 * i] for i in range(n_src)]
    pos = 1 + 2 * n_src
    if rope:
        cos_ref, sin_ref = refs[pos:pos + 2]
        pos += 2
    o_ref, vt_ref, s_ref, p_ref = refs[pos:pos + 4]
    bps = k_refs[0].shape[0]
    tq = q_ref.shape[0] // bps
    pps = o_ref.shape[1] // LANES

    @pl.when(pl.program_id(2) == 0)
    def _():
        for bb in range(bps):
            _fill_vt(vt_ref, [r.at[bb] for r in v_refs], base=bb * 2 * pps)

    top = lax.broadcasted_iota(jnp.int32, (LANES, tq), 0) < HEAD_DIM
    streams = []
    for bb in range(bps):
        rows = slice(bb * tq, (bb + 1) * tq)
        keys = [r.at[bb] for r in k_refs]
        for pp in range(pps):
            for h in range(2):
                if dk == HEAD_DIM:
                    lanes = slice(pp * LANES, (pp + 1) * LANES)
                    both = q_ref[rows, lanes].astype(F32).T
                    q_t = (jnp.where(top, both, 0.0) if h == 0 else jnp.where(top, 0.0, both)).astype(BF16)
                else:
                    lanes = slice((2 * pp + h) * LANES, (2 * pp + h + 1) * LANES)
                    q_t = q_ref[rows, lanes].astype(F32).T
                    if rope:
                        n_pe = cos_ref.shape[0]
                        pe = _rope_rows(q_t[HEAD_DIM:HEAD_DIM + n_pe], cos_ref[...], sin_ref[...])
                        q_t = jnp.concatenate([q_t[:HEAD_DIM], pe, q_t[HEAD_DIM + n_pe:]], axis=0)
                    q_t = q_t.astype(BF16)
                streams.append((q_t, lanes, keys))
    outs = _attend_t(streams, [r.at[0] for r in k_refs], vt_ref, s_ref, p_ref, tk)
    for bb in range(bps):
        for pp in range(pps):
            first = 2 * (bb * pps + pp)
            o_ref[bb * tq:(bb + 1) * tq, pp * LANES:(pp + 1) * LANES] = (
                jnp.concatenate(outs[first:first + 2], axis=0).T.astype(o_ref.dtype))


def _pair_attention(q, kv, batch, dk, tq, tables=None, tk=128, pps=1, bps=1):
    t, w = q.shape
    qw = (LANES if dk == HEAD_DIM else 2 * LANES) * pps
    vw = LANES * pps
    groups = w // qw
    s = t // batch
    tq = min(tq, s)
    nq = s // tq
    tk = min([tk] + [k.shape[1] for k, _ in kv])
    assert bps == 1 or nq == 1
    in_specs = [pl.BlockSpec((bps * tq, qw), lambda b, p, i: (b * nq + i, p))]
    args = [q]
    n_total = 0
    for k, v in kv:
        n = k.shape[1]
        n_total += n
        in_specs += [pl.BlockSpec((bps, n, qw), lambda b, p, i: (b, 0, p)),
                     pl.BlockSpec((bps, n, vw), lambda b, p, i: (b, 0, p))]
        args += [k, v]
    if tables is not None:
        blk = pl.BlockSpec((tables[0].shape[0], tq), lambda b, p, i: (0, i))
        in_specs += [blk, blk]
        args += list(tables)
    return pl.pallas_call(
        functools.partial(_pair_kernel, n_src=len(kv), tk=tk, dk=dk, rope=tables is not None),
        grid=(batch // bps, groups, nq),
        in_specs=in_specs,
        out_specs=pl.BlockSpec((bps * tq, vw), lambda b, p, i: (b * nq + i, p)),
        out_shape=jax.ShapeDtypeStruct((t, groups * vw), BF16),
        scratch_shapes=_attn_scratch(n_total, 2 * pps * bps, tk, tq),
        compiler_params=_params(("arbitrary", "arbitrary", "arbitrary")),
        name="pair_attention",
    )(*args)


NB_QROWS = 4
NB_KROWS = 12
NB_TQ = NB_QROWS * GRID_W
NB_CROWS = 4
NB_TK = NB_CROWS * GRID_W
NB_INVALID = 2 * NA_ROWS - 1


def _nb_kernel(q_ref, kc_ref, vc_ref, k_ref, v_ref, tab_ref, o_ref, kcs_ref, vtc_ref, vt_ref, bias_ref,
               s_ref, p_ref):
    i = pl.program_id(2)
    rows = k_ref.shape[0] // GRID_W
    half = NA_ROWS // 2
    a = i * NB_QROWS
    base_blk = jnp.clip(i - half // NB_QROWS, 0, (rows - NB_KROWS) // NB_QROWS)
    base = base_blk * NB_QROWS
    base_chunk = base_blk * (NB_QROWS // NB_CROWS)
    tq = q_ref.shape[0]

    @pl.when(i == 0)
    def _():
        l, width = kcs_ref.shape
        kcs_ref[...] = kc_ref[...].reshape(l, width).astype(kcs_ref.dtype)
        vc = vc_ref[...].reshape(l, width)
        for ref, src in ((vtc_ref, vc), (vt_ref, v_ref)):
            for c in range(ref.shape[1]):
                for pp in range(width // LANES):
                    tiles = _vt_tiles(src[c * NB_TK:(c + 1) * NB_TK, pp * LANES:(pp + 1) * LANES])
                    for h in range(2):
                        ref[2 * pp + h, c] = tiles[h]

    def unclipped(first_row):
        return jnp.logical_and(first_row - half >= 0, first_row + NB_QROWS - 1 - half <= rows - NA_ROWS)

    @pl.when(jnp.logical_not(jnp.logical_and(unclipped(a), unclipped(a - NB_QROWS))))
    def _():
        lo_tile = lax.broadcasted_iota(jnp.int32, (GRID_W, LANES), 1) < HEAD_DIM
        for h in range(bias_ref.shape[0]):
            for j in range(NB_KROWS):
                kr = base + j
                for ip in range(NB_QROWS // 2):
                    halves = []
                    for qi in (2 * ip, 2 * ip + 1):
                        qr = a + qi
                        r0 = jnp.clip(qr - half, 0, rows - NA_ROWS)
                        valid = jnp.logical_and(kr >= r0, kr < r0 + NA_ROWS)
                        halves.append(tab_ref[h, jnp.where(valid, kr - qr + NA_ROWS - 1, NB_INVALID)])
                    bias_ref[h, j * GRID_W:(j + 1) * GRID_W, ip * LANES:(ip + 1) * LANES] = (
                        jnp.where(lo_tile, halves[0], halves[1]))

    top = lax.broadcasted_iota(jnp.int32, (LANES, tq), 0) < HEAD_DIM
    pps = q_ref.shape[1] // LANES
    q_ts = []
    for pp in range(pps):
        both = q_ref[:, pp * LANES:(pp + 1) * LANES].astype(F32).T
        q_ts += [jnp.where(top, both, 0.0).astype(BF16), jnp.where(top, 0.0, both).astype(BF16)]
    pair_lanes = lambda i_: slice((i_ // 2) * LANES, (i_ // 2 + 1) * LANES)
    chunks = []
    for c in range(vtc_ref.shape[1]):
        chunks.append((lambda i_, c=c: kcs_ref[c * NB_TK:(c + 1) * NB_TK, pair_lanes(i_)],
                       lambda i_, c=c: vtc_ref[i_, c], None))
    for c in range(NB_KROWS // NB_CROWS):
        start = pl.multiple_of((base + c * NB_CROWS) * GRID_W, NB_TK)
        chunks.append((lambda i_, start=start: k_ref[pl.ds(start, NB_TK), pair_lanes(i_)].astype(BF16),
                       lambda i_, c=c: vt_ref[i_, base_chunk + c],
                       lambda i_, c=c: bias_ref[i_, c * NB_TK:(c + 1) * NB_TK, :]))
    outs = _pipeline(q_ts, chunks, s_ref, p_ref)
    for pp in range(pps):
        o_ref[:, pp * LANES:(pp + 1) * LANES] = (
            jnp.concatenate(outs[2 * pp:2 * pp + 2], axis=0).T.astype(o_ref.dtype))


def _nb_attention(q, k, v, kctx, vctx, table, batch, pps=4):
    t, w = q.shape
    s = t // batch
    pw = LANES * pps
    groups = w // pw
    heads = 2 * pps
    tq = NB_TQ
    nq = s // tq
    l = kctx.shape[1]
    assert l % NB_TK == 0 and (s // GRID_W) % NB_QROWS == 0 and kctx.shape[2] == heads
    return pl.pallas_call(
        _nb_kernel,
        grid=(batch, groups, nq),
        in_specs=[
            pl.BlockSpec((tq, pw), lambda b, p, i: (b * nq + i, p)),
            pl.BlockSpec((None, l, heads, HEAD_DIM), lambda b, p, i: (b, 0, 0, 0)),
            pl.BlockSpec((None, l, heads, HEAD_DIM), lambda b, p, i: (b, 0, 0, 0)),
            pl.BlockSpec((None, s, pw), lambda b, p, i: (b, 0, p)),
            pl.BlockSpec((None, s, pw), lambda b, p, i: (b, 0, p)),
            pl.BlockSpec((heads, NB_INVALID + 1, GRID_W, LANES), lambda b, p, i: (p, 0, 0, 0)),
        ],
        out_specs=pl.BlockSpec((tq, pw), lambda b, p, i: (b * nq + i, p)),
        out_shape=jax.ShapeDtypeStruct((t, w), BF16),
        scratch_shapes=[pltpu.VMEM((l, pw), BF16),
                        pltpu.VMEM((heads, l // NB_TK, VT_ROWS, NB_TK), BF16),
                        pltpu.VMEM((heads, s // NB_TK, VT_ROWS, NB_TK), BF16),
                        pltpu.VMEM((heads, NB_KROWS * GRID_W, tq), F32),
                        *_pipe_scratch(heads, NB_TK, tq)],
        compiler_params=_params(("arbitrary", "arbitrary", "arbitrary")),
        name="nb_attention",
    )(q, kctx, vctx, k, v, table)


def _nb_bias_table(rpb):
    h, n_dr, n_dc = rpb.shape
    qc = jnp.arange(GRID_W)[None, :]
    kc = jnp.arange(GRID_W)[:, None]
    c0 = jnp.clip(qc - NA_COLS // 2, 0, GRID_W - NA_COLS)
    inwin = jnp.logical_and(kc >= c0, kc < c0 + NA_COLS)
    period = 2 * GRID_W - 1
    rev = rpb.astype(F32)[..., ::-1]
    seq = jnp.concatenate([rev[..., NA_COLS - 1:], jnp.zeros((h, n_dr, period - n_dc), F32),
                           rev[..., :NA_COLS - 1]], axis=-1)
    rel = jnp.tile(seq, (1, 1, GRID_W))[..., :GRID_W * (period - 1)]
    rel = rel.reshape(h, n_dr, GRID_W, period - 1)[..., :GRID_W]
    t = jnp.where(inwin[None, None], rel * LOG2E, NEG)
    t = jnp.concatenate([t, jnp.full((h, 1, GRID_W, GRID_W), NEG, F32)], axis=1)
    return jnp.concatenate([t, t], axis=-1)


PAD = 8


def _window_sum(pad_ref, tmp_ref, lanes, win, tm):
    def rows(lo, n):
        return pad_ref[PAD + lo:PAD + lo + n, lanes]

    if win <= 4:
        acc = rows(-(win // 2), tm)
        for d in range(-(win // 2) + 1, win // 2):
            acc = acc + rows(d, tm)
        return acc
    ext = win // 2 - 2
    n = tm + 2 * ext
    cur = rows(-ext - 2, n)
    for d in (-1, 0, 1):
        cur = cur + rows(-ext + d, n)
    width = 4
    while width < win:
        tmp_ref[0:n, :] = cur
        h = width // 2
        prev_ext, ext = ext, ext - h
        n = tm + 2 * ext
        lo = prev_ext - ext
        cur = tmp_ref[lo - h:lo - h + n, :] + tmp_ref[lo + h:lo + h + n, :]
        width *= 2
    return cur


def _project_out(x_ref, ta, tb, w_ref, g_ref, gate_ref, o_ref):
    half = ta.shape[1]
    y = (jnp.dot(ta, w_ref[0:half, :], preferred_element_type=F32)
         + jnp.dot(tb, w_ref[half:2 * half, :], preferred_element_type=F32))
    o_ref[...] = x_ref[...] + gate_ref[...] * _rms_lanes(y, g_ref[...])


def _out_kernel(x_ref, a_ref, ga_ref, b_ref, gb_ref, w_ref, g_ref, gate_ref, o_ref):
    ta = (a_ref[...].astype(F32) * _silu(ga_ref[...].astype(F32))).astype(BF16)
    tb = (b_ref[...].astype(F32) * _silu(gb_ref[...].astype(F32))).astype(BF16)
    _project_out(x_ref, ta, tb, w_ref, g_ref, gate_ref, o_ref)


def _out_even_kernel(x_ref, a_ref, ga_ref, u_ref, up_ref, un_ref, gb_ref, map_ref, sc_ref, w_ref, g_ref,
                     gate_ref, o_ref, pad_ref, tmp_ref, *, nseq, seq):
    tm = a_ref.shape[0]
    pos = pl.program_id(0) % nseq
    u = u_ref[...]
    pad_ref[PAD:PAD + tm, :] = u
    pad_ref[0:PAD, :] = jnp.where(pos == 0, 0.0, up_ref[...])
    pad_ref[PAD + tm:PAD + tm + PAD, :] = jnp.where(pos == nseq - 1, 0.0, un_ref[...])
    t = pos * tm + lax.broadcasted_iota(jnp.int32, (tm, LANES), 0)
    ys = []
    for g, win in enumerate(POOL_WINDOWS):
        lanes = slice(g * LANES, (g + 1) * LANES)
        acc = _window_sum(pad_ref, tmp_ref, lanes, win, tm)
        cnt = jnp.minimum(t + win // 2, seq) - jnp.maximum(t - win // 2, 0)
        diff = acc / cnt.astype(F32) - u[:, lanes]
        ys.append(jnp.dot(diff.astype(BF16), map_ref[g].astype(BF16), preferred_element_type=F32))
    o_b = jnp.concatenate(ys, axis=1) * sc_ref[...]
    ta = (a_ref[...].astype(F32) * _silu(ga_ref[...].astype(F32))).astype(BF16)
    tb = (o_b * _silu(gb_ref[...].astype(F32))).astype(BF16)
    _project_out(x_ref, ta, tb, w_ref, g_ref, gate_ref, o_ref)


def _out_even(x, a, ga, u, gb, b_map, b_scale, w, g_post, gate, rows_per_mod, seq, tm=1024):
    t, d = x.shape
    half = a.shape[1]
    tm = min(tm, seq)
    nseq = seq // tm
    br = pl.BlockSpec((tm, half), lambda i: (i, 0))
    halo_blocks = tm // PAD
    prev = pl.BlockSpec((PAD, half), lambda i: (jnp.maximum(i * halo_blocks - 1, 0), 0))
    nxt = pl.BlockSpec((PAD, half), lambda i: (jnp.minimum((i + 1) * halo_blocks, t // PAD - 1), 0))
    full = lambda arr: pl.BlockSpec(arr.shape, lambda i: (0,) * arr.ndim)
    sc = b_scale.reshape(1, half)
    gate, gate_spec = _mod_operand(gate, d, tm, rows_per_mod)
    return pl.pallas_call(
        functools.partial(_out_even_kernel, nseq=nseq, seq=seq),
        grid=(t // tm,),
        in_specs=[
            pl.BlockSpec((tm, d), lambda i: (i, 0)),
            br, br, br, prev, nxt, br, full(b_map), full(sc), full(w), full(g_post), gate_spec,
        ],
        out_specs=pl.BlockSpec((tm, d), lambda i: (i, 0)),
        out_shape=jax.ShapeDtypeStruct((t, d), F32),
        scratch_shapes=[pltpu.VMEM((tm + 2 * PAD, half), F32), pltpu.VMEM((tm + 2 * PAD, LANES), F32)],
        compiler_params=_params(("arbitrary",)),
        name="out_proj_even",
    )(x, a, ga, u, u, u, gb, b_map, sc, w, g_post, gate)


def _out_proj(x, a, ga, b, gb, w, g_post, gate, rows_per_mod, tm=1024):
    t, d = x.shape
    half = a.shape[1]
    tm = min(tm, t)
    br = pl.BlockSpec((tm, half), lambda i: (i, 0))
    gate, gate_spec = _mod_operand(gate, d, tm, rows_per_mod)
    return pl.pallas_call(
        _out_kernel,
        grid=(t // tm,),
        in_specs=[
            pl.BlockSpec((tm, d), lambda i: (i, 0)),
            br, br, br, br,
            pl.BlockSpec((2 * half, d), lambda i: (0, 0)),
            pl.BlockSpec((1, d), lambda i: (0, 0)),
            gate_spec,
        ],
        out_specs=pl.BlockSpec((tm, d), lambda i: (i, 0)),
        out_shape=jax.ShapeDtypeStruct((t, d), F32),
        compiler_params=_params(("arbitrary",)),
        name="out_proj",
    )(x, a, ga, b, gb, w, g_post, gate)


def _kvup_kernel(c_ref, pe_ref, wk_ref, wv_ref, k_ref, v_ref):
    cb = c_ref[...].astype(BF16)
    heads = k_ref.shape[1] // LANES
    k = jnp.dot(cb, wk_ref[...], preferred_element_type=F32)
    k = k + jnp.concatenate([pe_ref[...]] * heads, axis=1)
    k_ref[...] = k.astype(k_ref.dtype)
    v_ref[...] = jnp.dot(cb, wv_ref[...], preferred_element_type=F32).astype(v_ref.dtype)


def _kvup(ckv, pe128, wk, wv, tm=512):
    t, r = ckv.shape
    tm = min(tm, t)
    nk, nv = wk.shape[1], wv.shape[1]
    row = lambda width: pl.BlockSpec((tm, width), lambda i: (i, 0))
    full = lambda a: pl.BlockSpec(a.shape, lambda i: (0, 0))
    return pl.pallas_call(
        _kvup_kernel,
        grid=(t // tm,),
        in_specs=[row(r), row(LANES), full(wk), full(wv)],
        out_specs=[row(nk), row(nv)],
        out_shape=[jax.ShapeDtypeStruct((t, nk), BF16), jax.ShapeDtypeStruct((t, nv), BF16)],
        compiler_params=_params(("arbitrary",)),
        name="kv_up",
    )(ckv, pe128, wk, wv)


def kernel(x_prompt, x_sample, cache_a_k, cache_a_v, cache_c_k, cache_c_v, cache_d_ckv, cache_d_kpe,
           c, c_ctx, w_mod, b_mod, g_pre, g_post, w_in_e, a_q_norm, a_k_norm, b_map, b_scale, w_out_e,
           w_in_o, c_rpb, d_q_norm, d_w_uq, d_kv_norm, d_w_ukv, w_out_o):
    bp, sp, d = x_prompt.shape
    bs, ss, _ = x_sample.shape
    past = cache_a_k.shape[2]
    depth = w_mod.shape[0]
    tp, ts = bp * sp, bs * ss
    branch = d // 2
    d_heads = 8
    d_nope, d_rope, d_v = 64, 32, 64
    dqk = d_nope + d_rope
    q_lora, kv_lora = d_q_norm.shape[1], d_kv_norm.shape[1]
    kv_w = a_k_norm.shape[1] * 2

    xp = x_prompt.reshape(tp, d)
    xs = x_sample.reshape(ts, d)

    cond = jnp.concatenate([c_ctx[None, :], c, jnp.zeros((8 - 1 - bs, d), F32)], axis=0)
    mods = _mod_call(cond, w_mod, b_mod)

    mod_table = mods.reshape(depth * cond.shape[0] * MOD_PARTS, 1, d)

    def mod_rows(layer, part, lo_row):
        return mod_table, layer * cond.shape[0] + lo_row, part

    rope_a = _axial_tables_rows(ss, HEAD_DIM)
    rope_d = _axial_tables_rows(ss, d_rope)
    rope_d_lanes = _axial_tables_lanes(ss, d_rope, d_nope)

    outs = {name: [] for name in ("a_k", "a_v", "c_k", "c_v", "d_ckv", "d_kpe")}
    for layer in range(depth):
        j = layer // 2
        g_in = g_pre[layer].reshape(1, d)
        g_out = g_post[layer].reshape(1, d)
        paths = (("p", xp, bp, sp, 0, 1, tp), ("s", xs, bs, ss, 1, bs, ss))
        if layer % 2 == 0:
            w_in = w_in_e[j].astype(BF16)
            w_out = w_out_e[j].astype(BF16)
            o0 = branch
            new_x = []
            for name, x, nb, seq, row0, nrows, rpm in paths:
                shift, scale, gate = (mod_rows(layer, part, row0) for part in range(MOD_PARTS))
                v_dt = F32 if name == "p" else BF16
                segs = [(0, branch, F32), (o0, kv_w, F32), (o0 + kv_w, kv_w, v_dt),
                        (o0 + 2 * kv_w, branch, BF16), (o0 + 2 * kv_w + branch, branch, F32),
                        (o0 + 2 * kv_w + 2 * branch, branch, BF16)]
                q, k, v, ga, ub, gb = _norm_mm(x, g_in, scale, shift, w_in, segs, rpm, name="in_proj_even")
                k3, v3 = k.reshape(nb, seq, kv_w), v.reshape(nb, seq, kv_w)
                if name == "p":
                    o_a, kn, vn = _gqa_attention(q, [(k3, v3)], a_q_norm[j], a_k_norm[j], None, nb, tq=256,
                                                 emit_k=True)
                    outs["a_k"].append(kn)
                    outs["a_v"].append(vn)
                else:
                    kv = [(cache_a_k[:, j], cache_a_v[:, j]), (k3, v3)]
                    o_a = _gqa_attention(q, kv, a_q_norm[j], a_k_norm[j], rope_a, nb, tq=256, emit_k=False)
                new_x.append(_out_even(x, o_a, ga, ub, gb, b_map[j], b_scale[j], w_out, g_out, gate, rpm, seq))
            xp, xs = new_x
        else:
            wi = w_in_o[j].astype(BF16)
            off_kpe = 4 * branch + q_lora + kv_lora
            w_in = jnp.concatenate([wi[:, :off_kpe], jnp.zeros((d, d_nope), BF16),
                                    wi[:, off_kpe:off_kpe + d_rope],
                                    jnp.zeros((d, LANES - dqk), BF16),
                                    wi[:, off_kpe + d_rope:]], axis=1)
            w_out = w_out_o[j].astype(BF16)
            w_uq = jnp.pad(d_w_uq[j].reshape(q_lora, d_heads, dqk),
                           ((0, 0), (0, 0), (0, LANES - dqk))).reshape(q_lora, d_heads * LANES).astype(BF16)
            ukv = d_w_ukv[j].reshape(kv_lora, d_heads, d_nope + d_v)
            w_uk = jnp.pad(ukv[:, :, :d_nope], ((0, 0), (0, 0), (0, LANES - d_nope))
                           ).reshape(kv_lora, d_heads * LANES).astype(BF16)
            w_uv = ukv[:, :, d_nope:].reshape(kv_lora, d_heads * d_v).astype(BF16)
            qn_g = d_q_norm[j].reshape(1, q_lora)
            kvn_g = d_kv_norm[j].reshape(1, kv_lora)
            table = _nb_bias_table(c_rpb[j])
            new_x = []
            for name, x, nb, seq, row0, nrows, rpm in paths:
                shift, scale, gate = (mod_rows(layer, part, row0) for part in range(MOD_PARTS))
                latent = name == "s"
                qc, kc, vc, gc, gd, qd, ckv_n, kpe, k_d, v_d, *cache_kv = _in_odd(
                    x, g_in, scale, shift, w_in, qn_g, kvn_g, w_uq, w_uk, w_uv,
                    rope_d_lanes, rpm, seq, d_rope // 4 if latent else 0,
                    not latent, F32 if latent else BF16, branch,
                    LOG2E * HEAD_DIM ** -0.5, LOG2E * dqk ** -0.5)
                kc3, vc3 = kc.reshape(nb, seq, branch), vc.reshape(nb, seq, branch)
                kd3, vd3 = k_d.reshape(nb, seq, -1), v_d.reshape(nb, seq, -1)
                if not latent:
                    outs["c_k"].append(cache_kv[0])
                    outs["c_v"].append(cache_kv[1])
                    outs["d_ckv"].append(ckv_n)
                    outs["d_kpe"].append(kpe[:, d_nope:dqk])
                    all_pairs = branch // LANES
                    o_c = _pair_attention(qc, [(kc3, vc3)], nb, HEAD_DIM, tq=256, pps=all_pairs, bps=2)
                    o_d = _pair_attention(qd, [(kd3, vd3)], nb, 2 * HEAD_DIM, tq=256, pps=all_pairs, bps=2)
                else:
                    o_c = _nb_attention(qc, kc3, vc3, cache_c_k[:, j], cache_c_v[:, j], table, nb)
                    pe_ctx = jnp.pad(cache_d_kpe[:, j].reshape(nb * past, d_rope),
                                     ((0, 0), (d_nope, LANES - dqk)))
                    k_ctx, v_ctx = _kvup(cache_d_ckv[:, j].reshape(nb * past, kv_lora), pe_ctx, w_uk, w_uv)
                    kv = [(k_ctx.reshape(nb, past, -1), v_ctx.reshape(nb, past, -1)), (kd3, vd3)]
                    o_d = _pair_attention(qd, kv, nb, 2 * HEAD_DIM, tq=512, tables=rope_d, pps=2)
                new_x.append(_out_proj(x, o_c, gc, o_d, gd, w_out, g_out, gate, rpm))
            xp, xs = new_x

    kvh = kv_w // HEAD_DIM
    ch = branch // HEAD_DIM

    def stacked(name, *tail):
        return jnp.stack([a.reshape(bp, sp, *tail) for a in outs[name]], axis=1)

    return (xp.reshape(bp, sp, d), xs.reshape(bs, ss, d),
            stacked("a_k", kvh, HEAD_DIM), stacked("a_v", kvh, HEAD_DIM),
            stacked("c_k", ch, HEAD_DIM), stacked("c_v", ch, HEAD_DIM),
            stacked("d_ckv", kv_lora), stacked("d_kpe", d_rope))
```
